```python
import math
import jax, jax.numpy as jnp
from jax import lax
import numpy as np

D_MODEL = 1024
BATCH = 4
SEQ = 8192
DEPTH = 4

HEAD_DIM = 64
N_HEADS = D_MODEL // HEAD_DIM
MIX_WIDTH = N_HEADS * HEAD_DIM
N_HEADS_MOBA = N_HEADS // 4
N_HEADS_SB = N_HEADS // 4
N_HEADS_DIL = N_HEADS - N_HEADS_MOBA - N_HEADS_SB
MOBA_BLOCK = 256
MOBA_TOPK = 3
MOBA_QCHUNK = 32
SB_QBLOCK = 128
DIL_CONFIGS = ((128, 1), (512, 4), (2048, 16))
N_GROUPS = 4
EXPERTS_PER_GROUP = 4
MOE_TOPK = 2
D_EXPERT = D_MODEL // 4
NORM_EPS = 1e-6
NEG_INF = -1e30

kernel_name = "hymba_moba_stickbreak_dilated_hmoe"


def rms_norm(x, g):
    xf = x.astype(jnp.float32)
    y = xf * lax.rsqrt(jnp.mean(xf * xf, axis=-1, keepdims=True) + NORM_EPS)
    return (y * g.astype(jnp.float32)).astype(x.dtype)


def alibi_slopes(n):
    return jnp.asarray(2.0 ** (-8.0 * np.arange(1, n + 1, dtype=np.float32) / n), dtype=jnp.float32)


def moba_attention(q, k, v, slopes):
    B, H, S, hd = q.shape
    blk, qc = MOBA_BLOCK, MOBA_QCHUNK
    n_blk = -(-S // blk)
    s_pad = n_blk * blk
    pad = ((0, 0), (0, 0), (0, s_pad - S), (0, 0))
    q, k, v = jnp.pad(q, pad), jnp.pad(k, pad), jnp.pad(v, pad)
    k_blk = k.reshape(B, H, n_blk, blk, hd)
    v_blk = v.reshape(B, H, n_blk, blk, hd)
    k_mean = jnp.mean(k_blk.astype(jnp.float32), axis=3)
    n_sel = max(1, min(MOBA_TOPK, n_blk - 1))
    scale = hd ** -0.5
    n_chunks = s_pad // qc
    q_chunks = q.reshape(B, H, n_chunks, qc, hd).transpose(2, 0, 1, 3, 4)
    b_ix = jnp.arange(B)[:, None, None, None]
    h_ix = jnp.arange(H)[None, :, None, None]
    m = slopes[None, :, None, None]

    def chunk(args):
        q_c, c = args
        t0 = c * qc
        own = t0 // blk
        t = t0 + jnp.arange(qc)
        qf = q_c.astype(jnp.float32)
        gate = jnp.einsum('bhqd,bhnd->bhqn', qf, k_mean)
        gate = jnp.where(jnp.arange(n_blk) < own, gate, NEG_INF)
        _, sel = lax.top_k(gate, n_sel)
        sel_ok = sel < own
        k_sel = k_blk[b_ix, h_ix, sel].astype(jnp.float32)
        v_sel = v_blk[b_ix, h_ix, sel].astype(jnp.float32)
        pos_sel = sel[..., None] * blk + jnp.arange(blk)
        s_sel = jnp.einsum('bhqd,bhqnkd->bhqnk', qf, k_sel) * scale
        s_sel = s_sel - m[..., None] * (t[:, None, None] - pos_sel).astype(jnp.float32)
        s_sel = jnp.where(sel_ok[..., None], s_sel, NEG_INF).reshape(B, H, qc, n_sel * blk)
        k_own = lax.dynamic_slice_in_dim(k, own * blk, blk, axis=2).astype(jnp.float32)
        v_own = lax.dynamic_slice_in_dim(v, own * blk, blk, axis=2).astype(jnp.float32)
        dist = t[:, None] - (own * blk + jnp.arange(blk))[None, :]
        s_own = jnp.einsum('bhqd,bhkd->bhqk', qf, k_own) * scale - m * dist.astype(jnp.float32)
        s_own = jnp.where(dist >= 0, s_own, NEG_INF)
        p = jax.nn.softmax(jnp.concatenate([s_sel, s_own], axis=-1), axis=-1)
        p_sel = p[..., :n_sel * blk].reshape(B, H, qc, n_sel, blk)
        p_own = p[..., n_sel * blk:]
        o = (jnp.einsum('bhqnk,bhqnkd->bhqd', p_sel, v_sel)
             + jnp.einsum('bhqk,bhkd->bhqd', p_own, v_own))
        return o.astype(v.dtype)

    o = lax.map(chunk, (q_chunks, jnp.arange(n_chunks)))
    return o.transpose(1, 2, 0, 3, 4).reshape(B, H, s_pad, hd)[:, :, :S]


def stick_breaking_attention(q, k, v):
    B, H, S, hd = q.shape
    qb = SB_QBLOCK
    n_qb = S // qb
    scale = hd ** -0.5
    k_pos = jnp.arange(S)
    q_blocks = q.reshape(B, H, n_qb, qb, hd).transpose(2, 0, 1, 3, 4)
    kf = k.astype(jnp.float32)
    vf = v.astype(jnp.float32)

    def block(args):
        q_b, i = args
        t = i * qb + jnp.arange(qb)
        z = jnp.einsum('bhqd,bhkd->bhqk', q_b.astype(jnp.float32), kf) * scale
        past = k_pos[None, :] < t[:, None]
        log_not = jnp.where(past, jax.nn.log_sigmoid(-z), 0.0)
        tail = lax.cumsum(log_not, axis=3, reverse=True) - log_not
        a = jnp.where(past, jnp.exp(jax.nn.log_sigmoid(z) + tail), 0.0)
        return jnp.einsum('bhqk,bhkd->bhqd', a, vf).astype(v.dtype)

    o = lax.map(block, (q_blocks, jnp.arange(n_qb)))
    return o.transpose(1, 2, 0, 3, 4).reshape(B, H, S, hd)


def dilated_branch(q, k, v, slopes, window, dilation):
    B, H, S, hd = q.shape
    n = window // dilation
    L = S // dilation
    n_blk = -(-L // n)
    l_pad = n_blk * n

    def to_blocks(t):
        t = t.reshape(B, H, L, dilation, hd).transpose(0, 1, 3, 2, 4)
        t = jnp.pad(t, ((0, 0), (0, 0), (0, 0), (0, l_pad - L), (0, 0)))
        return t.reshape(B, H, dilation, n_blk, n, hd)

    def with_prev(t):
        prev = jnp.pad(t, ((0, 0), (0, 0), (0, 0), (1, 0), (0, 0), (0, 0)))[:, :, :, :-1]
        return jnp.concatenate([prev, t], axis=4)

    qb = to_blocks(q).astype(jnp.float32)
    kb = with_prev(to_blocks(k)).astype(jnp.float32)
    vb = with_prev(to_blocks(v)).astype(jnp.float32)
    delta = jnp.arange(n)[:, None] + n - jnp.arange(2 * n)[None, :]
    k_sub = jnp.arange(n_blk)[:, None] * n - n + jnp.arange(2 * n)[None, :]
    valid = ((delta >= 0) & (delta <= n))[None] & (k_sub >= 0)[:, None, :]
    s = jnp.einsum('bhrnqd,bhrnkd->bhrnqk', qb, kb) * (hd ** -0.5)
    s = s - slopes[:, None, None, None, None] * (delta * dilation).astype(jnp.float32)
    s = jnp.where(valid, s, NEG_INF)
    s_max = jnp.max(s, axis=-1, keepdims=True)
    p = jnp.exp(s - s_max)
    den = jnp.sum(p, axis=-1, keepdims=True)
    o = jnp.einsum('bhrnqk,bhrnkd->bhrnqd', p, vb) / den
    lse = (s_max + jnp.log(den))[..., 0]

    def from_blocks(t):
        t = t.reshape((B, H, dilation, l_pad) + t.shape[5:])[:, :, :, :L]
        t = jnp.moveaxis(t, 2, 3)
        return t.reshape((B, H, S) + t.shape[4:])

    return from_blocks(o), from_blocks(lse)


def dilated_mixture(q, k, v, slopes):
    outs, lses = [], []
    for window, dilation in DIL_CONFIGS:
        o, lse = dilated_branch(q, k, v, slopes, window, dilation)
        outs.append(o)
        lses.append(lse)
    wts = jax.nn.softmax(jnp.stack(lses), axis=0)
    o = jnp.einsum('cbhs,cbhsd->bhsd', wts, jnp.stack(outs))
    return o.astype(v.dtype)


def mixer_sublayer(x, ln_g, w_in, mix_g, w_out):
    B, S, _ = x.shape
    h = rms_norm(x, ln_g)
    qkv = (h @ w_in).reshape(B, S, 3, N_HEADS, HEAD_DIM).transpose(2, 0, 3, 1, 4)
    q, k, v = qkv[0], qkv[1], qkv[2]
    a = N_HEADS_MOBA
    b = N_HEADS_MOBA + N_HEADS_SB
    o_a = moba_attention(q[:, :a], k[:, :a], v[:, :a], alibi_slopes(N_HEADS_MOBA))
    o_b = stick_breaking_attention(q[:, a:b], k[:, a:b], v[:, a:b])
    o_c = dilated_mixture(q[:, b:], k[:, b:], v[:, b:], alibi_slopes(N_HEADS_DIL))

    def merge(o):
        return o.transpose(0, 2, 1, 3).reshape(B, S, -1)

    wa, wb = a * HEAD_DIM, b * HEAD_DIM
    y = jnp.concatenate([rms_norm(merge(o_a), mix_g[:wa]),
                         rms_norm(merge(o_b), mix_g[wa:wb]),
                         rms_norm(merge(o_c), mix_g[wb:])], axis=-1)
    return x + (y @ w_out).astype(x.dtype)


def hierarchical_moe(h, w_gr, b_gr, w_er, b_er, w_gate, w_up, w_down):
    f32 = jnp.float32
    hf = h.astype(f32)
    g_prob = jax.nn.softmax(hf @ w_gr.astype(f32) + b_gr.astype(f32), axis=-1)
    g_w, g_idx = lax.top_k(g_prob, 1)
    g_onehot = jax.nn.one_hot(g_idx[..., 0], N_GROUPS, dtype=f32)
    e_logits = jnp.einsum('bsd,gde->bsge', hf, w_er.astype(f32)) + b_er.astype(f32)
    e_logits = jnp.einsum('bsge,bsg->bse', e_logits, g_onehot)
    e_val, e_idx = lax.top_k(e_logits, MOE_TOPK)
    e_w = jax.nn.softmax(e_val, axis=-1) * g_w
    within = jnp.sum(jax.nn.one_hot(e_idx, EXPERTS_PER_GROUP, dtype=f32) * e_w[..., None], axis=-2)
    combine = g_onehot[..., None] * within[:, :, None, :]
    y = jnp.zeros(hf.shape, f32)
    for g in range(N_GROUPS):
        gate = jnp.einsum('bsd,edf->bsef', h, w_gate[g])
        up = jnp.einsum('bsd,edf->bsef', h, w_up[g])
        act = jax.nn.silu(gate) * up * combine[:, :, g, :, None].astype(h.dtype)
        y = y + jnp.einsum('bsef,efd->bsd', act, w_down[g]).astype(f32)
    return y.astype(h.dtype)


def setup_inputs(seed: int = 0) -> dict:
    key = jax.random.key(seed)
    ks = jax.random.split(key, 16)
    f32 = jnp.float32
    D = D_MODEL

    def nrm(k, shape, scale):
        return jax.random.normal(k, shape, f32) * scale

    return {
        "x": nrm(ks[0], (BATCH, SEQ, D), 1.0),
        "ln1_g": 1.0 + nrm(ks[1], (DEPTH, D), 0.02),
        "w_in": nrm(ks[2], (DEPTH, D, 3 * MIX_WIDTH), D ** -0.5),
        "mix_norm_g": 1.0 + nrm(ks[3], (DEPTH, MIX_WIDTH), 0.02),
        "w_out": nrm(ks[4], (DEPTH, MIX_WIDTH, D), MIX_WIDTH ** -0.5),
        "ln2_g": 1.0 + nrm(ks[5], (DEPTH, D), 0.02),
        "w_group_router": nrm(ks[6], (DEPTH, D, N_GROUPS), D ** -0.5),
        "b_group_router": nrm(ks[7], (DEPTH, N_GROUPS), 0.01),
        "w_expert_router": nrm(ks[8], (DEPTH, N_GROUPS, D, EXPERTS_PER_GROUP), D ** -0.5),
        "b_expert_router": nrm(ks[9], (DEPTH, N_GROUPS, EXPERTS_PER_GROUP), 0.01),
        "w_gate": nrm(ks[10], (DEPTH, N_GROUPS, EXPERTS_PER_GROUP, D, D_EXPERT), D ** -0.5),
        "w_up": nrm(ks[11], (DEPTH, N_GROUPS, EXPERTS_PER_GROUP, D, D_EXPERT), D ** -0.5),
        "w_down": nrm(ks[12], (DEPTH, N_GROUPS, EXPERTS_PER_GROUP, D_EXPERT, D), D_EXPERT ** -0.5),
        "final_norm_g": 1.0 + nrm(ks[13], (D,), 0.02),
    }


def reference(x, ln1_g, w_in, mix_norm_g, w_out, ln2_g, w_group_router, b_group_router,
              w_expert_router, b_expert_router, w_gate, w_up, w_down, final_norm_g):
    for l in range(DEPTH):
        x = mixer_sublayer(x, ln1_g[l], w_in[l], mix_norm_g[l], w_out[l])
        x = x + hierarchical_moe(rms_norm(x, ln2_g[l]), w_group_router[l], b_group_router[l],
                                 w_expert_router[l], b_expert_router[l],
                                 w_gate[l], w_up[l], w_down[l])
    return rms_norm(x, final_norm_g)
```

```python
import functools

import jax
import jax.numpy as jnp
from jax import lax
from jax.experimental import pallas as pl
from jax.experimental.pallas import tpu as pltpu

F32 = jnp.float32
BF16 = jnp.bfloat16

D_MODEL = 1024
HEAD_DIM = 64
N_HEADS = 16
LANES = 128
HEADS_PER_BLOCK = LANES // HEAD_DIM
N_HEADS_MOBA = 4
N_HEADS_SB = 4
N_HEADS_DIL = 8
MOBA_BLOCK = 256
MOBA_TOPK = 3
DIL_CONFIGS = ((128, 1), (512, 4), (2048, 16))
DIL_N = 128
N_GROUPS = 4
EXPERTS_PER_GROUP = 4
N_EXPERTS = N_GROUPS * EXPERTS_PER_GROUP
D_EXPERT = 256
NORM_EPS = 1e-6
NEG_INF = -1e30
SCALE = HEAD_DIM ** -0.5

Q_BLK0 = 0
K_BLK0 = D_MODEL // LANES
V_BLK0 = 2 * D_MODEL // LANES
ROW_BLKS = 3 * D_MODEL // LANES
SB_BLK = N_HEADS_MOBA // HEADS_PER_BLOCK
DIL_BLK = (N_HEADS_MOBA + N_HEADS_SB) // HEADS_PER_BLOCK

VMEM_LIMIT = 56 * 1024 * 1024


def _params(n_axes, vmem=VMEM_LIMIT):
    return pltpu.CompilerParams(dimension_semantics=("arbitrary",) * n_axes,
                                vmem_limit_bytes=vmem)


def _dot_nt(a, b):
    return lax.dot_general(a, b, (((1,), (1,)), ((), ())), preferred_element_type=F32)


def _dot(a, b):
    return jnp.dot(a, b, preferred_element_type=F32)


def _split3(x):
    hi = x.astype(BF16)
    r1 = x - hi.astype(F32)
    mid = r1.astype(BF16)
    lo = (r1 - mid.astype(F32)).astype(BF16)
    return hi, mid, lo


def _split2(x):
    hi = x.astype(BF16)
    lo = (x - hi.astype(F32)).astype(BF16)
    return hi, lo


def _qkv_kernel(x_ref, g_ref, w_ref, o_ref, h_ref):
    @pl.when(pl.program_id(1) == 0)
    def _():
        x = x_ref[...]
        ms = jnp.mean(x * x, axis=-1, keepdims=True)
        h_ref[...] = (x * lax.rsqrt(ms + NORM_EPS) * g_ref[...]).astype(BF16)

    o_ref[...] = _dot(h_ref[...], w_ref[...]).astype(BF16)


def _qkv_proj(x2, g, w, *, tm=512, tn=1024):
    t, d = x2.shape
    n = w.shape[1]
    return pl.pallas_call(
        _qkv_kernel,
        out_shape=jax.ShapeDtypeStruct((t, n), BF16),
        grid=(t // tm, n // tn),
        in_specs=[pl.BlockSpec((tm, d), lambda i, j: (i, 0)),
                  pl.BlockSpec((1, d), lambda i, j: (0, 0)),
                  pl.BlockSpec((d, tn), lambda i, j: (0, j))],
        out_specs=pl.BlockSpec((tm, tn), lambda i, j: (i, j)),
        scratch_shapes=[pltpu.VMEM((tm, d), BF16)],
        compiler_params=_params(2),
        name="qkv_proj",
    )(x2, g.reshape(1, d), w)


_MB_SEL0 = 0
_MB_POS0 = 32
_MB_BLK0 = 64


def _moba_kernel(q_ref, k_ref, v_ref, o_ref, kmean_ref, kx_ref, qaug_ref, m_ref, l_ref, acc_ref,
                 *, n_blk):
    pair = pl.program_id(1)
    qi = pl.program_id(2)
    blk = MOBA_BLOCK
    lane = lax.broadcasted_iota(jnp.int32, (blk, LANES), 1)
    row = lax.broadcasted_iota(jnp.int32, (blk, LANES), 0)
    lane_f = lane.astype(F32)
    slopes = [jnp.where(pair == 0, 2.0 ** (-2 * (hh + 1)), 2.0 ** (-2 * (hh + 3))).astype(F32)
              for hh in range(HEADS_PER_BLOCK)]

    @pl.when(qi == 0)
    def _():
        kmean_ref[...] = jnp.zeros_like(kmean_ref)

        def body(n, c):
            kb = k_ref[0, pl.ds(pl.multiple_of(n * blk, blk), blk), :].astype(F32)
            kmean_ref[pl.ds(n, 1), :] = jnp.sum(kb, axis=0, keepdims=True) * (1.0 / blk)
            return c

        lax.fori_loop(0, n_blk, body, 0)
        kx = jnp.zeros((blk, LANES), F32)
        for hh in range(HEADS_PER_BLOCK):
            kx = jnp.where(lane == _MB_POS0 + 2 * hh, slopes[hh] * ((row // LANES) * LANES).astype(F32), kx)
            kx = jnp.where(lane == _MB_POS0 + 2 * hh + 1, slopes[hh] * (row % LANES).astype(F32), kx)
        kx_ref[...] = kx.astype(BF16)

    q2 = q_ref[0]
    km_parts = _split3(kmean_ref[...])
    for hh in range(HEADS_PER_BLOCK):
        in_head = (lane >= hh * HEAD_DIM) & (lane < (hh + 1) * HEAD_DIM)
        qm = jnp.where(in_head, q2, jnp.zeros_like(q2))
        gate = _dot_nt(qm, km_parts[0]) + _dot_nt(qm, km_parts[1]) + _dot_nt(qm, km_parts[2])
        gate = jnp.where(lane < qi, gate, NEG_INF)
        gate = jnp.where(lane < n_blk, gate, -jnp.inf)
        sel = jnp.zeros((blk, LANES), jnp.bool_)
        for _ in range(MOBA_TOPK):
            gmax = jnp.max(gate, axis=1, keepdims=True)
            first = jnp.min(jnp.where(gate == gmax, lane_f, float(LANES)), axis=1, keepdims=True)
            pick = lane_f == first
            sel = sel | pick
            gate = jnp.where(pick, -jnp.inf, gate)
        sel = sel & (lane < qi)
        extra = jnp.where(sel, 0.0, NEG_INF)
        extra = jnp.where(lane >= n_blk, 0.0, extra)
        extra = jnp.where((lane == _MB_POS0 + 2 * hh) | (lane == _MB_POS0 + 2 * hh + 1), 1.0, extra)
        blk_lane = lane - (_MB_BLK0 + 32 * hh)
        extra = jnp.where((blk_lane >= 0) & (blk_lane < 32),
                          slopes[hh] * (blk_lane * blk).astype(F32), extra)
        qaug_ref[hh] = jnp.concatenate([qm * jnp.asarray(SCALE, BF16), extra.astype(BF16)], axis=1)
        m_ref[hh] = jnp.full((blk, 1), -jnp.inf, F32)
        l_ref[hh] = jnp.zeros((blk, 1), F32)
        acc_ref[hh] = jnp.zeros((blk, LANES), F32)

    col2 = lax.broadcasted_iota(jnp.int32, (blk, blk), 1)
    row2 = lax.broadcasted_iota(jnp.int32, (blk, blk), 0)
    causal = col2 <= row2

    def attend(j, is_own):
        start = pl.multiple_of(j * blk, blk)
        kj = k_ref[0, pl.ds(start, blk), :]
        vj = v_ref[0, pl.ds(start, blk), :]
        ind = (lane % 32 == j) & (lane >= _MB_BLK0) if is_own else \
              (lane % 32 == j) & ((lane < 32) | (lane >= _MB_BLK0))
        kx = jnp.where(ind, jnp.ones((blk, LANES), BF16), kx_ref[...])
        k_aug = jnp.concatenate([kj, kx], axis=1)
        for hh in range(HEADS_PER_BLOCK):
            s = _dot_nt(qaug_ref[hh], k_aug)
            if is_own:
                s = jnp.where(causal, s, NEG_INF)
            m_old = m_ref[hh]
            m_new = jnp.maximum(m_old, jnp.max(s, axis=1, keepdims=True))
            alpha = jnp.exp(m_old - m_new)
            p = jnp.exp(s - m_new)
            l_ref[hh] = alpha * l_ref[hh] + jnp.sum(p, axis=1, keepdims=True)
            acc_ref[hh] = alpha * acc_ref[hh] + _dot(p.astype(BF16), vj)
            m_ref[hh] = m_new

    attend(qi, True)

    def body(j, c):
        attend(j, False)
        return c

    lax.fori_loop(0, qi, body, 0)

    o0 = acc_ref[0] / l_ref[0]
    o1 = acc_ref[1] / l_ref[1]
    o_ref[0] = jnp.where(lane < HEAD_DIM, o0, o1).astype(BF16)


def _moba_attention(qkv):
    b, s, _ = qkv.shape
    blk = MOBA_BLOCK
    n_blk = s // blk
    assert s % blk == 0 and MOBA_TOPK <= n_blk - 1 and n_blk <= 32
    n_pairs = N_HEADS_MOBA // HEADS_PER_BLOCK
    return pl.pallas_call(
        functools.partial(_moba_kernel, n_blk=n_blk),
        out_shape=jax.ShapeDtypeStruct((b, s, N_HEADS_MOBA * HEAD_DIM), BF16),
        grid=(b, n_pairs, n_blk),
        in_specs=[pl.BlockSpec((1, blk, LANES), lambda bi, p, i: (bi, i, Q_BLK0 + p)),
                  pl.BlockSpec((1, s, LANES), lambda bi, p, i: (bi, 0, K_BLK0 + p)),
                  pl.BlockSpec((1, s, LANES), lambda bi, p, i: (bi, 0, V_BLK0 + p))],
        out_specs=pl.BlockSpec((1, blk, LANES), lambda bi, p, i: (bi, i, p)),
        scratch_shapes=[pltpu.VMEM((LANES, LANES), F32),
                        pltpu.VMEM((blk, LANES), BF16),
                        pltpu.VMEM((HEADS_PER_BLOCK, blk, 2 * LANES), BF16),
                        pltpu.VMEM((HEADS_PER_BLOCK, blk, 1), F32),
                        pltpu.VMEM((HEADS_PER_BLOCK, blk, 1), F32),
                        pltpu.VMEM((HEADS_PER_BLOCK, blk, LANES), F32)],
        compiler_params=_params(3),
        name="moba_attn",
    )(qkv, qkv, qkv)


SB_TILE = 256


def _softplus(z):
    return jnp.maximum(z, 0.0) + jnp.log(1.0 + jnp.exp(-jnp.abs(z)))


def _sb_kernel(q_ref, k_ref, v_ref, u_ref, o_ref, carry_ref, acc_ref):
    qi = pl.program_id(2)
    t = SB_TILE
    lane = lax.broadcasted_iota(jnp.int32, (t, LANES), 1)
    q2 = q_ref[0]
    u = u_ref[...]
    qms = []
    for hh in range(HEADS_PER_BLOCK):
        in_head = (lane >= hh * HEAD_DIM) & (lane < (hh + 1) * HEAD_DIM)
        qms.append(jnp.where(in_head, q2, jnp.zeros_like(q2)) * jnp.asarray(SCALE, BF16))

    col2 = lax.broadcasted_iota(jnp.int32, (t, t), 1)
    row2 = lax.broadcasted_iota(jnp.int32, (t, t), 0)
    past = col2 < row2

    def tile(j, diagonal):
        start = pl.multiple_of(j * t, t)
        kj = k_ref[0, pl.ds(start, t), :]
        vj = v_ref[0, pl.ds(start, t), :]
        for hh in range(HEADS_PER_BLOCK):
            z = _dot_nt(qms[hh], kj)
            sp = _softplus(z)
            if diagonal:
                sp = jnp.where(past, sp, 0.0)
            hi, lo = _split2(sp)
            c = _dot(hi, u) + _dot(lo, u)
            if diagonal:
                a = jnp.where(past, jnp.exp(z - c), 0.0)
                acc_ref[hh] = _dot(a.astype(BF16), vj)
                carry_ref[hh] = jnp.sum(sp, axis=1, keepdims=True)
            else:
                carry = carry_ref[hh]
                a = jnp.exp(z - (c + carry))
                acc_ref[hh] += _dot(a.astype(BF16), vj)
                carry_ref[hh] = carry + jnp.sum(sp, axis=1, keepdims=True)

    tile(qi, True)

    def body(jj, c):
        tile(qi - 1 - jj, False)
        return c

    lax.fori_loop(0, qi, body, 0)
    o_ref[0] = jnp.where(lane < HEAD_DIM, acc_ref[0], acc_ref[1]).astype(BF16)


def _sb_attention(qkv):
    b, s, _ = qkv.shape
    t = SB_TILE
    assert s % t == 0
    n_pairs = N_HEADS_SB // HEADS_PER_BLOCK
    u = (lax.broadcasted_iota(jnp.int32, (t, t), 0) >= lax.broadcasted_iota(jnp.int32, (t, t), 1)).astype(BF16)
    return pl.pallas_call(
        _sb_kernel,
        out_shape=jax.ShapeDtypeStruct((b, s, N_HEADS_SB * HEAD_DIM), BF16),
        grid=(b, n_pairs, s // t),
        in_specs=[pl.BlockSpec((1, t, LANES), lambda bi, p, i: (bi, i, Q_BLK0 + SB_BLK + p)),
                  pl.BlockSpec((1, s, LANES), lambda bi, p, i: (bi, 0, K_BLK0 + SB_BLK + p)),
                  pl.BlockSpec((1, s, LANES), lambda bi, p, i: (bi, 0, V_BLK0 + SB_BLK + p)),
                  pl.BlockSpec((t, t), lambda bi, p, i: (0, 0))],
        out_specs=pl.BlockSpec((1, t, LANES), lambda bi, p, i: (bi, i, p)),
        scratch_shapes=[pltpu.VMEM((HEADS_PER_BLOCK, t, 1), F32),
                        pltpu.VMEM((HEADS_PER_BLOCK, t, LANES), F32)],
        compiler_params=_params(3),
        name="sb_attn",
    )(qkv, qkv, qkv, u)


def _dil_kernel(q_ref, k_ref, v_ref, kp_ref, vp_ref, o_ref, lse_ref, bias_ref, *, dilation, n_sub):
    pair = pl.program_id(2)
    i = pl.program_id(3)
    n = DIL_N
    lane = lax.broadcasted_iota(jnp.int32, (n, LANES), 1)

    @pl.when(i == 0)
    def _():
        col = lax.broadcasted_iota(jnp.int32, (n, 2 * n), 1)
        row = lax.broadcasted_iota(jnp.int32, (n, 2 * n), 0)
        delta = row + n - col
        valid = (delta >= 0) & (delta <= n)
        for hh in range(HEADS_PER_BLOCK):
            slope = jnp.asarray(2.0 ** -(hh + 1), F32)
            for p in range(1, N_HEADS_DIL // HEADS_PER_BLOCK):
                slope = jnp.where(pair == p, 2.0 ** -(2 * p + hh + 1), slope)
            bias_ref[hh] = jnp.where(valid, -slope * (delta * dilation).astype(F32), NEG_INF)

    q2 = q_ref[0]
    outs = []
    lses = []
    for sb in range(n_sub):
        qs = q2[sb * n:(sb + 1) * n]
        if sb == 0:
            kk = jnp.concatenate([kp_ref[0], k_ref[0, 0:n]], axis=0)
            vv = jnp.concatenate([vp_ref[0], v_ref[0, 0:n]], axis=0)
        else:
            kk = k_ref[0, (sb - 1) * n:(sb + 1) * n]
            vv = v_ref[0, (sb - 1) * n:(sb + 1) * n]
        o_h = []
        lse_h = []
        for hh in range(HEADS_PER_BLOCK):
            in_head = (lane >= hh * HEAD_DIM) & (lane < (hh + 1) * HEAD_DIM)
            qm = jnp.where(in_head, qs, jnp.zeros_like(qs)) * jnp.asarray(SCALE, BF16)
            s = _dot_nt(qm, kk) + bias_ref[hh]
            if sb == 0:
                colh = lax.broadcasted_iota(jnp.int32, (n, 2 * n), 1)
                s = s + jnp.where(colh < n, jnp.where(i == 0, NEG_INF, 0.0), 0.0)
            m = jnp.max(s, axis=1, keepdims=True)
            p = jnp.exp(s - m)
            den = jnp.sum(p, axis=1, keepdims=True)
            o_h.append(_dot(p.astype(BF16), vv) / den)
            lse_h.append(jnp.broadcast_to(m + jnp.log(den), (n, LANES)))
        outs.append(jnp.where(lane < HEAD_DIM, o_h[0], o_h[1]))
        lses.append(jnp.where(lane < HEAD_DIM, lse_h[0], lse_h[1]))
    o_ref[0] = jnp.concatenate(outs, axis=0).astype(BF16)
    lse_ref[0] = jnp.concatenate(lses, axis=0)


def _dilated_branch(qkv, dilation):
    b, s, w3 = qkv.shape
    n = DIL_N
    l = s // dilation
    assert l % n == 0
    tq = min(512, l)
    n_sub = tq // n
    sub_per_tile = tq // n
    n_pairs = N_HEADS_DIL // HEADS_PER_BLOCK
    wd = N_HEADS_DIL * HEAD_DIM
    qv = qkv.reshape(b, l, dilation * w3)

    def qmap(off):
        return lambda bi, r, p, i: (bi, i, r * ROW_BLKS + off + DIL_BLK + p)

    def pmap(off):
        return lambda bi, r, p, i: (bi, jnp.maximum(i * sub_per_tile - 1, 0), r * ROW_BLKS + off + DIL_BLK + p)

    omap = lambda bi, r, p, i: (bi, i, r * (wd // LANES) + p)
    o, lse = pl.pallas_call(
        functools.partial(_dil_kernel, dilation=dilation, n_sub=n_sub),
        out_shape=(jax.ShapeDtypeStruct((b, l, dilation * wd), BF16),
                   jax.ShapeDtypeStruct((b, l, dilation * wd), F32)),
        grid=(b, dilation, n_pairs, l // tq),
        in_specs=[pl.BlockSpec((1, tq, LANES), qmap(Q_BLK0)),
                  pl.BlockSpec((1, tq, LANES), qmap(K_BLK0)),
                  pl.BlockSpec((1, tq, LANES), qmap(V_BLK0)),
                  pl.BlockSpec((1, n, LANES), pmap(K_BLK0)),
                  pl.BlockSpec((1, n, LANES), pmap(V_BLK0))],
        out_specs=(pl.BlockSpec((1, tq, LANES), omap), pl.BlockSpec((1, tq, LANES), omap)),
        scratch_shapes=[pltpu.VMEM((HEADS_PER_BLOCK, n, 2 * n), F32)],
        compiler_params=_params(4),
        name=f"dilated_attn_d{dilation}",
    )(qv, qv, qv, qv, qv)
    return o.reshape(b, s, wd), lse.reshape(b, s, wd)


def _rms(x, g):
    return x * lax.rsqrt(jnp.mean(x * x, axis=-1, keepdims=True) + NORM_EPS) * g


def _outproj_kernel(x_ref, oa_ref, ob_ref, o1_ref, o2_ref, o3_ref, l1_ref, l2_ref, l3_ref,
                    g_ref, w_ref, out_ref):
    l1, l2, l3 = l1_ref[...], l2_ref[...], l3_ref[...]
    lmax = jnp.maximum(jnp.maximum(l1, l2), l3)
    e1, e2, e3 = jnp.exp(l1 - lmax), jnp.exp(l2 - lmax), jnp.exp(l3 - lmax)
    oc = (e1 * o1_ref[...].astype(F32) + e2 * o2_ref[...].astype(F32) + e3 * o3_ref[...].astype(F32)) \
        / (e1 + e2 + e3)
    g = g_ref[...]
    wa = N_HEADS_MOBA * HEAD_DIM
    wb = wa + N_HEADS_SB * HEAD_DIM
    y = jnp.concatenate([_rms(oa_ref[...].astype(F32), g[:, :wa]),
                         _rms(ob_ref[...].astype(F32), g[:, wa:wb]),
                         _rms(oc, g[:, wb:])], axis=1).astype(BF16)
    out_ref[...] = x_ref[...] + _dot(y, w_ref[...])


def _out_proj(x2, oa, ob, ocs, lses, g, w, *, tm=512):
    t, d = x2.shape
    wa = oa.shape[1]
    wb = ob.shape[1]
    wc = ocs[0].shape[1]
    row = lambda i: (i, 0)
    const = lambda i: (0, 0)
    return pl.pallas_call(
        _outproj_kernel,
        out_shape=jax.ShapeDtypeStruct((t, d), F32),
        grid=(t // tm,),
        in_specs=[pl.BlockSpec((tm, d), row),
                  pl.BlockSpec((tm, wa), row), pl.BlockSpec((tm, wb), row),
                  pl.BlockSpec((tm, wc), row), pl.BlockSpec((tm, wc), row), pl.BlockSpec((tm, wc), row),
                  pl.BlockSpec((tm, wc), row), pl.BlockSpec((tm, wc), row), pl.BlockSpec((tm, wc), row),
                  pl.BlockSpec((1, d), const), pl.BlockSpec((d, d), const)],
        out_specs=pl.BlockSpec((tm, d), row),
        compiler_params=_params(1),
        name="out_proj",
    )(x2, oa, ob, *ocs, *lses, g.reshape(1, d), w)


_EXP_LANE0 = N_GROUPS


def _moe_kernel(x_ref, g_ref, wrh_ref, wrl_ref, br_ref, wg_ref, wu_ref, wd_ref, out_ref,
                h_ref, comb_ref, acc_ref):
    e = pl.program_id(1)
    tm = x_ref.shape[0]
    lane = lax.broadcasted_iota(jnp.int32, (tm, LANES), 1)

    @pl.when(e == 0)
    def _():
        x = x_ref[...]
        h = _rms(x, g_ref[...])
        h_ref[...] = h.astype(BF16)
        hh, hl = _split2(h)
        logits = _dot(hh, wrh_ref[...]) + _dot(hh, wrl_ref[...]) + _dot(hl, wrh_ref[...]) + br_ref[...]
        lane_f = lane.astype(F32)
        big = float(LANES)
        gl = jnp.where(lane < N_GROUPS, logits, -jnp.inf)
        gmax = jnp.max(gl, axis=1, keepdims=True)
        gidx = jnp.min(jnp.where(gl == gmax, lane_f, big), axis=1, keepdims=True)
        g_w = 1.0 / jnp.sum(jnp.exp(gl - gmax), axis=1, keepdims=True)
        lane_group = ((lane - _EXP_LANE0) // EXPERTS_PER_GROUP).astype(F32)
        in_group = (lane >= _EXP_LANE0) & (lane < _EXP_LANE0 + N_EXPERTS) & (lane_group == gidx)
        el = jnp.where(in_group, logits, -jnp.inf)
        v1 = jnp.max(el, axis=1, keepdims=True)
        i1 = jnp.min(jnp.where(el == v1, lane_f, big), axis=1, keepdims=True)
        el2 = jnp.where(lane_f == i1, -jnp.inf, el)
        v2 = jnp.max(el2, axis=1, keepdims=True)
        i2 = jnp.min(jnp.where(el2 == v2, lane_f, big), axis=1, keepdims=True)
        r = jnp.exp(v2 - v1)
        w1 = g_w / (1.0 + r)
        w2 = g_w * r / (1.0 + r)
        comb_ref[...] = jnp.where(lane_f == i1, w1, 0.0) + jnp.where(lane_f == i2, w2, 0.0)
        acc_ref[...] = jnp.zeros_like(acc_ref)

    h = h_ref[...]
    gate = _dot(h, wg_ref[0])
    up = _dot(h, wu_ref[0])
    cw = jnp.sum(jnp.where(lane == e + _EXP_LANE0, comb_ref[...], 0.0), axis=1, keepdims=True)
    act = gate / (1.0 + jnp.exp(-gate)) * up * cw
    acc_ref[...] += _dot(act.astype(BF16), wd_ref[0])

    @pl.when(e == N_EXPERTS - 1)
    def _():
        out_ref[...] = x_ref[...] + acc_ref[...]


def _moe(x2, g, wr_hi, wr_lo, br, wg, wu, wd, *, tm=1024):
    t, d = x2.shape
    f = wg.shape[2]
    row = lambda i, e: (i, 0)
    const = lambda i, e: (0, 0)
    return pl.pallas_call(
        _moe_kernel,
        out_shape=jax.ShapeDtypeStruct((t, d), F32),
        grid=(t // tm, N_EXPERTS),
        in_specs=[pl.BlockSpec((tm, d), row),
                  pl.BlockSpec((1, d), const),
                  pl.BlockSpec((d, LANES), const), pl.BlockSpec((d, LANES), const),
                  pl.BlockSpec((1, LANES), const),
                  pl.BlockSpec((1, d, f), lambda i, e: (e, 0, 0)),
                  pl.BlockSpec((1, d, f), lambda i, e: (e, 0, 0)),
                  pl.BlockSpec((1, f, d), lambda i, e: (e, 0, 0))],
        out_specs=pl.BlockSpec((tm, d), row),
        scratch_shapes=[pltpu.VMEM((tm, d), BF16), pltpu.VMEM((tm, LANES), F32), pltpu.VMEM((tm, d), F32)],
        compiler_params=_params(2),
        name="hier_moe",
    )(x2, g.reshape(1, d), wr_hi, wr_lo, br, wg, wu, wd)


def _final_norm_kernel(x_ref, g_ref, o_ref):
    o_ref[...] = _rms(x_ref[...], g_ref[...])


def _final_norm(x2, g, *, tm=1024):
    t, d = x2.shape
    return pl.pallas_call(
        _final_norm_kernel,
        out_shape=jax.ShapeDtypeStruct((t, d), F32),
        grid=(t // tm,),
        in_specs=[pl.BlockSpec((tm, d), lambda i: (i, 0)), pl.BlockSpec((1, d), lambda i: (0, 0))],
        out_specs=pl.BlockSpec((tm, d), lambda i: (i, 0)),
        compiler_params=_params(1),
        name="final_norm",
    )(x2, g.reshape(1, d))


def _router_weights(w_gr, b_gr, w_er, b_er):
    d = w_gr.shape[0]
    w = jnp.concatenate([w_gr, jnp.moveaxis(w_er, 0, 1).reshape(d, N_EXPERTS)], axis=1)
    w = jnp.pad(w, ((0, 0), (0, LANES - w.shape[1])))
    bias = jnp.pad(jnp.concatenate([b_gr, b_er.reshape(-1)]), (0, LANES - N_GROUPS - N_EXPERTS))
    hi, lo = _split2(w)
    return hi, lo, bias.reshape(1, LANES)


def _layer(x2, b, s, ln1_g, w_in, mix_g, w_out, ln2_g, w_gr, b_gr, w_er, b_er, w_gate, w_up, w_down):
    t, d = x2.shape
    qkv = _qkv_proj(x2, ln1_g, w_in.astype(BF16)).reshape(b, s, 3 * d)
    oa = _moba_attention(qkv)
    ob = _sb_attention(qkv)
    ocs, lses = [], []
    for _, dilation in DIL_CONFIGS:
        o, lse = _dilated_branch(qkv, dilation)
        ocs.append(o.reshape(t, -1))
        lses.append(lse.reshape(t, -1))
    x2 = _out_proj(x2, oa.reshape(t, -1), ob.reshape(t, -1), ocs, lses, mix_g, w_out.astype(BF16))
    wr_hi, wr_lo, br = _router_weights(w_gr, b_gr, w_er, b_er)
    f = w_gate.shape[-1]
    return _moe(x2, ln2_g, wr_hi, wr_lo, br,
                w_gate.reshape(N_EXPERTS, d, f).astype(BF16),
                w_up.reshape(N_EXPERTS, d, f).astype(BF16),
                w_down.reshape(N_EXPERTS, f, d).astype(BF16))


def kernel(x, ln1_g, w_in, mix_norm_g, w_out, ln2_g, w_group_router, b_group_router,
           w_expert_router, b_expert_router, w_gate, w_up, w_down, final_norm_g):
    b, s, d = x.shape
    x2 = x.reshape(b * s, d)
    for l in range(ln1_g.shape[0]):
        x2 = _layer(x2, b, s, ln1_g[l], w_in[l], mix_norm_g[l], w_out[l], ln2_g[l],
                    w_group_router[l], b_group_router[l], w_expert_router[l], b_expert_router[l],
                    w_gate[l], w_up[l], w_down[l])
    return _final_norm(x2, final_norm_g).reshape(b, s, d)
```

```python
import functools

import jax
import jax.numpy as jnp
from jax import lax
from jax.experimental import pallas as pl
from jax.experimental.pallas import tpu as pltpu

F32 = jnp.float32
BF16 = jnp.bfloat16

D_MODEL = 1024
HEAD_DIM = 64
N_HEADS = 16
LANES = 128
HEADS_PER_BLOCK = LANES // HEAD_DIM
N_HEADS_MOBA = 4
N_HEADS_SB = 4
N_HEADS_DIL = 8
MOBA_BLOCK = 256
MOBA_TOPK = 3
DIL_CONFIGS = ((128, 1), (512, 4), (2048, 16))
DIL_N = 128
N_GROUPS = 4
EXPERTS_PER_GROUP = 4
N_EXPERTS = N_GROUPS * EXPERTS_PER_GROUP
D_EXPERT = 256
NORM_EPS = 1e-6
NEG_INF = -1e30
SCALE = HEAD_DIM ** -0.5

Q_BLK0 = 0
K_BLK0 = D_MODEL // LANES
V_BLK0 = 2 * D_MODEL // LANES
ROW_BLKS = 3 * D_MODEL // LANES
SB_BLK = N_HEADS_MOBA // HEADS_PER_BLOCK
DIL_BLK = (N_HEADS_MOBA + N_HEADS_SB) // HEADS_PER_BLOCK

VMEM_LIMIT = 56 * 1024 * 1024


def _params(n_axes, vmem=VMEM_LIMIT):
    return pltpu.CompilerParams(dimension_semantics=("arbitrary",) * n_axes,
                                vmem_limit_bytes=vmem)


def _dot_nt(a, b):
    return lax.dot_general(a, b, (((1,), (1,)), ((), ())), preferred_element_type=F32)


def _dot(a, b):
    return jnp.dot(a, b, preferred_element_type=F32)


def _split3(x):
    hi = x.astype(BF16)
    r1 = x - hi.astype(F32)
    mid = r1.astype(BF16)
    lo = (r1 - mid.astype(F32)).astype(BF16)
    return hi, mid, lo


def _split2(x):
    hi = x.astype(BF16)
    lo = (x - hi.astype(F32)).astype(BF16)
    return hi, lo


def _qkv_kernel(x_ref, g_ref, w_ref, o_ref, h_ref):
    @pl.when(pl.program_id(1) == 0)
    def _():
        x = x_ref[...]
        ms = jnp.mean(x * x, axis=-1, keepdims=True)
        h_ref[...] = (x * lax.rsqrt(ms + NORM_EPS) * g_ref[...]).astype(BF16)

    o_ref[...] = _dot(h_ref[...], w_ref[...]).astype(BF16)


def _qkv_proj(x2, g, w, *, tm=512, tn=1024):
    t, d = x2.shape
    n = w.shape[1]
    return pl.pallas_call(
        _qkv_kernel,
        out_shape=jax.ShapeDtypeStruct((t, n), BF16),
        grid=(t // tm, n // tn),
        in_specs=[pl.BlockSpec((tm, d), lambda i, j: (i, 0)),
                  pl.BlockSpec((1, d), lambda i, j: (0, 0)),
                  pl.BlockSpec((d, tn), lambda i, j: (0, j))],
        out_specs=pl.BlockSpec((tm, tn), lambda i, j: (i, j)),
        scratch_shapes=[pltpu.VMEM((tm, d), BF16)],
        compiler_params=_params(2),
        name="qkv_proj",
    )(x2, g.reshape(1, d), w)


_MB_SEL0 = 0
_MB_POS0 = 32
_MB_KILL = 40
_MB_BLK0 = 64


_ROW_CHUNK = 32


def _moba_kernel(q_ref, k_ref, v_ref, o_ref, kmean_ref, kx_ref, qaug_ref, m_ref, alpha_ref, acc_ref,
                 s_ref, p_ref, *, n_blk):
    pair = pl.program_id(1)
    qi = pl.program_id(2)
    blk = MOBA_BLOCK
    lane = lax.broadcasted_iota(jnp.int32, (blk, LANES), 1)
    row = lax.broadcasted_iota(jnp.int32, (blk, LANES), 0)
    lane_f = lane.astype(F32)
    slopes = [jnp.where(pair == 0, 2.0 ** (-2 * (hh + 1)), 2.0 ** (-2 * (hh + 3))).astype(F32)
              for hh in range(HEADS_PER_BLOCK)]

    @pl.when(qi == 0)
    def _():
        kmean_ref[...] = jnp.zeros_like(kmean_ref)

        def body(n, c):
            kb = k_ref[0, pl.ds(pl.multiple_of(n * blk, blk), blk), :].astype(F32)
            kmean_ref[pl.ds(n, 1), :] = jnp.sum(kb, axis=0, keepdims=True) * (1.0 / blk)
            return c

        lax.fori_loop(0, n_blk, body, 0)
        kx = jnp.zeros((blk, LANES), F32)
        for hh in range(HEADS_PER_BLOCK):
            kx = jnp.where(lane == _MB_POS0 + 2 * hh, slopes[hh] * ((row // LANES) * LANES).astype(F32), kx)
            kx = jnp.where(lane == _MB_POS0 + 2 * hh + 1, slopes[hh] * (row % LANES).astype(F32), kx)
        kx_ref[...] = kx.astype(BF16)

    q2 = q_ref[0]
    km_parts = _split3(kmean_ref[...])
    for hh in range(HEADS_PER_BLOCK):
        in_head = (lane >= hh * HEAD_DIM) & (lane < (hh + 1) * HEAD_DIM)
        qm = jnp.where(in_head, q2, jnp.zeros_like(q2))
        gate = _dot_nt(qm, km_parts[0]) + _dot_nt(qm, km_parts[1]) + _dot_nt(qm, km_parts[2])
        gate = jnp.where(lane < qi, gate, NEG_INF)
        gate = jnp.where(lane < n_blk, gate, -jnp.inf)
        sel = jnp.zeros((blk, LANES), jnp.bool_)
        for _ in range(MOBA_TOPK):
            gmax = jnp.max(gate, axis=1, keepdims=True)
            first = jnp.min(jnp.where(gate == gmax, lane_f, float(LANES)), axis=1, keepdims=True)
            pick = lane_f == first
            sel = sel | pick
            gate = jnp.where(pick, -jnp.inf, gate)
        sel = sel & (lane < qi)
        extra = jnp.where(sel, 0.0, NEG_INF)
        extra = jnp.where(lane >= n_blk, 0.0, extra)
        extra = jnp.where((lane == _MB_POS0 + 2 * hh) | (lane == _MB_POS0 + 2 * hh + 1) | (lane == _MB_KILL),
                          1.0, extra)
        blk_lane = lane - (_MB_BLK0 + 32 * hh)
        extra = jnp.where((blk_lane >= 0) & (blk_lane < 32),
                          slopes[hh] * (blk_lane * blk).astype(F32), extra)
        qaug_ref[hh] = jnp.concatenate([qm * jnp.asarray(SCALE, BF16), extra.astype(BF16)], axis=1)
        m_ref[hh] = jnp.full((blk, LANES), -jnp.inf, F32)
        acc_ref[hh] = jnp.zeros((blk, LANES), F32)

    lane1 = lax.broadcasted_iota(jnp.int32, (1, LANES), 1)
    rc = _ROW_CHUNK

    def score_stage(j, buf, *, is_own=False, valid=True):
        kj = k_ref[0, pl.ds(pl.multiple_of(j * blk, blk), blk), :]
        ind = (lane1 % 32 == j) & (lane1 >= _MB_BLK0) if is_own else \
              (lane1 % 32 == j) & ((lane1 < 32) | (lane1 >= _MB_BLK0))
        kill = jnp.where(lane1 == _MB_KILL, jnp.where(valid, 0.0, NEG_INF), 0.0)
        side = jnp.broadcast_to(jnp.where(ind, 1.0, kill), (blk, LANES)).astype(BF16)
        kx = jnp.where(ind | (lane1 == _MB_KILL), side, kx_ref[...])
        k_aug = jnp.concatenate([kj, kx], axis=1)
        for hh in range(HEADS_PER_BLOCK):
            s_ref[buf, hh] = _dot_nt(qaug_ref[hh], k_aug)

    def softmax_stage(buf, *, is_own=False):
        for hh in range(HEADS_PER_BLOCK):
            for c in range(blk // rc):
                rows = slice(c * rc, (c + 1) * rc)
                s = s_ref[buf, hh, rows, :]
                if is_own:
                    col_c = lax.broadcasted_iota(jnp.int32, (rc, blk), 1)
                    row_c = lax.broadcasted_iota(jnp.int32, (rc, blk), 0) + c * rc
                    s = jnp.where(col_c <= row_c, s, NEG_INF)
                m_old = m_ref[hh, rows, :]
                m_new = jnp.maximum(m_old, jnp.max(s, axis=1, keepdims=True))
                alpha_ref[buf, hh, rows, :] = jnp.exp(m_old - m_new)
                m_ref[hh, rows, :] = m_new
                p = jnp.exp(s - jnp.concatenate([m_new, m_new], axis=1))
                p_ref[buf, hh, rows, :] = p.astype(BF16)

    def value_stage(j, buf):
        vj = v_ref[0, pl.ds(pl.multiple_of(j * blk, blk), blk), :]
        for hh in range(HEADS_PER_BLOCK):
            in_head1 = (lane1 >= hh * HEAD_DIM) & (lane1 < (hh + 1) * HEAD_DIM)
            v_aug = jnp.where(in_head1, vj, jnp.ones_like(vj))
            acc_ref[hh] = alpha_ref[buf, hh] * acc_ref[hh] + _dot(p_ref[buf, hh], v_aug)

    score_stage(qi, 0, is_own=True)
    softmax_stage(0, is_own=True)
    value_stage(qi, 0)

    n_past = qi

    @pl.when(n_past > 0)
    def _():
        s_ref[1] = jnp.full(s_ref.shape[1:], NEG_INF, F32)
        p_ref[...] = jnp.zeros_like(p_ref)
        alpha_ref[...] = jnp.ones_like(alpha_ref)
        last = n_past - 1

        def pair(tt, c):
            t = 2 * tt
            score_stage(jnp.minimum(t, last), 0, valid=t < n_past)
            softmax_stage(1)
            value_stage(jnp.clip(t - 2, 0, last), 0)
            score_stage(jnp.minimum(t + 1, last), 1, valid=t + 1 < n_past)
            softmax_stage(0)
            value_stage(jnp.clip(t - 1, 0, last), 1)
            return c

        lax.fori_loop(0, (n_past + 3) // 2, pair, 0)

    acc0 = acc_ref[0]
    acc1 = acc_ref[1]
    o0 = acc0 / pltpu.roll(acc0, HEAD_DIM, axis=1)
    o1 = acc1 / pltpu.roll(acc1, HEAD_DIM, axis=1)
    o_ref[0] = jnp.where(lane < HEAD_DIM, o0, o1).astype(BF16)


def _moba_attention(qkv):
    b, s, _ = qkv.shape
    blk = MOBA_BLOCK
    n_blk = s // blk
    assert s % blk == 0 and MOBA_TOPK <= n_blk - 1 and n_blk <= 32
    n_pairs = N_HEADS_MOBA // HEADS_PER_BLOCK
    return pl.pallas_call(
        functools.partial(_moba_kernel, n_blk=n_blk),
        out_shape=jax.ShapeDtypeStruct((b, s, N_HEADS_MOBA * HEAD_DIM), BF16),
        grid=(b, n_pairs, n_blk),
        in_specs=[pl.BlockSpec((1, blk, LANES), lambda bi, p, i: (bi, i, Q_BLK0 + p)),
                  pl.BlockSpec((1, s, LANES), lambda bi, p, i: (bi, 0, K_BLK0 + p)),
                  pl.BlockSpec((1, s, LANES), lambda bi, p, i: (bi, 0, V_BLK0 + p))],
        out_specs=pl.BlockSpec((1, blk, LANES), lambda bi, p, i: (bi, i, p)),
        scratch_shapes=[pltpu.VMEM((LANES, LANES), F32),
                        pltpu.VMEM((blk, LANES), BF16),
                        pltpu.VMEM((HEADS_PER_BLOCK, blk, 2 * LANES), BF16),
                        pltpu.VMEM((HEADS_PER_BLOCK, blk, LANES), F32),
                        pltpu.VMEM((2, HEADS_PER_BLOCK, blk, LANES), F32),
                        pltpu.VMEM((HEADS_PER_BLOCK, blk, LANES), F32),
                        pltpu.VMEM((2, HEADS_PER_BLOCK, blk, blk), F32),
                        pltpu.VMEM((2, HEADS_PER_BLOCK, blk, blk), BF16)],
        compiler_params=_params(3),
        name="moba_attn",
    )(qkv, qkv, qkv)


SB_TILE = 256


def _softplus(z):
    return jnp.maximum(z, 0.0) + jnp.log(1.0 + jnp.exp(-jnp.abs(z)))


_SB_SLOTS = 3


def _sb_kernel(q_ref, k_ref, v_ref, u_ref, o_ref, qaug_ref, carry_ref, acc_ref,
               z_ref, sphl_ref, c_ref, rs_ref, a_ref):
    qi = pl.program_id(2)
    t = SB_TILE
    rc = _ROW_CHUNK
    lane = lax.broadcasted_iota(jnp.int32, (t, LANES), 1)
    lane1 = lax.broadcasted_iota(jnp.int32, (1, LANES), 1)
    q2 = q_ref[0]
    for hh in range(HEADS_PER_BLOCK):
        in_head = (lane >= hh * HEAD_DIM) & (lane < (hh + 1) * HEAD_DIM)
        qm = jnp.where(in_head, q2, jnp.zeros_like(q2)) * jnp.asarray(SCALE, BF16)
        qaug_ref[hh] = jnp.concatenate([qm, jnp.where(lane == 0, 1.0, 0.0).astype(BF16)], axis=1)

    def score_stage(j, slot, *, valid=True):
        kj = k_ref[0, pl.ds(pl.multiple_of(j * t, t), t), :]
        kill = jnp.where(lane1 == 0, jnp.where(valid, 0.0, NEG_INF), 0.0)
        k_aug = jnp.concatenate([kj, jnp.broadcast_to(kill, (t, LANES)).astype(BF16)], axis=1)
        for hh in range(HEADS_PER_BLOCK):
            z_ref[slot, hh] = _dot_nt(qaug_ref[hh], k_aug)

    def _past_mask(c):
        col_c = lax.broadcasted_iota(jnp.int32, (rc, t), 1)
        row_c = lax.broadcasted_iota(jnp.int32, (rc, t), 0) + c * rc
        return col_c < row_c

    def softplus_stage(slot, *, diagonal=False):
        for hh in range(HEADS_PER_BLOCK):
            for c in range(t // rc):
                rows = slice(c * rc, (c + 1) * rc)
                sp = _softplus(z_ref[slot, hh, rows, :])
                if diagonal:
                    sp = jnp.where(_past_mask(c), sp, 0.0)
                hi, lo = _split2(sp)
                sphl_ref[slot, pl.ds((2 * hh) * t + c * rc, rc), :] = hi
                sphl_ref[slot, pl.ds((2 * hh + 1) * t + c * rc, rc), :] = lo
                rs_ref[slot, hh, rows, :] = jnp.broadcast_to(jnp.sum(sp, axis=1, keepdims=True), (rc, LANES))

    def suffix_stage(slot):
        c_ref[slot] = _dot(sphl_ref[slot], u_ref[...])

    def weight_stage(slot, *, diagonal=False):
        for hh in range(HEADS_PER_BLOCK):
            for c in range(t // rc):
                rows = slice(c * rc, (c + 1) * rc)
                z = z_ref[slot, hh, rows, :]
                cc = c_ref[slot, pl.ds((2 * hh) * t + c * rc, rc), :] + \
                    c_ref[slot, pl.ds((2 * hh + 1) * t + c * rc, rc), :]
                if diagonal:
                    a = jnp.where(_past_mask(c), jnp.exp(z - cc), 0.0)
                    carry_ref[hh, rows, :] = rs_ref[slot, hh, rows, :]
                else:
                    carry = carry_ref[hh, rows, :]
                    a = jnp.exp(z - (cc + jnp.concatenate([carry, carry], axis=1)))
                    carry_ref[hh, rows, :] = carry + rs_ref[slot, hh, rows, :]
                a_ref[slot, hh, rows, :] = a.astype(BF16)

    def value_stage(j, slot, *, first=False):
        vj = v_ref[0, pl.ds(pl.multiple_of(j * t, t), t), :]
        for hh in range(HEADS_PER_BLOCK):
            av = _dot(a_ref[slot, hh], vj)
            acc_ref[hh] = av if first else acc_ref[hh] + av

    score_stage(qi, 0)
    softplus_stage(0, diagonal=True)
    suffix_stage(0)
    weight_stage(0, diagonal=True)
    value_stage(qi, 0, first=True)

    n_past = qi

    @pl.when(n_past > 0)
    def _():
        z_ref[1] = jnp.full(z_ref.shape[1:], NEG_INF, F32)
        z_ref[2] = jnp.full(z_ref.shape[1:], NEG_INF, F32)
        c_ref[1] = jnp.zeros(c_ref.shape[1:], F32)
        rs_ref[1] = jnp.zeros(rs_ref.shape[1:], F32)
        last = n_past - 1

        def key_tile(i):
            return jnp.clip(last - i, 0, last)

        def trip(tt, c):
            for k in range(_SB_SLOTS):
                i = _SB_SLOTS * tt + k
                score_stage(key_tile(i), k, valid=i < n_past)
                softplus_stage((k - 1) % _SB_SLOTS)
                suffix_stage((k - 1) % _SB_SLOTS)
                weight_stage((k - 2) % _SB_SLOTS)
                value_stage(key_tile(i - 2), (k - 2) % _SB_SLOTS)
            return c

        lax.fori_loop(0, (n_past + 2 + _SB_SLOTS - 1) // _SB_SLOTS, trip, 0)

    o_ref[0] = jnp.where(lane < HEAD_DIM, acc_ref[0], acc_ref[1]).astype(BF16)


def _sb_attention(qkv):
    b, s, _ = qkv.shape
    t = SB_TILE
    assert s % t == 0
    n_pairs = N_HEADS_SB // HEADS_PER_BLOCK
    u = (lax.broadcasted_iota(jnp.int32, (t, t), 0) >= lax.broadcasted_iota(jnp.int32, (t, t), 1)).astype(BF16)
    return pl.pallas_call(
        _sb_kernel,
        out_shape=jax.ShapeDtypeStruct((b, s, N_HEADS_SB * HEAD_DIM), BF16),
        grid=(b, n_pairs, s // t),
        in_specs=[pl.BlockSpec((1, t, LANES), lambda bi, p, i: (bi, i, Q_BLK0 + SB_BLK + p)),
                  pl.BlockSpec((1, s, LANES), lambda bi, p, i: (bi, 0, K_BLK0 + SB_BLK + p)),
                  pl.BlockSpec((1, s, LANES), lambda bi, p, i: (bi, 0, V_BLK0 + SB_BLK + p)),
                  pl.BlockSpec((t, t), lambda bi, p, i: (0, 0))],
        out_specs=pl.BlockSpec((1, t, LANES), lambda bi, p, i: (bi, i, p)),
        scratch_shapes=[pltpu.VMEM((HEADS_PER_BLOCK, t, 2 * LANES), BF16),
                        pltpu.VMEM((HEADS_PER_BLOCK, t, LANES), F32),
                        pltpu.VMEM((HEADS_PER_BLOCK, t, LANES), F32),
                        pltpu.VMEM((_SB_SLOTS, HEADS_PER_BLOCK, t, t), F32),
                        pltpu.VMEM((_SB_SLOTS, 2 * HEADS_PER_BLOCK * t, t), BF16),
                        pltpu.VMEM((_SB_SLOTS, 2 * HEADS_PER_BLOCK * t, t), F32),
                        pltpu.VMEM((_SB_SLOTS, HEADS_PER_BLOCK, t, LANES), F32),
                        pltpu.VMEM((_SB_SLOTS, HEADS_PER_BLOCK, t, t), BF16)],
        compiler_params=_params(3),
        name="sb_attn",
    )(qkv, qkv, qkv, u)


def _dil_kernel(q_ref, k_ref, v_ref, kp_ref, vp_ref, o_ref, lse_ref, bias_ref, *, dilation, n_sub):
    pair = pl.program_id(2)
    i = pl.program_id(3)
    n = DIL_N
    lane = lax.broadcasted_iota(jnp.int32, (n, LANES), 1)

    @pl.when(i == 0)
    def _():
        col = lax.broadcasted_iota(jnp.int32, (n, 2 * n), 1)
        row = lax.broadcasted_iota(jnp.int32, (n, 2 * n), 0)
        delta = row + n - col
        valid = (delta >= 0) & (delta <= n)
        for hh in range(HEADS_PER_BLOCK):
            slope = jnp.asarray(2.0 ** -(hh + 1), F32)
            for p in range(1, N_HEADS_DIL // HEADS_PER_BLOCK):
                slope = jnp.where(pair == p, 2.0 ** -(2 * p + hh + 1), slope)
            bias_ref[hh] = jnp.where(valid, -slope * (delta * dilation).astype(F32), NEG_INF)

    q2 = q_ref[0]
    outs = []
    lses = []
    for sb in range(n_sub):
        qs = q2[sb * n:(sb + 1) * n]
        if sb == 0:
            kk = jnp.concatenate([kp_ref[0], k_ref[0, 0:n]], axis=0)
            vv = jnp.concatenate([vp_ref[0], v_ref[0, 0:n]], axis=0)
        else:
            kk = k_ref[0, (sb - 1) * n:(sb + 1) * n]
            vv = v_ref[0, (sb - 1) * n:(sb + 1) * n]
        o_h = []
        lse_h = []
        for hh in range(HEADS_PER_BLOCK):
            in_head = (lane >= hh * HEAD_DIM) & (lane < (hh + 1) * HEAD_DIM)
            qm = jnp.where(in_head, qs, jnp.zeros_like(qs)) * jnp.asarray(SCALE, BF16)
            s = _dot_nt(qm, kk) + bias_ref[hh]
            if sb == 0:
                colh = lax.broadcasted_iota(jnp.int32, (n, 2 * n), 1)
                s = s + jnp.where(colh < n, jnp.where(i == 0, NEG_INF, 0.0), 0.0)
            m = jnp.max(s, axis=1, keepdims=True)
            p = jnp.exp(s - m)
            den = jnp.sum(p, axis=1, keepdims=True)
            o_h.append(_dot(p.astype(BF16), vv) / den)
            lse_h.append(jnp.broadcast_to(m + jnp.log(den), (n, LANES)))
        outs.append(jnp.where(lane < HEAD_DIM, o_h[0], o_h[1]))
        lses.append(jnp.where(lane < HEAD_DIM, lse_h[0], lse_h[1]))
    o_ref[0] = jnp.concatenate(outs, axis=0).astype(BF16)
    lse_ref[0] = jnp.concatenate(lses, axis=0)


def _dilated_branch(qkv, dilation):
    b, s, w3 = qkv.shape
    n = DIL_N
    l = s // dilation
    assert l % n == 0
    tq = min(512, l)
    n_sub = tq // n
    sub_per_tile = tq // n
    n_pairs = N_HEADS_DIL // HEADS_PER_BLOCK
    wd = N_HEADS_DIL * HEAD_DIM
    qv = qkv.reshape(b, l, dilation * w3)

    def qmap(off):
        return lambda bi, r, p, i: (bi, i, r * ROW_BLKS + off + DIL_BLK + p)

    def pmap(off):
        return lambda bi, r, p, i: (bi, jnp.maximum(i * sub_per_tile - 1, 0), r * ROW_BLKS + off + DIL_BLK + p)

    omap = lambda bi, r, p, i: (bi, i, r * (wd // LANES) + p)
    o, lse = pl.pallas_call(
        functools.partial(_dil_kernel, dilation=dilation, n_sub=n_sub),
        out_shape=(jax.ShapeDtypeStruct((b, l, dilation * wd), BF16),
                   jax.ShapeDtypeStruct((b, l, dilation * wd), F32)),
        grid=(b, dilation, n_pairs, l // tq),
        in_specs=[pl.BlockSpec((1, tq, LANES), qmap(Q_BLK0)),
                  pl.BlockSpec((1, tq, LANES), qmap(K_BLK0)),
                  pl.BlockSpec((1, tq, LANES), qmap(V_BLK0)),
                  pl.BlockSpec((1, n, LANES), pmap(K_BLK0)),
                  pl.BlockSpec((1, n, LANES), pmap(V_BLK0))],
        out_specs=(pl.BlockSpec((1, tq, LANES), omap), pl.BlockSpec((1, tq, LANES), omap)),
        scratch_shapes=[pltpu.VMEM((HEADS_PER_BLOCK, n, 2 * n), F32)],
        compiler_params=_params(4),
        name=f"dilated_attn_d{dilation}",
    )(qv, qv, qv, qv, qv)
    return o.reshape(b, s, wd), lse.reshape(b, s, wd)


def _rms(x, g):
    return x * lax.rsqrt(jnp.mean(x * x, axis=-1, keepdims=True) + NORM_EPS) * g


def _outproj_kernel(x_ref, oa_ref, ob_ref, o1_ref, o2_ref, o3_ref, l1_ref, l2_ref, l3_ref,
                    g_ref, w_ref, out_ref):
    l1, l2, l3 = l1_ref[...], l2_ref[...], l3_ref[...]
    lmax = jnp.maximum(jnp.maximum(l1, l2), l3)
    e1, e2, e3 = jnp.exp(l1 - lmax), jnp.exp(l2 - lmax), jnp.exp(l3 - lmax)
    oc = (e1 * o1_ref[...].astype(F32) + e2 * o2_ref[...].astype(F32) + e3 * o3_ref[...].astype(F32)) \
        / (e1 + e2 + e3)
    g = g_ref[...]
    wa = N_HEADS_MOBA * HEAD_DIM
    wb = wa + N_HEADS_SB * HEAD_DIM
    y = jnp.concatenate([_rms(oa_ref[...].astype(F32), g[:, :wa]),
                         _rms(ob_ref[...].astype(F32), g[:, wa:wb]),
                         _rms(oc, g[:, wb:])], axis=1).astype(BF16)
    out_ref[...] = x_ref[...] + _dot(y, w_ref[...])


def _out_proj(x2, oa, ob, ocs, lses, g, w, *, tm=512):
    t, d = x2.shape
    wa = oa.shape[1]
    wb = ob.shape[1]
    wc = ocs[0].shape[1]
    row = lambda i: (i, 0)
    const = lambda i: (0, 0)
    return pl.pallas_call(
        _outproj_kernel,
        out_shape=jax.ShapeDtypeStruct((t, d), F32),
        grid=(t // tm,),
        in_specs=[pl.BlockSpec((tm, d), row),
                  pl.BlockSpec((tm, wa), row), pl.BlockSpec((tm, wb), row),
                  pl.BlockSpec((tm, wc), row), pl.BlockSpec((tm, wc), row), pl.BlockSpec((tm, wc), row),
                  pl.BlockSpec((tm, wc), row), pl.BlockSpec((tm, wc), row), pl.BlockSpec((tm, wc), row),
                  pl.BlockSpec((1, d), const), pl.BlockSpec((d, d), const)],
        out_specs=pl.BlockSpec((tm, d), row),
        compiler_params=_params(1),
        name="out_proj",
    )(x2, oa, ob, *ocs, *lses, g.reshape(1, d), w)


_EXP_LANE0 = N_GROUPS


def _moe_kernel(x_ref, g_ref, wrh_ref, wrl_ref, br_ref, wg_ref, wu_ref, wd_ref, out_ref,
                h_ref, comb_ref, acc_ref):
    e = pl.program_id(1)
    tm = x_ref.shape[0]
    lane = lax.broadcasted_iota(jnp.int32, (tm, LANES), 1)

    @pl.when(e == 0)
    def _():
        x = x_ref[...]
        h = _rms(x, g_ref[...])
        h_ref[...] = h.astype(BF16)
        hh, hl = _split2(h)
        logits = _dot(hh, wrh_ref[...]) + _dot(hh, wrl_ref[...]) + _dot(hl, wrh_ref[...]) + br_ref[...]
        lane_f = lane.astype(F32)
        big = float(LANES)
        gl = jnp.where(lane < N_GROUPS, logits, -jnp.inf)
        gmax = jnp.max(gl, axis=1, keepdims=True)
        gidx = jnp.min(jnp.where(gl == gmax, lane_f, big), axis=1, keepdims=True)
        g_w = 1.0 / jnp.sum(jnp.exp(gl - gmax), axis=1, keepdims=True)
        lane_group = ((lane - _EXP_LANE0) // EXPERTS_PER_GROUP).astype(F32)
        in_group = (lane >= _EXP_LANE0) & (lane < _EXP_LANE0 + N_EXPERTS) & (lane_group == gidx)
        el = jnp.where(in_group, logits, -jnp.inf)
        v1 = jnp.max(el, axis=1, keepdims=True)
        i1 = jnp.min(jnp.where(el == v1, lane_f, big), axis=1, keepdims=True)
        el2 = jnp.where(lane_f == i1, -jnp.inf, el)
        v2 = jnp.max(el2, axis=1, keepdims=True)
        i2 = jnp.min(jnp.where(el2 == v2, lane_f, big), axis=1, keepdims=True)
        r = jnp.exp(v2 - v1)
        w1 = g_w / (1.0 + r)
        w2 = g_w * r / (1.0 + r)
        comb_ref[...] = jnp.where(lane_f == i1, w1, 0.0) + jnp.where(lane_f == i2, w2, 0.0)
        acc_ref[...] = jnp.zeros_like(acc_ref)

    h = h_ref[...]
    gate = _dot(h, wg_ref[0])
    up = _dot(h, wu_ref[0])
    cw = jnp.sum(jnp.where(lane == e + _EXP_LANE0, comb_ref[...], 0.0), axis=1, keepdims=True)
    act = gate / (1.0 + jnp.exp(-gate)) * up * cw
    acc_ref[...] += _dot(act.astype(BF16), wd_ref[0])

    @pl.when(e == N_EXPERTS - 1)
    def _():
        out_ref[...] = x_ref[...] + acc_ref[...]


def _moe(x2, g, wr_hi, wr_lo, br, wg, wu, wd, *, tm=1024):
    t, d = x2.shape
    f = wg.shape[2]
    row = lambda i, e: (i, 0)
    const = lambda i, e: (0, 0)
    return pl.pallas_call(
        _moe_kernel,
        out_shape=jax.ShapeDtypeStruct((t, d), F32),
        grid=(t // tm, N_EXPERTS),
        in_specs=[pl.BlockSpec((tm, d), row),
                  pl.BlockSpec((1, d), const),
                  pl.BlockSpec((d, LANES), const), pl.BlockSpec((d, LANES), const),
                  pl.BlockSpec((1, LANES), const),
                  pl.BlockSpec((1, d, f), lambda i, e: (e, 0, 0)),
                  pl.BlockSpec((1, d, f), lambda i, e: (e, 0, 0)),
                  pl.BlockSpec((1, f, d), lambda i, e: (e, 0, 0))],
        out_specs=pl.BlockSpec((tm, d), row),
        scratch_shapes=[pltpu.VMEM((tm, d), BF16), pltpu.VMEM((tm, LANES), F32), pltpu.VMEM((tm, d), F32)],
        compiler_params=_params(2),
        name="hier_moe",
    )(x2, g.reshape(1, d), wr_hi, wr_lo, br, wg, wu, wd)


def _final_norm_kernel(x_ref, g_ref, o_ref):
    o_ref[...] = _rms(x_ref[...], g_ref[...])


def _final_norm(x2, g, *, tm=1024):
    t, d = x2.shape
    return pl.pallas_call(
        _final_norm_kernel,
        out_shape=jax.ShapeDtypeStruct((t, d), F32),
        grid=(t // tm,),
        in_specs=[pl.BlockSpec((tm, d), lambda i: (i, 0)), pl.BlockSpec((1, d), lambda i: (0, 0))],
        out_specs=pl.BlockSpec((tm, d), lambda i: (i, 0)),
        compiler_params=_params(1),
        name="final_norm",
    )(x2, g.reshape(1, d))


def _router_weights(w_gr, b_gr, w_er, b_er):
    d = w_gr.shape[0]
    w = jnp.concatenate([w_gr, jnp.moveaxis(w_er, 0, 1).reshape(d, N_EXPERTS)], axis=1)
    w = jnp.pad(w, ((0, 0), (0, LANES - w.shape[1])))
    bias = jnp.pad(jnp.concatenate([b_gr, b_er.reshape(-1)]), (0, LANES - N_GROUPS - N_EXPERTS))
    hi, lo = _split2(w)
    return hi, lo, bias.reshape(1, LANES)


def _layer(x2, b, s, ln1_g, w_in, mix_g, w_out, ln2_g, w_gr, b_gr, w_er, b_er, w_gate, w_up, w_down):
    t, d = x2.shape
    qkv = _qkv_proj(x2, ln1_g, w_in.astype(BF16)).reshape(b, s, 3 * d)
    oa = _moba_attention(qkv)
    ob = _sb_attention(qkv)
    ocs, lses = [], []
    for _, dilation in DIL_CONFIGS:
        o, lse = _dilated_branch(qkv, dilation)
        ocs.append(o.reshape(t, -1))
        lses.append(lse.reshape(t, -1))
    x2 = _out_proj(x2, oa.reshape(t, -1), ob.reshape(t, -1), ocs, lses, mix_g, w_out.astype(BF16))
    wr_hi, wr_lo, br = _router_weights(w_gr, b_gr, w_er, b_er)
    f = w_gate.shape[-1]
    return _moe(x2, ln2_g, wr_hi, wr_lo, br,
                w_gate.reshape(N_EXPERTS, d, f).astype(BF16),
                w_up.reshape(N_EXPERTS, d, f).astype(BF16),
                w_down.reshape(N_EXPERTS, f, d).astype(BF16))


def kernel(x, ln1_g, w_in, mix_norm_g, w_out, ln2_g, w_group_router, b_group_router,
           w_expert_router, b_expert_router, w_gate, w_up, w_down, final_norm_g):
    b, s, d = x.shape
    x2 = x.reshape(b * s, d)
    for l in range(ln1_g.shape[0]):
        x2 = _layer(x2, b, s, ln1_g[l], w_in[l], mix_norm_g[l], w_out[l], ln2_g[l],
                    w_group_router[l], b_group_router[l], w_expert_router[l], b_expert_router[l],
                    w_gate[l], w_up[l], w_down[l])
    return _final_norm(x2, final_norm_g).reshape(b, s, d)
```

```python
import functools

import jax
import jax.numpy as jnp
from jax import lax
from jax.experimental import pallas as pl
from jax.experimental.pallas import tpu as pltpu

F32 = jnp.float32
BF16 = jnp.bfloat16

D_MODEL = 1024
HEAD_DIM = 64
N_HEADS = 16
LANES = 128
HEADS_PER_BLOCK = LANES // HEAD_DIM
N_HEADS_MOBA = 4
N_HEADS_SB = 4
N_HEADS_DIL = 8
MOBA_BLOCK = 256
MOBA_TOPK = 3
DIL_CONFIGS = ((128, 1), (512, 4), (2048, 16))
DIL_N = 128
N_GROUPS = 4
EXPERTS_PER_GROUP = 4
N_EXPERTS = N_GROUPS * EXPERTS_PER_GROUP
D_EXPERT = 256
NORM_EPS = 1e-6
NEG_INF = -1e30
SCALE = HEAD_DIM ** -0.5

Q_BLK0 = 0
K_BLK0 = D_MODEL // LANES
V_BLK0 = 2 * D_MODEL // LANES
ROW_BLKS = 3 * D_MODEL // LANES
SB_BLK = N_HEADS_MOBA // HEADS_PER_BLOCK
DIL_BLK = (N_HEADS_MOBA + N_HEADS_SB) // HEADS_PER_BLOCK

VMEM_LIMIT = 56 * 1024 * 1024


def _params(n_axes, vmem=VMEM_LIMIT):
    return pltpu.CompilerParams(dimension_semantics=("arbitrary",) * n_axes,
                                vmem_limit_bytes=vmem)


def _dot_nt(a, b):
    return lax.dot_general(a, b, (((1,), (1,)), ((), ())), preferred_element_type=F32)


def _dot(a, b):
    return jnp.dot(a, b, preferred_element_type=F32)


def _split3(x):
    hi = x.astype(BF16)
    r1 = x - hi.astype(F32)
    mid = r1.astype(BF16)
    lo = (r1 - mid.astype(F32)).astype(BF16)
    return hi, mid, lo


def _split2(x):
    hi = x.astype(BF16)
    lo = (x - hi.astype(F32)).astype(BF16)
    return hi, lo


def _qkv_kernel(x_ref, g_ref, w_ref, o_ref, h_ref):
    @pl.when(pl.program_id(1) == 0)
    def _():
        x = x_ref[...]
        ms = jnp.mean(x * x, axis=-1, keepdims=True)
        h_ref[...] = (x * lax.rsqrt(ms + NORM_EPS) * g_ref[...]).astype(BF16)

    o_ref[...] = _dot(h_ref[...], w_ref[...]).astype(BF16)


def _qkv_proj(x2, g, w, *, tm=512, tn=1024):
    t, d = x2.shape
    n = w.shape[1]
    return pl.pallas_call(
        _qkv_kernel,
        out_shape=jax.ShapeDtypeStruct((t, n), BF16),
        grid=(t // tm, n // tn),
        in_specs=[pl.BlockSpec((tm, d), lambda i, j: (i, 0)),
                  pl.BlockSpec((1, d), lambda i, j: (0, 0)),
                  pl.BlockSpec((d, tn), lambda i, j: (0, j))],
        out_specs=pl.BlockSpec((tm, tn), lambda i, j: (i, j)),
        scratch_shapes=[pltpu.VMEM((tm, d), BF16)],
        compiler_params=_params(2),
        name="qkv_proj",
    )(x2, g.reshape(1, d), w)


_MB_SEL0 = 0
_MB_POS0 = 32
_MB_KILL = 40
_MB_BLK0 = 64


_ROW_CHUNK = 32


def _moba_kernel(q_ref, k_ref, v_ref, o_ref, kmean_ref, kx_ref, qaug_ref, m_ref, alpha_ref, acc_ref,
                 s_ref, p_ref, *, n_blk):
    pair = pl.program_id(1)
    qi = pl.program_id(2)
    blk = MOBA_BLOCK
    lane = lax.broadcasted_iota(jnp.int32, (blk, LANES), 1)
    row = lax.broadcasted_iota(jnp.int32, (blk, LANES), 0)
    lane_f = lane.astype(F32)
    slopes = [jnp.where(pair == 0, 2.0 ** (-2 * (hh + 1)), 2.0 ** (-2 * (hh + 3))).astype(F32)
              for hh in range(HEADS_PER_BLOCK)]

    @pl.when(qi == 0)
    def _():
        kmean_ref[...] = jnp.zeros_like(kmean_ref)

        def body(n, c):
            kb = k_ref[0, pl.ds(pl.multiple_of(n * blk, blk), blk), :].astype(F32)
            kmean_ref[pl.ds(n, 1), :] = jnp.sum(kb, axis=0, keepdims=True) * (1.0 / blk)
            return c

        lax.fori_loop(0, n_blk, body, 0)
        kx = jnp.zeros((blk, LANES), F32)
        for hh in range(HEADS_PER_BLOCK):
            kx = jnp.where(lane == _MB_POS0 + 2 * hh, slopes[hh] * ((row // LANES) * LANES).astype(F32), kx)
            kx = jnp.where(lane == _MB_POS0 + 2 * hh + 1, slopes[hh] * (row % LANES).astype(F32), kx)
        kx_ref[...] = kx.astype(BF16)

    q2 = q_ref[0]
    km_parts = _split3(kmean_ref[...])
    for hh in range(HEADS_PER_BLOCK):
        in_head = (lane >= hh * HEAD_DIM) & (lane < (hh + 1) * HEAD_DIM)
        qm = jnp.where(in_head, q2, jnp.zeros_like(q2))
        gate = _dot_nt(qm, km_parts[0]) + _dot_nt(qm, km_parts[1]) + _dot_nt(qm, km_parts[2])
        gate = jnp.where(lane < qi, gate, NEG_INF)
        gate = jnp.where(lane < n_blk, gate, -jnp.inf)
        sel = jnp.zeros((blk, LANES), jnp.bool_)
        for _ in range(MOBA_TOPK):
            gmax = jnp.max(gate, axis=1, keepdims=True)
            first = jnp.min(jnp.where(gate == gmax, lane_f, float(LANES)), axis=1, keepdims=True)
            pick = lane_f == first
            sel = sel | pick
            gate = jnp.where(pick, -jnp.inf, gate)
        sel = sel & (lane < qi)
        extra = jnp.where(sel, 0.0, NEG_INF)
        extra = jnp.where(lane >= n_blk, 0.0, extra)
        extra = jnp.where((lane == _MB_POS0 + 2 * hh) | (lane == _MB_POS0 + 2 * hh + 1) | (lane == _MB_KILL),
                          1.0, extra)
        blk_lane = lane - (_MB_BLK0 + 32 * hh)
        extra = jnp.where((blk_lane >= 0) & (blk_lane < 32),
                          slopes[hh] * (blk_lane * blk).astype(F32), extra)
        qaug_ref[hh] = jnp.concatenate([qm * jnp.asarray(SCALE, BF16), extra.astype(BF16)], axis=1)
        m_ref[hh] = jnp.full((blk, LANES), -jnp.inf, F32)
        acc_ref[hh] = jnp.zeros((blk, LANES), F32)

    lane1 = lax.broadcasted_iota(jnp.int32, (1, LANES), 1)
    rc = _ROW_CHUNK

    def score_stage(j, buf, *, is_own=False, valid=True):
        kj = k_ref[0, pl.ds(pl.multiple_of(j * blk, blk), blk), :]
        ind = (lane1 % 32 == j) & (lane1 >= _MB_BLK0) if is_own else \
              (lane1 % 32 == j) & ((lane1 < 32) | (lane1 >= _MB_BLK0))
        kill = jnp.where(lane1 == _MB_KILL, jnp.where(valid, 0.0, NEG_INF), 0.0)
        side = jnp.broadcast_to(jnp.where(ind, 1.0, kill), (blk, LANES)).astype(BF16)
        kx = jnp.where(ind | (lane1 == _MB_KILL), side, kx_ref[...])
        k_aug = jnp.concatenate([kj, kx], axis=1)
        for hh in range(HEADS_PER_BLOCK):
            s_ref[buf, hh] = _dot_nt(qaug_ref[hh], k_aug)

    def softmax_stage(buf, *, is_own=False):
        for hh in range(HEADS_PER_BLOCK):
            for c in range(blk // rc):
                rows = slice(c * rc, (c + 1) * rc)
                s = s_ref[buf, hh, rows, :]
                if is_own:
                    col_c = lax.broadcasted_iota(jnp.int32, (rc, blk), 1)
                    row_c = lax.broadcasted_iota(jnp.int32, (rc, blk), 0) + c * rc
                    s = jnp.where(col_c <= row_c, s, NEG_INF)
                m_old = m_ref[hh, rows, :]
                m_new = jnp.maximum(m_old, jnp.max(s, axis=1, keepdims=True))
                alpha_ref[buf, hh, rows, :] = jnp.exp(m_old - m_new)
                m_ref[hh, rows, :] = m_new
                p = jnp.exp(s - jnp.concatenate([m_new, m_new], axis=1))
                p_ref[buf, hh, rows, :] = p.astype(BF16)

    def value_stage(j, buf):
        vj = v_ref[0, pl.ds(pl.multiple_of(j * blk, blk), blk), :]
        for hh in range(HEADS_PER_BLOCK):
            in_head1 = (lane1 >= hh * HEAD_DIM) & (lane1 < (hh + 1) * HEAD_DIM)
            v_aug = jnp.where(in_head1, vj, jnp.ones_like(vj))
            acc_ref[hh] = alpha_ref[buf, hh] * acc_ref[hh] + _dot(p_ref[buf, hh], v_aug)

    score_stage(qi, 0, is_own=True)
    softmax_stage(0, is_own=True)
    value_stage(qi, 0)

    n_past = qi

    @pl.when(n_past > 0)
    def _():
        s_ref[1] = jnp.full(s_ref.shape[1:], NEG_INF, F32)
        p_ref[...] = jnp.zeros_like(p_ref)
        alpha_ref[...] = jnp.ones_like(alpha_ref)
        last = n_past - 1

        def pair(tt, c):
            t = 2 * tt
            score_stage(jnp.minimum(t, last), 0, valid=t < n_past)
            softmax_stage(1)
            value_stage(jnp.clip(t - 2, 0, last), 0)
            score_stage(jnp.minimum(t + 1, last), 1, valid=t + 1 < n_past)
            softmax_stage(0)
            value_stage(jnp.clip(t - 1, 0, last), 1)
            return c

        lax.fori_loop(0, (n_past + 3) // 2, pair, 0)

    acc0 = acc_ref[0]
    acc1 = acc_ref[1]
    o0 = acc0 / pltpu.roll(acc0, HEAD_DIM, axis=1)
    o1 = acc1 / pltpu.roll(acc1, HEAD_DIM, axis=1)
    o_ref[0] = jnp.where(lane < HEAD_DIM, o0, o1).astype(BF16)


def _moba_attention(qkv):
    b, s, _ = qkv.shape
    blk = MOBA_BLOCK
    n_blk = s // blk
    assert s % blk == 0 and MOBA_TOPK <= n_blk - 1 and n_blk <= 32
    n_pairs = N_HEADS_MOBA // HEADS_PER_BLOCK
    return pl.pallas_call(
        functools.partial(_moba_kernel, n_blk=n_blk),
        out_shape=jax.ShapeDtypeStruct((b, s, N_HEADS_MOBA * HEAD_DIM), BF16),
        grid=(b, n_pairs, n_blk),
        in_specs=[pl.BlockSpec((1, blk, LANES), lambda bi, p, i: (bi, i, Q_BLK0 + p)),
                  pl.BlockSpec((1, s, LANES), lambda bi, p, i: (bi, 0, K_BLK0 + p)),
                  pl.BlockSpec((1, s, LANES), lambda bi, p, i: (bi, 0, V_BLK0 + p))],
        out_specs=pl.BlockSpec((1, blk, LANES), lambda bi, p, i: (bi, i, p)),
        scratch_shapes=[pltpu.VMEM((LANES, LANES), F32),
                        pltpu.VMEM((blk, LANES), BF16),
                        pltpu.VMEM((HEADS_PER_BLOCK, blk, 2 * LANES), BF16),
                        pltpu.VMEM((HEADS_PER_BLOCK, blk, LANES), F32),
                        pltpu.VMEM((2, HEADS_PER_BLOCK, blk, LANES), F32),
                        pltpu.VMEM((HEADS_PER_BLOCK, blk, LANES), F32),
                        pltpu.VMEM((2, HEADS_PER_BLOCK, blk, blk), F32),
                        pltpu.VMEM((2, HEADS_PER_BLOCK, blk, blk), BF16)],
        compiler_params=_params(3),
        name="moba_attn",
    )(qkv, qkv, qkv)


SB_TILE = 256


def _softplus(z):
    return jnp.maximum(z, 0.0) + jnp.log(1.0 + jnp.exp(-jnp.abs(z)))


_SB_SLOTS = 3
SB_DEAD_MASS = 128.0


def _sb_kernel(q_ref, k_ref, v_ref, u_ref, o_ref, qaug_ref, carry_ref, acc_ref,
               z_ref, sphl_ref, c_ref, rs_ref, a_ref):
    qi = pl.program_id(2)
    t = SB_TILE
    rc = _ROW_CHUNK
    lane = lax.broadcasted_iota(jnp.int32, (t, LANES), 1)
    lane1 = lax.broadcasted_iota(jnp.int32, (1, LANES), 1)
    q2 = q_ref[0]
    for hh in range(HEADS_PER_BLOCK):
        in_head = (lane >= hh * HEAD_DIM) & (lane < (hh + 1) * HEAD_DIM)
        qm = jnp.where(in_head, q2, jnp.zeros_like(q2)) * jnp.asarray(SCALE, BF16)
        qaug_ref[hh] = jnp.concatenate([qm, jnp.where(lane == 0, 1.0, 0.0).astype(BF16)], axis=1)

    def score_stage(j, slot, *, valid=True):
        kj = k_ref[0, pl.ds(pl.multiple_of(j * t, t), t), :]
        kill = jnp.where(lane1 == 0, jnp.where(valid, 0.0, NEG_INF), 0.0)
        k_aug = jnp.concatenate([kj, jnp.broadcast_to(kill, (t, LANES)).astype(BF16)], axis=1)
        for hh in range(HEADS_PER_BLOCK):
            z_ref[slot, hh] = _dot_nt(qaug_ref[hh], k_aug)

    def _past_mask(c):
        col_c = lax.broadcasted_iota(jnp.int32, (rc, t), 1)
        row_c = lax.broadcasted_iota(jnp.int32, (rc, t), 0) + c * rc
        return col_c < row_c

    def softplus_stage(slot, *, diagonal=False):
        for hh in range(HEADS_PER_BLOCK):
            for c in range(t // rc):
                rows = slice(c * rc, (c + 1) * rc)
                sp = _softplus(z_ref[slot, hh, rows, :])
                if diagonal:
                    sp = jnp.where(_past_mask(c), sp, 0.0)
                hi, lo = _split2(sp)
                sphl_ref[slot, pl.ds((2 * hh) * t + c * rc, rc), :] = hi
                sphl_ref[slot, pl.ds((2 * hh + 1) * t + c * rc, rc), :] = lo
                rs_ref[slot, hh, rows, :] = jnp.broadcast_to(jnp.sum(sp, axis=1, keepdims=True), (rc, LANES))

    def suffix_stage(slot):
        c_ref[slot] = _dot(sphl_ref[slot], u_ref[...])

    def weight_stage(slot, *, diagonal=False):
        for hh in range(HEADS_PER_BLOCK):
            for c in range(t // rc):
                rows = slice(c * rc, (c + 1) * rc)
                z = z_ref[slot, hh, rows, :]
                cc = c_ref[slot, pl.ds((2 * hh) * t + c * rc, rc), :] + \
                    c_ref[slot, pl.ds((2 * hh + 1) * t + c * rc, rc), :]
                if diagonal:
                    a = jnp.where(_past_mask(c), jnp.exp(z - cc), 0.0)
                    carry_ref[hh, rows, :] = rs_ref[slot, hh, rows, :]
                else:
                    carry = carry_ref[hh, rows, :]
                    a = jnp.exp(z - (cc + jnp.concatenate([carry, carry], axis=1)))
                    carry_ref[hh, rows, :] = carry + rs_ref[slot, hh, rows, :]
                a_ref[slot, hh, rows, :] = a.astype(BF16)

    def value_stage(j, slot, *, first=False):
        vj = v_ref[0, pl.ds(pl.multiple_of(j * t, t), t), :]
        for hh in range(HEADS_PER_BLOCK):
            av = _dot(a_ref[slot, hh], vj)
            acc_ref[hh] = av if first else acc_ref[hh] + av

    score_stage(qi, 0)
    softplus_stage(0, diagonal=True)
    suffix_stage(0)
    weight_stage(0, diagonal=True)
    value_stage(qi, 0, first=True)

    n_past = qi

    @pl.when(n_past > 0)
    def _():
        z_ref[1] = jnp.full(z_ref.shape[1:], NEG_INF, F32)
        z_ref[2] = jnp.full(z_ref.shape[1:], NEG_INF, F32)
        c_ref[1] = jnp.zeros(c_ref.shape[1:], F32)
        rs_ref[1] = jnp.zeros(rs_ref.shape[1:], F32)
        last = n_past - 1

        def key_tile(i):
            return jnp.clip(last - i, 0, last)

        def trip(state):
            tt, _ = state
            for k in range(_SB_SLOTS):
                i = _SB_SLOTS * tt + k
                score_stage(key_tile(i), k, valid=i < n_past)
                softplus_stage((k - 1) % _SB_SLOTS)
                suffix_stage((k - 1) % _SB_SLOTS)
                weight_stage((k - 2) % _SB_SLOTS)
                value_stage(key_tile(i - 2), (k - 2) % _SB_SLOTS)
            return tt + 1, jnp.min(carry_ref[...])

        n_trips = (n_past + 2 + _SB_SLOTS - 1) // _SB_SLOTS
        lax.while_loop(lambda st: (st[0] < n_trips) & (st[1] < SB_DEAD_MASS), trip,
                       (jnp.int32(0), jnp.min(carry_ref[...])))

    o_ref[0] = jnp.where(lane < HEAD_DIM, acc_ref[0], acc_ref[1]).astype(BF16)


def _sb_attention(qkv):
    b, s, _ = qkv.shape
    t = SB_TILE
    assert s % t == 0
    n_pairs = N_HEADS_SB // HEADS_PER_BLOCK
    u = (lax.broadcasted_iota(jnp.int32, (t, t), 0) >= lax.broadcasted_iota(jnp.int32, (t, t), 1)).astype(BF16)
    return pl.pallas_call(
        _sb_kernel,
        out_shape=jax.ShapeDtypeStruct((b, s, N_HEADS_SB * HEAD_DIM), BF16),
        grid=(b, n_pairs, s // t),
        in_specs=[pl.BlockSpec((1, t, LANES), lambda bi, p, i: (bi, i, Q_BLK0 + SB_BLK + p)),
                  pl.BlockSpec((1, s, LANES), lambda bi, p, i: (bi, 0, K_BLK0 + SB_BLK + p)),
                  pl.BlockSpec((1, s, LANES), lambda bi, p, i: (bi, 0, V_BLK0 + SB_BLK + p)),
                  pl.BlockSpec((t, t), lambda bi, p, i: (0, 0))],
        out_specs=pl.BlockSpec((1, t, LANES), lambda bi, p, i: (bi, i, p)),
        scratch_shapes=[pltpu.VMEM((HEADS_PER_BLOCK, t, 2 * LANES), BF16),
                        pltpu.VMEM((HEADS_PER_BLOCK, t, LANES), F32),
                        pltpu.VMEM((HEADS_PER_BLOCK, t, LANES), F32),
                        pltpu.VMEM((_SB_SLOTS, HEADS_PER_BLOCK, t, t), F32),
                        pltpu.VMEM((_SB_SLOTS, 2 * HEADS_PER_BLOCK * t, t), BF16),
                        pltpu.VMEM((_SB_SLOTS, 2 * HEADS_PER_BLOCK * t, t), F32),
                        pltpu.VMEM((_SB_SLOTS, HEADS_PER_BLOCK, t, LANES), F32),
                        pltpu.VMEM((_SB_SLOTS, HEADS_PER_BLOCK, t, t), BF16)],
        compiler_params=_params(3),
        name="sb_attn",
    )(qkv, qkv, qkv, u)


def _dil_kernel(q_ref, k_ref, v_ref, kp_ref, vp_ref, o_ref, lse_ref, bias_ref, *, dilation, n_sub):
    pair = pl.program_id(2)
    i = pl.program_id(3)
    n = DIL_N
    lane = lax.broadcasted_iota(jnp.int32, (n, LANES), 1)

    @pl.when(i == 0)
    def _():
        col = lax.broadcasted_iota(jnp.int32, (n, 2 * n), 1)
        row = lax.broadcasted_iota(jnp.int32, (n, 2 * n), 0)
        delta = row + n - col
        valid = (delta >= 0) & (delta <= n)
        for hh in range(HEADS_PER_BLOCK):
            slope = jnp.asarray(2.0 ** -(hh + 1), F32)
            for p in range(1, N_HEADS_DIL // HEADS_PER_BLOCK):
                slope = jnp.where(pair == p, 2.0 ** -(2 * p + hh + 1), slope)
            bias_ref[hh] = jnp.where(valid, -slope * (delta * dilation).astype(F32), NEG_INF)

    q2 = q_ref[0]
    outs = []
    lses = []
    for sb in range(n_sub):
        qs = q2[sb * n:(sb + 1) * n]
        if sb == 0:
            kk = jnp.concatenate([kp_ref[0], k_ref[0, 0:n]], axis=0)
            vv = jnp.concatenate([vp_ref[0], v_ref[0, 0:n]], axis=0)
        else:
            kk = k_ref[0, (sb - 1) * n:(sb + 1) * n]
            vv = v_ref[0, (sb - 1) * n:(sb + 1) * n]
        o_h = []
        lse_h = []
        for hh in range(HEADS_PER_BLOCK):
            in_head = (lane >= hh * HEAD_DIM) & (lane < (hh + 1) * HEAD_DIM)
            qm = jnp.where(in_head, qs, jnp.zeros_like(qs)) * jnp.asarray(SCALE, BF16)
            s = _dot_nt(qm, kk) + bias_ref[hh]
            if sb == 0:
                colh = lax.broadcasted_iota(jnp.int32, (n, 2 * n), 1)
                s = s + jnp.where(colh < n, jnp.where(i == 0, NEG_INF, 0.0), 0.0)
            m = jnp.max(s, axis=1, keepdims=True)
            p = jnp.exp(s - m)
            den = jnp.sum(p, axis=1, keepdims=True)
            o_h.append(_dot(p.astype(BF16), vv) / den)
            lse_h.append(jnp.broadcast_to(m + jnp.log(den), (n, LANES)))
        outs.append(jnp.where(lane < HEAD_DIM, o_h[0], o_h[1]))
        lses.append(jnp.where(lane < HEAD_DIM, lse_h[0], lse_h[1]))
    o_ref[0] = jnp.concatenate(outs, axis=0).astype(BF16)
    lse_ref[0] = jnp.concatenate(lses, axis=0)


def _dilated_branch(qkv, dilation):
    b, s, w3 = qkv.shape
    n = DIL_N
    l = s // dilation
    assert l % n == 0
    tq = min(512, l)
    n_sub = tq // n
    sub_per_tile = tq // n
    n_pairs = N_HEADS_DIL // HEADS_PER_BLOCK
    wd = N_HEADS_DIL * HEAD_DIM
    qv = qkv.reshape(b, l, dilation * w3)

    def qmap(off):
        return lambda bi, r, p, i: (bi, i, r * ROW_BLKS + off + DIL_BLK + p)

    def pmap(off):
        return lambda bi, r, p, i: (bi, jnp.maximum(i * sub_per_tile - 1, 0), r * ROW_BLKS + off + DIL_BLK + p)

    omap = lambda bi, r, p, i: (bi, i, r * (wd // LANES) + p)
    o, lse = pl.pallas_call(
        functools.partial(_dil_kernel, dilation=dilation, n_sub=n_sub),
        out_shape=(jax.ShapeDtypeStruct((b, l, dilation * wd), BF16),
                   jax.ShapeDtypeStruct((b, l, dilation * wd), F32)),
        grid=(b, dilation, n_pairs, l // tq),
        in_specs=[pl.BlockSpec((1, tq, LANES), qmap(Q_BLK0)),
                  pl.BlockSpec((1, tq, LANES), qmap(K_BLK0)),
                  pl.BlockSpec((1, tq, LANES), qmap(V_BLK0)),
                  pl.BlockSpec((1, n, LANES), pmap(K_BLK0)),
                  pl.BlockSpec((1, n, LANES), pmap(V_BLK0))],
        out_specs=(pl.BlockSpec((1, tq, LANES), omap), pl.BlockSpec((1, tq, LANES), omap)),
        scratch_shapes=[pltpu.VMEM((HEADS_PER_BLOCK, n, 2 * n), F32)],
        compiler_params=_params(4),
        name=f"dilated_attn_d{dilation}",
    )(qv, qv, qv, qv, qv)
    return o.reshape(b, s, wd), lse.reshape(b, s, wd)


def _rms(x, g):
    return x * lax.rsqrt(jnp.mean(x * x, axis=-1, keepdims=True) + NORM_EPS) * g


def _outproj_kernel(x_ref, oa_ref, ob_ref, o1_ref, o2_ref, o3_ref, l1_ref, l2_ref, l3_ref,
                    g_ref, w_ref, out_ref):
    l1, l2, l3 = l1_ref[...], l2_ref[...], l3_ref[...]
    lmax = jnp.maximum(jnp.maximum(l1, l2), l3)
    e1, e2, e3 = jnp.exp(l1 - lmax), jnp.exp(l2 - lmax), jnp.exp(l3 - lmax)
    oc = (e1 * o1_ref[...].astype(F32) + e2 * o2_ref[...].astype(F32) + e3 * o3_ref[...].astype(F32)) \
        / (e1 + e2 + e3)
    g = g_ref[...]
    wa = N_HEADS_MOBA * HEAD_DIM
    wb = wa + N_HEADS_SB * HEAD_DIM
    y = jnp.concatenate([_rms(oa_ref[...].astype(F32), g[:, :wa]),
                         _rms(ob_ref[...].astype(F32), g[:, wa:wb]),
                         _rms(oc, g[:, wb:])], axis=1).astype(BF16)
    out_ref[...] = x_ref[...] + _dot(y, w_ref[...])


def _out_proj(x2, oa, ob, ocs, lses, g, w, *, tm=512):
    t, d = x2.shape
    wa = oa.shape[1]
    wb = ob.shape[1]
    wc = ocs[0].shape[1]
    row = lambda i: (i, 0)
    const = lambda i: (0, 0)
    return pl.pallas_call(
        _outproj_kernel,
        out_shape=jax.ShapeDtypeStruct((t, d), F32),
        grid=(t // tm,),
        in_specs=[pl.BlockSpec((tm, d), row),
                  pl.BlockSpec((tm, wa), row), pl.BlockSpec((tm, wb), row),
                  pl.BlockSpec((tm, wc), row), pl.BlockSpec((tm, wc), row), pl.BlockSpec((tm, wc), row),
                  pl.BlockSpec((tm, wc), row), pl.BlockSpec((tm, wc), row), pl.BlockSpec((tm, wc), row),
                  pl.BlockSpec((1, d), const), pl.BlockSpec((d, d), const)],
        out_specs=pl.BlockSpec((tm, d), row),
        compiler_params=_params(1),
        name="out_proj",
    )(x2, oa, ob, *ocs, *lses, g.reshape(1, d), w)


_EXP_LANE0 = N_GROUPS


def _moe_kernel(x_ref, g_ref, wrh_ref, wrl_ref, br_ref, wg_ref, wu_ref, wd_ref, out_ref,
                h_ref, comb_ref, acc_ref):
    e = pl.program_id(1)
    tm = x_ref.shape[0]
    lane = lax.broadcasted_iota(jnp.int32, (tm, LANES), 1)

    @pl.when(e == 0)
    def _():
        x = x_ref[...]
        h = _rms(x, g_ref[...])
        h_ref[...] = h.astype(BF16)
        hh, hl = _split2(h)
        logits = _dot(hh, wrh_ref[...]) + _dot(hh, wrl_ref[...]) + _dot(hl, wrh_ref[...]) + br_ref[...]
        lane_f = lane.astype(F32)
        big = float(LANES)
        gl = jnp.where(lane < N_GROUPS, logits, -jnp.inf)
        gmax = jnp.max(gl, axis=1, keepdims=True)
        gidx = jnp.min(jnp.where(gl == gmax, lane_f, big), axis=1, keepdims=True)
        g_w = 1.0 / jnp.sum(jnp.exp(gl - gmax), axis=1, keepdims=True)
        lane_group = ((lane - _EXP_LANE0) // EXPERTS_PER_GROUP).astype(F32)
        in_group = (lane >= _EXP_LANE0) & (lane < _EXP_LANE0 + N_EXPERTS) & (lane_group == gidx)
        el = jnp.where(in_group, logits, -jnp.inf)
        v1 = jnp.max(el, axis=1, keepdims=True)
        i1 = jnp.min(jnp.where(el == v1, lane_f, big), axis=1, keepdims=True)
        el2 = jnp.where(lane_f == i1, -jnp.inf, el)
        v2 = jnp.max(el2, axis=1, keepdims=True)
        i2 = jnp.min(jnp.where(el2 == v2, lane_f, big), axis=1, keepdims=True)
        r = jnp.exp(v2 - v1)
        w1 = g_w / (1.0 + r)
        w2 = g_w * r / (1.0 + r)
        comb_ref[...] = jnp.where(lane_f == i1, w1, 0.0) + jnp.where(lane_f == i2, w2, 0.0)
        acc_ref[...] = jnp.zeros_like(acc_ref)

    h = h_ref[...]
    gate = _dot(h, wg_ref[0])
    up = _dot(h, wu_ref[0])
    cw = jnp.sum(jnp.where(lane == e + _EXP_LANE0, comb_ref[...], 0.0), axis=1, keepdims=True)
    act = gate / (1.0 + jnp.exp(-gate)) * up * cw
    acc_ref[...] += _dot(act.astype(BF16), wd_ref[0])

    @pl.when(e == N_EXPERTS - 1)
    def _():
        out_ref[...] = x_ref[...] + acc_ref[...]


def _moe(x2, g, wr_hi, wr_lo, br, wg, wu, wd, *, tm=1024):
    t, d = x2.shape
    f = wg.shape[2]
    row = lambda i, e: (i, 0)
    const = lambda i, e: (0, 0)
    return pl.pallas_call(
        _moe_kernel,
        out_shape=jax.ShapeDtypeStruct((t, d), F32),
        grid=(t // tm, N_EXPERTS),
        in_specs=[pl.BlockSpec((tm, d), row),
                  pl.BlockSpec((1, d), const),
                  pl.BlockSpec((d, LANES), const), pl.BlockSpec((d, LANES), const),
                  pl.BlockSpec((1, LANES), const),
                  pl.BlockSpec((1, d, f), lambda i, e: (e, 0, 0)),
                  pl.BlockSpec((1, d, f), lambda i, e: (e, 0, 0)),
                  pl.BlockSpec((1, f, d), lambda i, e: (e, 0, 0))],
        out_specs=pl.BlockSpec((tm, d), row),
        scratch_shapes=[pltpu.VMEM((tm, d), BF16), pltpu.VMEM((tm, LANES), F32), pltpu.VMEM((tm, d), F32)],
        compiler_params=_params(2),
        name="hier_moe",
    )(x2, g.reshape(1, d), wr_hi, wr_lo, br, wg, wu, wd)


def _final_norm_kernel(x_ref, g_ref, o_ref):
    o_ref[...] = _rms(x_ref[...], g_ref[...])


def _final_norm(x2, g, *, tm=1024):
    t, d = x2.shape
    return pl.pallas_call(
        _final_norm_kernel,
        out_shape=jax.ShapeDtypeStruct((t, d), F32),
        grid=(t // tm,),
        in_specs=[pl.BlockSpec((tm, d), lambda i: (i, 0)), pl.BlockSpec((1, d), lambda i: (0, 0))],
        out_specs=pl.BlockSpec((tm, d), lambda i: (i, 0)),
        compiler_params=_params(1),
        name="final_norm",
    )(x2, g.reshape(1, d))


def _router_weights(w_gr, b_gr, w_er, b_er):
    d = w_gr.shape[0]
    w = jnp.concatenate([w_gr, jnp.moveaxis(w_er, 0, 1).reshape(d, N_EXPERTS)], axis=1)
    w = jnp.pad(w, ((0, 0), (0, LANES - w.shape[1])))
    bias = jnp.pad(jnp.concatenate([b_gr, b_er.reshape(-1)]), (0, LANES - N_GROUPS - N_EXPERTS))
    hi, lo = _split2(w)
    return hi, lo, bias.reshape(1, LANES)


def _layer(x2, b, s, ln1_g, w_in, mix_g, w_out, ln2_g, w_gr, b_gr, w_er, b_er, w_gate, w_up, w_down):
    t, d = x2.shape
    qkv = _qkv_proj(x2, ln1_g, w_in.astype(BF16)).reshape(b, s, 3 * d)
    oa = _moba_attention(qkv)
    ob = _sb_attention(qkv)
    ocs, lses = [], []
    for _, dilation in DIL_CONFIGS:
        o, lse = _dilated_branch(qkv, dilation)
        ocs.append(o.reshape(t, -1))
        lses.append(lse.reshape(t, -1))
    x2 = _out_proj(x2, oa.reshape(t, -1), ob.reshape(t, -1), ocs, lses, mix_g, w_out.astype(BF16))
    wr_hi, wr_lo, br = _router_weights(w_gr, b_gr, w_er, b_er)
    f = w_gate.shape[-1]
    return _moe(x2, ln2_g, wr_hi, wr_lo, br,
                w_gate.reshape(N_EXPERTS, d, f).astype(BF16),
                w_up.reshape(N_EXPERTS, d, f).astype(BF16),
                w_down.reshape(N_EXPERTS, f, d).astype(BF16))


def kernel(x, ln1_g, w_in, mix_norm_g, w_out, ln2_g, w_group_router, b_group_router,
           w_expert_router, b_expert_router, w_gate, w_up, w_down, final_norm_g):
    b, s, d = x.shape
    x2 = x.reshape(b * s, d)
    for l in range(ln1_g.shape[0]):
        x2 = _layer(x2, b, s, ln1_g[l], w_in[l], mix_norm_g[l], w_out[l], ln2_g[l],
                    w_group_router[l], b_group_router[l], w_expert_router[l], b_expert_router[l],
                    w_gate[l], w_up[l], w_down[l])
    return _final_norm(x2, final_norm_g).reshape(b, s, d)
```

```python
import functools

import jax
import jax.numpy as jnp
from jax import lax
from jax.experimental import pallas as pl
from jax.experimental.pallas import tpu as pltpu

F32 = jnp.float32
BF16 = jnp.bfloat16

D_MODEL = 1024
HEAD_DIM = 64
N_HEADS = 16
LANES = 128
HEADS_PER_BLOCK = LANES // HEAD_DIM
N_HEADS_MOBA = 4
N_HEADS_SB = 4
N_HEADS_DIL = 8
MOBA_BLOCK = 256
MOBA_TOPK = 3
DIL_CONFIGS = ((128, 1), (512, 4), (2048, 16))
DIL_N = 128
N_GROUPS = 4
EXPERTS_PER_GROUP = 4
N_EXPERTS = N_GROUPS * EXPERTS_PER_GROUP
D_EXPERT = 256
NORM_EPS = 1e-6
NEG_INF = -1e30
SCALE = HEAD_DIM ** -0.5

Q_BLK0 = 0
K_BLK0 = D_MODEL // LANES
V_BLK0 = 2 * D_MODEL // LANES
ROW_BLKS = 3 * D_MODEL // LANES
SB_BLK = N_HEADS_MOBA // HEADS_PER_BLOCK
DIL_BLK = (N_HEADS_MOBA + N_HEADS_SB) // HEADS_PER_BLOCK

VMEM_LIMIT = 56 * 1024 * 1024


def _params(n_axes, vmem=VMEM_LIMIT):
    return pltpu.CompilerParams(dimension_semantics=("arbitrary",) * n_axes,
                                vmem_limit_bytes=vmem)


def _dot_nt(a, b):
    return lax.dot_general(a, b, (((1,), (1,)), ((), ())), preferred_element_type=F32)


def _dot(a, b):
    return jnp.dot(a, b, preferred_element_type=F32)


def _split3(x):
    hi = x.astype(BF16)
    r1 = x - hi.astype(F32)
    mid = r1.astype(BF16)
    lo = (r1 - mid.astype(F32)).astype(BF16)
    return hi, mid, lo


def _split2(x):
    hi = x.astype(BF16)
    lo = (x - hi.astype(F32)).astype(BF16)
    return hi, lo


def _qkv_kernel(x_ref, g_ref, w_ref, o_ref, h_ref):
    @pl.when(pl.program_id(1) == 0)
    def _():
        x = x_ref[...]
        ms = jnp.mean(x * x, axis=-1, keepdims=True)
        h_ref[...] = (x * lax.rsqrt(ms + NORM_EPS) * g_ref[...]).astype(BF16)

    o_ref[...] = _dot(h_ref[...], w_ref[...]).astype(BF16)


def _qkv_proj(x2, g, w, *, tm=512, tn=1024):
    t, d = x2.shape
    n = w.shape[1]
    return pl.pallas_call(
        _qkv_kernel,
        out_shape=jax.ShapeDtypeStruct((t, n), BF16),
        grid=(t // tm, n // tn),
        in_specs=[pl.BlockSpec((tm, d), lambda i, j: (i, 0)),
                  pl.BlockSpec((1, d), lambda i, j: (0, 0)),
                  pl.BlockSpec((d, tn), lambda i, j: (0, j))],
        out_specs=pl.BlockSpec((tm, tn), lambda i, j: (i, j)),
        scratch_shapes=[pltpu.VMEM((tm, d), BF16)],
        compiler_params=_params(2),
        name="qkv_proj",
    )(x2, g.reshape(1, d), w)


_MB_SEL0 = 0
_MB_POS0 = 32
_MB_KILL = 40
_MB_BLK0 = 64


_ROW_CHUNK = 32


def _moba_kernel(q_ref, k_ref, v_ref, o_ref, kmean_ref, kx_ref, qaug_ref, m_ref, alpha_ref, acc_ref,
                 s_ref, p_ref, *, n_blk):
    pair = pl.program_id(1)
    qi = pl.program_id(2)
    blk = MOBA_BLOCK
    lane = lax.broadcasted_iota(jnp.int32, (blk, LANES), 1)
    row = lax.broadcasted_iota(jnp.int32, (blk, LANES), 0)
    lane_f = lane.astype(F32)
    slopes = [jnp.where(pair == 0, 2.0 ** (-2 * (hh + 1)), 2.0 ** (-2 * (hh + 3))).astype(F32)
              for hh in range(HEADS_PER_BLOCK)]

    @pl.when(qi == 0)
    def _():
        kmean_ref[...] = jnp.zeros_like(kmean_ref)

        def body(n, c):
            kb = k_ref[0, pl.ds(pl.multiple_of(n * blk, blk), blk), :].astype(F32)
            kmean_ref[pl.ds(n, 1), :] = jnp.sum(kb, axis=0, keepdims=True) * (1.0 / blk)
            return c

        lax.fori_loop(0, n_blk, body, 0)
        kx = jnp.zeros((blk, LANES), F32)
        for hh in range(HEADS_PER_BLOCK):
            kx = jnp.where(lane == _MB_POS0 + 2 * hh, slopes[hh] * ((row // LANES) * LANES).astype(F32), kx)
            kx = jnp.where(lane == _MB_POS0 + 2 * hh + 1, slopes[hh] * (row % LANES).astype(F32), kx)
        kx_ref[...] = kx.astype(BF16)

    q2 = q_ref[0]
    km_parts = _split3(kmean_ref[...])
    for hh in range(HEADS_PER_BLOCK):
        in_head = (lane >= hh * HEAD_DIM) & (lane < (hh + 1) * HEAD_DIM)
        qm = jnp.where(in_head, q2, jnp.zeros_like(q2))
        gate = _dot_nt(qm, km_parts[0]) + _dot_nt(qm, km_parts[1]) + _dot_nt(qm, km_parts[2])
        gate = jnp.where(lane < qi, gate, NEG_INF)
        gate = jnp.where(lane < n_blk, gate, -jnp.inf)
        sel = jnp.zeros((blk, LANES), jnp.bool_)
        for _ in range(MOBA_TOPK):
            gmax = jnp.max(gate, axis=1, keepdims=True)
            first = jnp.min(jnp.where(gate == gmax, lane_f, float(LANES)), axis=1, keepdims=True)
            pick = lane_f == first
            sel = sel | pick
            gate = jnp.where(pick, -jnp.inf, gate)
        sel = sel & (lane < qi)
        extra = jnp.where(sel, 0.0, NEG_INF)
        extra = jnp.where(lane >= n_blk, 0.0, extra)
        extra = jnp.where((lane == _MB_POS0 + 2 * hh) | (lane == _MB_POS0 + 2 * hh + 1) | (lane == _MB_KILL),
                          1.0, extra)
        blk_lane = lane - (_MB_BLK0 + 32 * hh)
        extra = jnp.where((blk_lane >= 0) & (blk_lane < 32),
                          slopes[hh] * (blk_lane * blk).astype(F32), extra)
        qaug_ref[hh] = jnp.concatenate([qm * jnp.asarray(SCALE, BF16), extra.astype(BF16)], axis=1)
        m_ref[hh] = jnp.full((blk, LANES), -jnp.inf, F32)
        acc_ref[hh] = jnp.zeros((blk, LANES), F32)

    lane1 = lax.broadcasted_iota(jnp.int32, (1, LANES), 1)
    rc = _ROW_CHUNK

    def score_stage(j, buf, *, is_own=False, valid=True):
        kj = k_ref[0, pl.ds(pl.multiple_of(j * blk, blk), blk), :]
        ind = (lane1 % 32 == j) & (lane1 >= _MB_BLK0) if is_own else \
              (lane1 % 32 == j) & ((lane1 < 32) | (lane1 >= _MB_BLK0))
        kill = jnp.where(lane1 == _MB_KILL, jnp.where(valid, 0.0, NEG_INF), 0.0)
        side = jnp.broadcast_to(jnp.where(ind, 1.0, kill), (blk, LANES)).astype(BF16)
        kx = jnp.where(ind | (lane1 == _MB_KILL), side, kx_ref[...])
        k_aug = jnp.concatenate([kj, kx], axis=1)
        for hh in range(HEADS_PER_BLOCK):
            s_ref[buf, hh] = _dot_nt(qaug_ref[hh], k_aug)

    def softmax_stage(buf, *, is_own=False):
        for hh in range(HEADS_PER_BLOCK):
            for c in range(blk // rc):
                rows = slice(c * rc, (c + 1) * rc)
                s = s_ref[buf, hh, rows, :]
                if is_own:
                    col_c = lax.broadcasted_iota(jnp.int32, (rc, blk), 1)
                    row_c = lax.broadcasted_iota(jnp.int32, (rc, blk), 0) + c * rc
                    s = jnp.where(col_c <= row_c, s, NEG_INF)
                m_old = m_ref[hh, rows, :]
                m_new = jnp.maximum(m_old, jnp.max(s, axis=1, keepdims=True))
                alpha_ref[buf, hh, rows, :] = jnp.exp(m_old - m_new)
                m_ref[hh, rows, :] = m_new
                p = jnp.exp(s - jnp.concatenate([m_new, m_new], axis=1))
                p_ref[buf, hh, rows, :] = p.astype(BF16)

    def value_stage(j, buf):
        vj = v_ref[0, pl.ds(pl.multiple_of(j * blk, blk), blk), :]
        for hh in range(HEADS_PER_BLOCK):
            in_head1 = (lane1 >= hh * HEAD_DIM) & (lane1 < (hh + 1) * HEAD_DIM)
            v_aug = jnp.where(in_head1, vj, jnp.ones_like(vj))
            acc_ref[hh] = alpha_ref[buf, hh] * acc_ref[hh] + _dot(p_ref[buf, hh], v_aug)

    score_stage(qi, 0, is_own=True)
    softmax_stage(0, is_own=True)
    value_stage(qi, 0)

    n_past = qi

    @pl.when(n_past > 0)
    def _():
        s_ref[1] = jnp.full(s_ref.shape[1:], NEG_INF, F32)
        p_ref[...] = jnp.zeros_like(p_ref)
        alpha_ref[...] = jnp.ones_like(alpha_ref)
        last = n_past - 1

        def pair(tt, c):
            t = 2 * tt
            score_stage(jnp.minimum(t, last), 0, valid=t < n_past)
            softmax_stage(1)
            value_stage(jnp.clip(t - 2, 0, last), 0)
            score_stage(jnp.minimum(t + 1, last), 1, valid=t + 1 < n_past)
            softmax_stage(0)
            value_stage(jnp.clip(t - 1, 0, last), 1)
            return c

        lax.fori_loop(0, (n_past + 3) // 2, pair, 0)

    acc0 = acc_ref[0]
    acc1 = acc_ref[1]
    o0 = acc0 / pltpu.roll(acc0, HEAD_DIM, axis=1)
    o1 = acc1 / pltpu.roll(acc1, HEAD_DIM, axis=1)
    o_ref[0] = jnp.where(lane < HEAD_DIM, o0, o1).astype(BF16)


def _moba_attention(qkv):
    b, s, _ = qkv.shape
    blk = MOBA_BLOCK
    n_blk = s // blk
    assert s % blk == 0 and MOBA_TOPK <= n_blk - 1 and n_blk <= 32
    n_pairs = N_HEADS_MOBA // HEADS_PER_BLOCK
    return pl.pallas_call(
        functools.partial(_moba_kernel, n_blk=n_blk),
        out_shape=jax.ShapeDtypeStruct((b, s, N_HEADS_MOBA * HEAD_DIM), BF16),
        grid=(b, n_pairs, n_blk),
        in_specs=[pl.BlockSpec((1, blk, LANES), lambda bi, p, i: (bi, i, Q_BLK0 + p)),
                  pl.BlockSpec((1, s, LANES), lambda bi, p, i: (bi, 0, K_BLK0 + p)),
                  pl.BlockSpec((1, s, LANES), lambda bi, p, i: (bi, 0, V_BLK0 + p))],
        out_specs=pl.BlockSpec((1, blk, LANES), lambda bi, p, i: (bi, i, p)),
        scratch_shapes=[pltpu.VMEM((LANES, LANES), F32),
                        pltpu.VMEM((blk, LANES), BF16),
                        pltpu.VMEM((HEADS_PER_BLOCK, blk, 2 * LANES), BF16),
                        pltpu.VMEM((HEADS_PER_BLOCK, blk, LANES), F32),
                        pltpu.VMEM((2, HEADS_PER_BLOCK, blk, LANES), F32),
                        pltpu.VMEM((HEADS_PER_BLOCK, blk, LANES), F32),
                        pltpu.VMEM((2, HEADS_PER_BLOCK, blk, blk), F32),
                        pltpu.VMEM((2, HEADS_PER_BLOCK, blk, blk), BF16)],
        compiler_params=_params(3),
        name="moba_attn",
    )(qkv, qkv, qkv)


SB_TILE = 256


def _softplus(z):
    return jnp.maximum(z, 0.0) + jnp.log(1.0 + jnp.exp(-jnp.abs(z)))


_SB_SLOTS = 3
SB_DEAD_MASS = 128.0


def _sb_kernel(q_ref, k_ref, v_ref, u_ref, o_ref, qaug_ref, carry_ref, acc_ref,
               z_ref, sphl_ref, c_ref, rs_ref, a_ref):
    qi = pl.program_id(2)
    t = SB_TILE
    rc = _ROW_CHUNK
    lane = lax.broadcasted_iota(jnp.int32, (t, LANES), 1)
    lane1 = lax.broadcasted_iota(jnp.int32, (1, LANES), 1)
    q2 = q_ref[0]
    for hh in range(HEADS_PER_BLOCK):
        in_head = (lane >= hh * HEAD_DIM) & (lane < (hh + 1) * HEAD_DIM)
        qm = jnp.where(in_head, q2, jnp.zeros_like(q2)) * jnp.asarray(SCALE, BF16)
        qaug_ref[hh] = jnp.concatenate([qm, jnp.where(lane == 0, 1.0, 0.0).astype(BF16)], axis=1)

    def score_stage(j, slot, *, valid=True):
        kj = k_ref[0, pl.ds(pl.multiple_of(j * t, t), t), :]
        kill = jnp.where(lane1 == 0, jnp.where(valid, 0.0, NEG_INF), 0.0)
        k_aug = jnp.concatenate([kj, jnp.broadcast_to(kill, (t, LANES)).astype(BF16)], axis=1)
        for hh in range(HEADS_PER_BLOCK):
            z_ref[slot, hh] = _dot_nt(qaug_ref[hh], k_aug)

    def _past_mask(c):
        col_c = lax.broadcasted_iota(jnp.int32, (rc, t), 1)
        row_c = lax.broadcasted_iota(jnp.int32, (rc, t), 0) + c * rc
        return col_c < row_c

    def softplus_stage(slot, *, diagonal=False):
        for hh in range(HEADS_PER_BLOCK):
            for c in range(t // rc):
                rows = slice(c * rc, (c + 1) * rc)
                sp = _softplus(z_ref[slot, hh, rows, :])
                if diagonal:
                    sp = jnp.where(_past_mask(c), sp, 0.0)
                hi, lo = _split2(sp)
                sphl_ref[slot, pl.ds((2 * hh) * t + c * rc, rc), :] = hi
                sphl_ref[slot, pl.ds((2 * hh + 1) * t + c * rc, rc), :] = lo
                rs_ref[slot, hh, rows, :] = jnp.broadcast_to(jnp.sum(sp, axis=1, keepdims=True), (rc, LANES))

    def suffix_stage(slot):
        c_ref[slot] = _dot(sphl_ref[slot], u_ref[...])

    def weight_stage(slot, *, diagonal=False):
        for hh in range(HEADS_PER_BLOCK):
            for c in range(t // rc):
                rows = slice(c * rc, (c + 1) * rc)
                z = z_ref[slot, hh, rows, :]
                cc = c_ref[slot, pl.ds((2 * hh) * t + c * rc, rc), :] + \
                    c_ref[slot, pl.ds((2 * hh + 1) * t + c * rc, rc), :]
                if diagonal:
                    a = jnp.where(_past_mask(c), jnp.exp(z - cc), 0.0)
                    carry_ref[hh, rows, :] = rs_ref[slot, hh, rows, :]
                else:
                    carry = carry_ref[hh, rows, :]
                    a = jnp.exp(z - (cc + jnp.concatenate([carry, carry], axis=1)))
                    carry_ref[hh, rows, :] = carry + rs_ref[slot, hh, rows, :]
                a_ref[slot, hh, rows, :] = a.astype(BF16)

    def value_stage(j, slot, *, first=False):
        vj = v_ref[0, pl.ds(pl.multiple_of(j * t, t), t), :]
        for hh in range(HEADS_PER_BLOCK):
            av = _dot(a_ref[slot, hh], vj)
            acc_ref[hh] = av if first else acc_ref[hh] + av

    score_stage(qi, 0)
    softplus_stage(0, diagonal=True)
    suffix_stage(0)
    weight_stage(0, diagonal=True)
    value_stage(qi, 0, first=True)

    n_past = qi

    @pl.when(n_past > 0)
    def _():
        z_ref[1] = jnp.full(z_ref.shape[1:], NEG_INF, F32)
        z_ref[2] = jnp.full(z_ref.shape[1:], NEG_INF, F32)
        c_ref[1] = jnp.zeros(c_ref.shape[1:], F32)
        rs_ref[1] = jnp.zeros(rs_ref.shape[1:], F32)
        last = n_past - 1

        def key_tile(i):
            return jnp.clip(last - i, 0, last)

        def trip(state):
            tt, _ = state
            for k in range(_SB_SLOTS):
                i = _SB_SLOTS * tt + k
                score_stage(key_tile(i), k, valid=i < n_past)
                softplus_stage((k - 1) % _SB_SLOTS)
                suffix_stage((k - 1) % _SB_SLOTS)
                weight_stage((k - 2) % _SB_SLOTS)
                value_stage(key_tile(i - 2), (k - 2) % _SB_SLOTS)
            return tt + 1, jnp.min(carry_ref[...])

        n_trips = (n_past + 2 + _SB_SLOTS - 1) // _SB_SLOTS
        lax.while_loop(lambda st: (st[0] < n_trips) & (st[1] < SB_DEAD_MASS), trip,
                       (jnp.int32(0), jnp.min(carry_ref[...])))

    o_ref[0] = jnp.where(lane < HEAD_DIM, acc_ref[0], acc_ref[1]).astype(BF16)


def _sb_attention(qkv):
    b, s, _ = qkv.shape
    t = SB_TILE
    assert s % t == 0
    n_pairs = N_HEADS_SB // HEADS_PER_BLOCK
    u = (lax.broadcasted_iota(jnp.int32, (t, t), 0) >= lax.broadcasted_iota(jnp.int32, (t, t), 1)).astype(BF16)
    return pl.pallas_call(
        _sb_kernel,
        out_shape=jax.ShapeDtypeStruct((b, s, N_HEADS_SB * HEAD_DIM), BF16),
        grid=(b, n_pairs, s // t),
        in_specs=[pl.BlockSpec((1, t, LANES), lambda bi, p, i: (bi, i, Q_BLK0 + SB_BLK + p)),
                  pl.BlockSpec((1, s, LANES), lambda bi, p, i: (bi, 0, K_BLK0 + SB_BLK + p)),
                  pl.BlockSpec((1, s, LANES), lambda bi, p, i: (bi, 0, V_BLK0 + SB_BLK + p)),
                  pl.BlockSpec((t, t), lambda bi, p, i: (0, 0))],
        out_specs=pl.BlockSpec((1, t, LANES), lambda bi, p, i: (bi, i, p)),
        scratch_shapes=[pltpu.VMEM((HEADS_PER_BLOCK, t, 2 * LANES), BF16),
                        pltpu.VMEM((HEADS_PER_BLOCK, t, LANES), F32),
                        pltpu.VMEM((HEADS_PER_BLOCK, t, LANES), F32),
                        pltpu.VMEM((_SB_SLOTS, HEADS_PER_BLOCK, t, t), F32),
                        pltpu.VMEM((_SB_SLOTS, 2 * HEADS_PER_BLOCK * t, t), BF16),
                        pltpu.VMEM((_SB_SLOTS, 2 * HEADS_PER_BLOCK * t, t), F32),
                        pltpu.VMEM((_SB_SLOTS, HEADS_PER_BLOCK, t, LANES), F32),
                        pltpu.VMEM((_SB_SLOTS, HEADS_PER_BLOCK, t, t), BF16)],
        compiler_params=_params(3),
        name="sb_attn",
    )(qkv, qkv, qkv, u)


DIL_SPAN = 2048
_DIL_UNROLL = 4


def _dil_kernel(q_ref, k_ref, v_ref, kp_ref, vp_ref, o_ref, qf_ref, kf_ref, vf_ref, oc_ref, lse_ref, bias_ref):
    pair = pl.program_id(1)
    i = pl.program_id(2)
    n = DIL_N
    span = DIL_SPAN
    lane = lax.broadcasted_iota(jnp.int32, (n, LANES), 1)
    colh = lax.broadcasted_iota(jnp.int32, (n, 2 * n), 1)

    @pl.when(i == 0)
    def _():
        row = lax.broadcasted_iota(jnp.int32, (n, 2 * n), 0)
        delta = row + n - colh
        valid = (delta >= 0) & (delta <= n)
        for ci, (_, d) in enumerate(DIL_CONFIGS):
            for hh in range(HEADS_PER_BLOCK):
                slope = jnp.asarray(2.0 ** -(hh + 1), F32)
                for p in range(1, N_HEADS_DIL // HEADS_PER_BLOCK):
                    slope = jnp.where(pair == p, 2.0 ** -(2 * p + hh + 1), slope)
                bias_ref[ci, hh] = jnp.where(valid, -slope * (delta * d).astype(F32), NEG_INF)

    qf_ref[...] = q_ref[0].astype(F32)
    kf_ref[0:span, :] = kp_ref[0].astype(F32)
    kf_ref[span:2 * span, :] = k_ref[0].astype(F32)
    vf_ref[0:span, :] = vp_ref[0].astype(F32)
    vf_ref[span:2 * span, :] = v_ref[0].astype(F32)

    def rows(start, size, d):
        return pl.ds(start, size) if d == 1 else pl.ds(start, size, stride=d)

    def unit(ci, d, r, bl):
        q0 = r + bl * (n * d)
        qu = qf_ref[rows(q0, n, d), :].astype(BF16)
        ku = kf_ref[rows(span + q0 - n * d, 2 * n, d), :].astype(BF16)
        vu = vf_ref[rows(span + q0 - n * d, 2 * n, d), :].astype(BF16)
        no_prev = jnp.where((i == 0) & (bl == 0), NEG_INF, 0.0)
        o_h, lse_h = [], []
        for hh in range(HEADS_PER_BLOCK):
            in_head = (lane >= hh * HEAD_DIM) & (lane < (hh + 1) * HEAD_DIM)
            qm = jnp.where(in_head, qu, jnp.zeros_like(qu)) * jnp.asarray(SCALE, BF16)
            s = _dot_nt(qm, ku) + bias_ref[ci, hh] + jnp.where(colh < n, no_prev, 0.0)
            m = jnp.max(s, axis=1, keepdims=True)
            p = jnp.exp(s - m)
            den = jnp.sum(p, axis=1, keepdims=True)
            o_h.append(_dot(p.astype(BF16), vu) / den)
            lse_h.append(jnp.broadcast_to(m + jnp.log(den), (n, LANES)))
        oc_ref[ci, rows(q0, n, d), :] = jnp.where(lane < HEAD_DIM, o_h[0], o_h[1])
        lse_ref[ci, rows(q0, n, d), :] = jnp.where(lane < HEAD_DIM, lse_h[0], lse_h[1])

    n_units = span // n
    for ci, (_, d) in enumerate(DIL_CONFIGS):
        per_res = n_units // d

        def group(g, c, ci=ci, d=d, per_res=per_res):
            for k in range(_DIL_UNROLL):
                u = g * _DIL_UNROLL + k
                unit(ci, d, u // per_res, u % per_res)
            return c

        lax.fori_loop(0, n_units // _DIL_UNROLL, group, 0)

    mc = 256
    for c0 in range(0, span, mc):
        l1, l2, l3 = (lse_ref[ci, c0:c0 + mc, :] for ci in range(3))
        lmax = jnp.maximum(jnp.maximum(l1, l2), l3)
        e1, e2, e3 = jnp.exp(l1 - lmax), jnp.exp(l2 - lmax), jnp.exp(l3 - lmax)
        mix = (e1 * oc_ref[0, c0:c0 + mc, :] + e2 * oc_ref[1, c0:c0 + mc, :] + e3 * oc_ref[2, c0:c0 + mc, :]) \
            / (e1 + e2 + e3)
        o_ref[0, c0:c0 + mc, :] = mix.astype(BF16)


def _dilated_mixture(qkv):
    b, s, _ = qkv.shape
    span = DIL_SPAN
    assert s % span == 0 and all(w <= span and span % (DIL_N * d) == 0 for w, d in DIL_CONFIGS)
    n_pairs = N_HEADS_DIL // HEADS_PER_BLOCK
    n_cfg = len(DIL_CONFIGS)

    def cur(off):
        return lambda bi, p, i: (bi, i, off + DIL_BLK + p)

    def prev(off):
        return lambda bi, p, i: (bi, jnp.maximum(i - 1, 0), off + DIL_BLK + p)

    blk = (1, span, LANES)
    return pl.pallas_call(
        _dil_kernel,
        out_shape=jax.ShapeDtypeStruct((b, s, N_HEADS_DIL * HEAD_DIM), BF16),
        grid=(b, n_pairs, s // span),
        in_specs=[pl.BlockSpec(blk, cur(Q_BLK0)), pl.BlockSpec(blk, cur(K_BLK0)), pl.BlockSpec(blk, cur(V_BLK0)),
                  pl.BlockSpec(blk, prev(K_BLK0)), pl.BlockSpec(blk, prev(V_BLK0))],
        out_specs=pl.BlockSpec(blk, lambda bi, p, i: (bi, i, p)),
        scratch_shapes=[pltpu.VMEM((span, LANES), F32),
                        pltpu.VMEM((2 * span, LANES), F32),
                        pltpu.VMEM((2 * span, LANES), F32),
                        pltpu.VMEM((n_cfg, span, LANES), F32),
                        pltpu.VMEM((n_cfg, span, LANES), F32),
                        pltpu.VMEM((n_cfg, HEADS_PER_BLOCK, DIL_N, 2 * DIL_N), F32)],
        compiler_params=_params(3),
        name="dilated_attn",
    )(qkv, qkv, qkv, qkv, qkv)


def _rms(x, g):
    return x * lax.rsqrt(jnp.mean(x * x, axis=-1, keepdims=True) + NORM_EPS) * g


def _outproj_kernel(x_ref, oa_ref, ob_ref, oc_ref, g_ref, w_ref, out_ref):
    g = g_ref[...]
    wa = N_HEADS_MOBA * HEAD_DIM
    wb = wa + N_HEADS_SB * HEAD_DIM
    y = jnp.concatenate([_rms(oa_ref[...].astype(F32), g[:, :wa]),
                         _rms(ob_ref[...].astype(F32), g[:, wa:wb]),
                         _rms(oc_ref[...].astype(F32), g[:, wb:])], axis=1).astype(BF16)
    out_ref[...] = x_ref[...] + _dot(y, w_ref[...])


def _out_proj(x2, oa, ob, oc, g, w, *, tm=512):
    t, d = x2.shape
    row = lambda i: (i, 0)
    const = lambda i: (0, 0)
    return pl.pallas_call(
        _outproj_kernel,
        out_shape=jax.ShapeDtypeStruct((t, d), F32),
        grid=(t // tm,),
        in_specs=[pl.BlockSpec((tm, d), row),
                  pl.BlockSpec((tm, oa.shape[1]), row), pl.BlockSpec((tm, ob.shape[1]), row),
                  pl.BlockSpec((tm, oc.shape[1]), row),
                  pl.BlockSpec((1, d), const), pl.BlockSpec((d, d), const)],
        out_specs=pl.BlockSpec((tm, d), row),
        compiler_params=_params(1),
        name="out_proj",
    )(x2, oa, ob, oc, g.reshape(1, d), w)


_EXP_LANE0 = N_GROUPS


def _moe_kernel(x_ref, g_ref, wrh_ref, wrl_ref, br_ref, wg_ref, wu_ref, wd_ref, out_ref,
                h_ref, comb_ref, acc_ref):
    e = pl.program_id(1)
    tm = x_ref.shape[0]
    lane = lax.broadcasted_iota(jnp.int32, (tm, LANES), 1)

    @pl.when(e == 0)
    def _():
        x = x_ref[...]
        h = _rms(x, g_ref[...])
        h_ref[...] = h.astype(BF16)
        hh, hl = _split2(h)
        logits = _dot(hh, wrh_ref[...]) + _dot(hh, wrl_ref[...]) + _dot(hl, wrh_ref[...]) + br_ref[...]
        lane_f = lane.astype(F32)
        big = float(LANES)
        gl = jnp.where(lane < N_GROUPS, logits, -jnp.inf)
        gmax = jnp.max(gl, axis=1, keepdims=True)
        gidx = jnp.min(jnp.where(gl == gmax, lane_f, big), axis=1, keepdims=True)
        g_w = 1.0 / jnp.sum(jnp.exp(gl - gmax), axis=1, keepdims=True)
        lane_group = ((lane - _EXP_LANE0) // EXPERTS_PER_GROUP).astype(F32)
        in_group = (lane >= _EXP_LANE0) & (lane < _EXP_LANE0 + N_EXPERTS) & (lane_group == gidx)
        el = jnp.where(in_group, logits, -jnp.inf)
        v1 = jnp.max(el, axis=1, keepdims=True)
        i1 = jnp.min(jnp.where(el == v1, lane_f, big), axis=1, keepdims=True)
        el2 = jnp.where(lane_f == i1, -jnp.inf, el)
        v2 = jnp.max(el2, axis=1, keepdims=True)
        i2 = jnp.min(jnp.where(el2 == v2, lane_f, big), axis=1, keepdims=True)
        r = jnp.exp(v2 - v1)
        w1 = g_w / (1.0 + r)
        w2 = g_w * r / (1.0 + r)
        comb_ref[...] = jnp.where(lane_f == i1, w1, 0.0) + jnp.where(lane_f == i2, w2, 0.0)
        acc_ref[...] = jnp.zeros_like(acc_ref)

    h = h_ref[...]
    gate = _dot(h, wg_ref[0])
    up = _dot(h, wu_ref[0])
    cw = jnp.sum(jnp.where(lane == e + _EXP_LANE0, comb_ref[...], 0.0), axis=1, keepdims=True)
    act = gate / (1.0 + jnp.exp(-gate)) * up * cw
    acc_ref[...] += _dot(act.astype(BF16), wd_ref[0])

    @pl.when(e == N_EXPERTS - 1)
    def _():
        out_ref[...] = x_ref[...] + acc_ref[...]


def _moe(x2, g, wr_hi, wr_lo, br, wg, wu, wd, *, tm=1024):
    t, d = x2.shape
    f = wg.shape[2]
    row = lambda i, e: (i, 0)
    const = lambda i, e: (0, 0)
    return pl.pallas_call(
        _moe_kernel,
        out_shape=jax.ShapeDtypeStruct((t, d), F32),
        grid=(t // tm, N_EXPERTS),
        in_specs=[pl.BlockSpec((tm, d), row),
                  pl.BlockSpec((1, d), const),
                  pl.BlockSpec((d, LANES), const), pl.BlockSpec((d, LANES), const),
                  pl.BlockSpec((1, LANES), const),
                  pl.BlockSpec((1, d, f), lambda i, e: (e, 0, 0)),
                  pl.BlockSpec((1, d, f), lambda i, e: (e, 0, 0)),
                  pl.BlockSpec((1, f, d), lambda i, e: (e, 0, 0))],
        out_specs=pl.BlockSpec((tm, d), row),
        scratch_shapes=[pltpu.VMEM((tm, d), BF16), pltpu.VMEM((tm, LANES), F32), pltpu.VMEM((tm, d), F32)],
        compiler_params=_params(2),
        name="hier_moe",
    )(x2, g.reshape(1, d), wr_hi, wr_lo, br, wg, wu, wd)


def _final_norm_kernel(x_ref, g_ref, o_ref):
    o_ref[...] = _rms(x_ref[...], g_ref[...])


def _final_norm(x2, g, *, tm=1024):
    t, d = x2.shape
    return pl.pallas_call(
        _final_norm_kernel,
        out_shape=jax.ShapeDtypeStruct((t, d), F32),
        grid=(t // tm,),
        in_specs=[pl.BlockSpec((tm, d), lambda i: (i, 0)), pl.BlockSpec((1, d), lambda i: (0, 0))],
        out_specs=pl.BlockSpec((tm, d), lambda i: (i, 0)),
        compiler_params=_params(1),
        name="final_norm",
    )(x2, g.reshape(1, d))


def _router_weights(w_gr, b_gr, w_er, b_er):
    d = w_gr.shape[0]
    w = jnp.concatenate([w_gr, jnp.moveaxis(w_er, 0, 1).reshape(d, N_EXPERTS)], axis=1)
    w = jnp.pad(w, ((0, 0), (0, LANES - w.shape[1])))
    bias = jnp.pad(jnp.concatenate([b_gr, b_er.reshape(-1)]), (0, LANES - N_GROUPS - N_EXPERTS))
    hi, lo = _split2(w)
    return hi, lo, bias.reshape(1, LANES)


def _layer(x2, b, s, ln1_g, w_in, mix_g, w_out, ln2_g, w_gr, b_gr, w_er, b_er, w_gate, w_up, w_down):
    t, d = x2.shape
    qkv = _qkv_proj(x2, ln1_g, w_in.astype(BF16)).reshape(b, s, 3 * d)
    oa = _moba_attention(qkv)
    ob = _sb_attention(qkv)
    oc = _dilated_mixture(qkv)
    x2 = _out_proj(x2, oa.reshape(t, -1), ob.reshape(t, -1), oc.reshape(t, -1), mix_g, w_out.astype(BF16))
    wr_hi, wr_lo, br = _router_weights(w_gr, b_gr, w_er, b_er)
    f = w_gate.shape[-1]
    return _moe(x2, ln2_g, wr_hi, wr_lo, br,
                w_gate.reshape(N_EXPERTS, d, f).astype(BF16),
                w_up.reshape(N_EXPERTS, d, f).astype(BF16),
                w_down.reshape(N_EXPERTS, f, d).astype(BF16))


def kernel(x, ln1_g, w_in, mix_norm_g, w_out, ln2_g, w_group_router, b_group_router,
           w_expert_router, b_expert_router, w_gate, w_up, w_down, final_norm_g):
    b, s, d = x.shape
    x2 = x.reshape(b * s, d)
    for l in range(ln1_g.shape[0]):
        x2 = _layer(x2, b, s, ln1_g[l], w_in[l], mix_norm_g[l], w_out[l], ln2_g[l],
                    w_group_router[l], b_group_router[l], w_expert_router[l], b_expert_router[l],
                    w_gate[l], w_up[l], w_down[l])
    return _final_norm(x2, final_norm_g).reshape(b, s, d)
```

```python
import functools

import jax
import jax.numpy as jnp
from jax import lax
from jax.experimental import pallas as pl
from jax.experimental.pallas import tpu as pltpu

F32 = jnp.float32
BF16 = jnp.bfloat16

D_MODEL = 1024
HEAD_DIM = 64
N_HEADS = 16
LANES = 128
HEADS_PER_BLOCK = LANES // HEAD_DIM
N_HEADS_MOBA = 4
N_HEADS_SB = 4
N_HEADS_DIL = 8
MOBA_BLOCK = 256
MOBA_TOPK = 3
DIL_CONFIGS = ((128, 1), (512, 4), (2048, 16))
DIL_N = 128
N_GROUPS = 4
EXPERTS_PER_GROUP = 4
N_EXPERTS = N_GROUPS * EXPERTS_PER_GROUP
D_EXPERT = 256
NORM_EPS = 1e-6
NEG_INF = -1e30
SCALE = HEAD_DIM ** -0.5

Q_BLK0 = 0
K_BLK0 = D_MODEL // LANES
V_BLK0 = 2 * D_MODEL // LANES
ROW_BLKS = 3 * D_MODEL // LANES
SB_BLK = N_HEADS_MOBA // HEADS_PER_BLOCK
DIL_BLK = (N_HEADS_MOBA + N_HEADS_SB) // HEADS_PER_BLOCK

VMEM_LIMIT = 56 * 1024 * 1024


def _params(n_axes, vmem=VMEM_LIMIT):
    return pltpu.CompilerParams(dimension_semantics=("arbitrary",) * n_axes,
                                vmem_limit_bytes=vmem)


def _dot_nt(a, b):
    return lax.dot_general(a, b, (((1,), (1,)), ((), ())), preferred_element_type=F32)


def _dot(a, b):
    return jnp.dot(a, b, preferred_element_type=F32)


def _split3(x):
    hi = x.astype(BF16)
    r1 = x - hi.astype(F32)
    mid = r1.astype(BF16)
    lo = (r1 - mid.astype(F32)).astype(BF16)
    return hi, mid, lo


def _split2(x):
    hi = x.astype(BF16)
    lo = (x - hi.astype(F32)).astype(BF16)
    return hi, lo


def _qkv_kernel(x_ref, g_ref, w_ref, o_ref, *, rows, tn):
    for c in range(x_ref.shape[0] // rows):
        r = slice(c * rows, (c + 1) * rows)
        x = x_ref[r, :]
        ms = jnp.mean(x * x, axis=-1, keepdims=True)
        h = (x * lax.rsqrt(ms + NORM_EPS) * g_ref[...]).astype(BF16)
        for j in range(w_ref.shape[1] // tn):
            o_ref[r, j * tn:(j + 1) * tn] = _dot(h, w_ref[:, j * tn:(j + 1) * tn]).astype(BF16)


def _qkv_proj(x2, g, w, *, tm=512, rows=256, tn=1024):
    t, d = x2.shape
    n = w.shape[1]
    return pl.pallas_call(
        functools.partial(_qkv_kernel, rows=rows, tn=tn),
        out_shape=jax.ShapeDtypeStruct((t, n), BF16),
        grid=(t // tm,),
        in_specs=[pl.BlockSpec((tm, d), lambda i: (i, 0)),
                  pl.BlockSpec((1, d), lambda i: (0, 0)),
                  pl.BlockSpec((d, n), lambda i: (0, 0))],
        out_specs=pl.BlockSpec((tm, n), lambda i: (i, 0)),
        compiler_params=_params(1),
        name="qkv_proj",
    )(x2, g.reshape(1, d), w)


_MB_SEL0 = 0
_MB_POS0 = 32
_MB_KILL = 40
_MB_BLK0 = 64


_ROW_CHUNK = 32


def _moba_kernel(q_ref, k_ref, v_ref, o_ref, kmean_ref, kx_ref, qaug_ref, m_ref, alpha_ref, acc_ref,
                 s_ref, p_ref, *, n_blk):
    pair = pl.program_id(1)
    qi = pl.program_id(2)
    blk = MOBA_BLOCK
    lane = lax.broadcasted_iota(jnp.int32, (blk, LANES), 1)
    row = lax.broadcasted_iota(jnp.int32, (blk, LANES), 0)
    lane_f = lane.astype(F32)
    slopes = [jnp.where(pair == 0, 2.0 ** (-2 * (hh + 1)), 2.0 ** (-2 * (hh + 3))).astype(F32)
              for hh in range(HEADS_PER_BLOCK)]

    @pl.when(qi == 0)
    def _():
        kmean_ref[...] = jnp.zeros_like(kmean_ref)

        def body(n, c):
            kb = k_ref[0, pl.ds(pl.multiple_of(n * blk, blk), blk), :].astype(F32)
            kmean_ref[pl.ds(n, 1), :] = jnp.sum(kb, axis=0, keepdims=True) * (1.0 / blk)
            return c

        lax.fori_loop(0, n_blk, body, 0)
        kx = jnp.zeros((blk, LANES), F32)
        for hh in range(HEADS_PER_BLOCK):
            kx = jnp.where(lane == _MB_POS0 + 2 * hh, slopes[hh] * ((row // LANES) * LANES).astype(F32), kx)
            kx = jnp.where(lane == _MB_POS0 + 2 * hh + 1, slopes[hh] * (row % LANES).astype(F32), kx)
        kx_ref[...] = kx.astype(BF16)

    q2 = q_ref[0]
    km_parts = _split3(kmean_ref[...])
    for hh in range(HEADS_PER_BLOCK):
        in_head = (lane >= hh * HEAD_DIM) & (lane < (hh + 1) * HEAD_DIM)
        qm = jnp.where(in_head, q2, jnp.zeros_like(q2))
        gate = _dot_nt(qm, km_parts[0]) + _dot_nt(qm, km_parts[1]) + _dot_nt(qm, km_parts[2])
        gate = jnp.where(lane < qi, gate, NEG_INF)
        gate = jnp.where(lane < n_blk, gate, -jnp.inf)
        sel = jnp.zeros((blk, LANES), jnp.bool_)
        for _ in range(MOBA_TOPK):
            gmax = jnp.max(gate, axis=1, keepdims=True)
            first = jnp.min(jnp.where(gate == gmax, lane_f, float(LANES)), axis=1, keepdims=True)
            pick = lane_f == first
            sel = sel | pick
            gate = jnp.where(pick, -jnp.inf, gate)
        sel = sel & (lane < qi)
        extra = jnp.where(sel, 0.0, NEG_INF)
        extra = jnp.where(lane >= n_blk, 0.0, extra)
        extra = jnp.where((lane == _MB_POS0 + 2 * hh) | (lane == _MB_POS0 + 2 * hh + 1) | (lane == _MB_KILL),
                          1.0, extra)
        blk_lane = lane - (_MB_BLK0 + 32 * hh)
        extra = jnp.where((blk_lane >= 0) & (blk_lane < 32),
                          slopes[hh] * (blk_lane * blk).astype(F32), extra)
        qaug_ref[hh] = jnp.concatenate([qm * jnp.asarray(SCALE, BF16), extra.astype(BF16)], axis=1)
        m_ref[hh] = jnp.full((blk, LANES), -jnp.inf, F32)
        acc_ref[hh] = jnp.zeros((blk, LANES), F32)

    lane1 = lax.broadcasted_iota(jnp.int32, (1, LANES), 1)
    rc = _ROW_CHUNK

    def score_stage(j, buf, *, is_own=False, valid=True):
        kj = k_ref[0, pl.ds(pl.multiple_of(j * blk, blk), blk), :]
        ind = (lane1 % 32 == j) & (lane1 >= _MB_BLK0) if is_own else \
              (lane1 % 32 == j) & ((lane1 < 32) | (lane1 >= _MB_BLK0))
        kill = jnp.where(lane1 == _MB_KILL, jnp.where(valid, 0.0, NEG_INF), 0.0)
        side = jnp.broadcast_to(jnp.where(ind, 1.0, kill), (blk, LANES)).astype(BF16)
        kx = jnp.where(ind | (lane1 == _MB_KILL), side, kx_ref[...])
        k_aug = jnp.concatenate([kj, kx], axis=1)
        for hh in range(HEADS_PER_BLOCK):
            s_ref[buf, hh] = _dot_nt(qaug_ref[hh], k_aug)

    def softmax_stage(buf, *, is_own=False):
        for hh in range(HEADS_PER_BLOCK):
            for c in range(blk // rc):
                rows = slice(c * rc, (c + 1) * rc)
                s = s_ref[buf, hh, rows, :]
                if is_own:
                    col_c = lax.broadcasted_iota(jnp.int32, (rc, blk), 1)
                    row_c = lax.broadcasted_iota(jnp.int32, (rc, blk), 0) + c * rc
                    s = jnp.where(col_c <= row_c, s, NEG_INF)
                m_old = m_ref[hh, rows, :]
                m_new = jnp.maximum(m_old, jnp.max(s, axis=1, keepdims=True))
                alpha_ref[buf, hh, rows, :] = jnp.exp(m_old - m_new)
                m_ref[hh, rows, :] = m_new
                p = jnp.exp(s - jnp.concatenate([m_new, m_new], axis=1))
                p_ref[buf, hh, rows, :] = p.astype(BF16)

    def value_stage(j, buf):
        vj = v_ref[0, pl.ds(pl.multiple_of(j * blk, blk), blk), :]
        for hh in range(HEADS_PER_BLOCK):
            in_head1 = (lane1 >= hh * HEAD_DIM) & (lane1 < (hh + 1) * HEAD_DIM)
            v_aug = jnp.where(in_head1, vj, jnp.ones_like(vj))
            acc_ref[hh] = alpha_ref[buf, hh] * acc_ref[hh] + _dot(p_ref[buf, hh], v_aug)

    score_stage(qi, 0, is_own=True)
    softmax_stage(0, is_own=True)
    value_stage(qi, 0)

    n_past = qi

    @pl.when(n_past > 0)
    def _():
        s_ref[1] = jnp.full(s_ref.shape[1:], NEG_INF, F32)
        p_ref[...] = jnp.zeros_like(p_ref)
        alpha_ref[...] = jnp.ones_like(alpha_ref)
        last = n_past - 1

        def pair(tt, c):
            t = 2 * tt
            score_stage(jnp.minimum(t, last), 0, valid=t < n_past)
            softmax_stage(1)
            value_stage(jnp.clip(t - 2, 0, last), 0)
            score_stage(jnp.minimum(t + 1, last), 1, valid=t + 1 < n_past)
            softmax_stage(0)
            value_stage(jnp.clip(t - 1, 0, last), 1)
            return c

        lax.fori_loop(0, (n_past + 3) // 2, pair, 0)

    acc0 = acc_ref[0]
    acc1 = acc_ref[1]
    o0 = acc0 / pltpu.roll(acc0, HEAD_DIM, axis=1)
    o1 = acc1 / pltpu.roll(acc1, HEAD_DIM, axis=1)
    o_ref[0] = jnp.where(lane < HEAD_DIM, o0, o1).astype(BF16)


def _moba_attention(qkv):
    b, s, _ = qkv.shape
    blk = MOBA_BLOCK
    n_blk = s // blk
    assert s % blk == 0 and MOBA_TOPK <= n_blk - 1 and n_blk <= 32
    n_pairs = N_HEADS_MOBA // HEADS_PER_BLOCK
    return pl.pallas_call(
        functools.partial(_moba_kernel, n_blk=n_blk),
        out_shape=jax.ShapeDtypeStruct((b, s, N_HEADS_MOBA * HEAD_DIM), BF16),
        grid=(b, n_pairs, n_blk),
        in_specs=[pl.BlockSpec((1, blk, LANES), lambda bi, p, i: (bi, i, Q_BLK0 + p)),
                  pl.BlockSpec((1, s, LANES), lambda bi, p, i: (bi, 0, K_BLK0 + p)),
                  pl.BlockSpec((1, s, LANES), lambda bi, p, i: (bi, 0, V_BLK0 + p))],
        out_specs=pl.BlockSpec((1, blk, LANES), lambda bi, p, i: (bi, i, p)),
        scratch_shapes=[pltpu.VMEM((LANES, LANES), F32),
                        pltpu.VMEM((blk, LANES), BF16),
                        pltpu.VMEM((HEADS_PER_BLOCK, blk, 2 * LANES), BF16),
                        pltpu.VMEM((HEADS_PER_BLOCK, blk, LANES), F32),
                        pltpu.VMEM((2, HEADS_PER_BLOCK, blk, LANES), F32),
                        pltpu.VMEM((HEADS_PER_BLOCK, blk, LANES), F32),
                        pltpu.VMEM((2, HEADS_PER_BLOCK, blk, blk), F32),
                        pltpu.VMEM((2, HEADS_PER_BLOCK, blk, blk), BF16)],
        compiler_params=_params(3),
        name="moba_attn",
    )(qkv, qkv, qkv)


SB_TILE = 256


def _softplus(z):
    return jnp.maximum(z, 0.0) + jnp.log(1.0 + jnp.exp(-jnp.abs(z)))


_SB_SLOTS = 3
SB_DEAD_MASS = 128.0


def _sb_kernel(q_ref, k_ref, v_ref, u_ref, o_ref, qaug_ref, carry_ref, acc_ref,
               z_ref, sphl_ref, c_ref, rs_ref, a_ref):
    qi = pl.program_id(2)
    t = SB_TILE
    rc = _ROW_CHUNK
    lane = lax.broadcasted_iota(jnp.int32, (t, LANES), 1)
    lane1 = lax.broadcasted_iota(jnp.int32, (1, LANES), 1)
    q2 = q_ref[0]
    for hh in range(HEADS_PER_BLOCK):
        in_head = (lane >= hh * HEAD_DIM) & (lane < (hh + 1) * HEAD_DIM)
        qm = jnp.where(in_head, q2, jnp.zeros_like(q2)) * jnp.asarray(SCALE, BF16)
        qaug_ref[hh] = jnp.concatenate([qm, jnp.where(lane == 0, 1.0, 0.0).astype(BF16)], axis=1)

    def score_stage(j, slot, *, valid=True):
        kj = k_ref[0, pl.ds(pl.multiple_of(j * t, t), t), :]
        kill = jnp.where(lane1 == 0, jnp.where(valid, 0.0, NEG_INF), 0.0)
        k_aug = jnp.concatenate([kj, jnp.broadcast_to(kill, (t, LANES)).astype(BF16)], axis=1)
        for hh in range(HEADS_PER_BLOCK):
            z_ref[slot, hh] = _dot_nt(qaug_ref[hh], k_aug)

    def _past_mask(c):
        col_c = lax.broadcasted_iota(jnp.int32, (rc, t), 1)
        row_c = lax.broadcasted_iota(jnp.int32, (rc, t), 0) + c * rc
        return col_c < row_c

    def softplus_stage(slot, *, diagonal=False):
        for hh in range(HEADS_PER_BLOCK):
            for c in range(t // rc):
                rows = slice(c * rc, (c + 1) * rc)
                sp = _softplus(z_ref[slot, hh, rows, :])
                if diagonal:
                    sp = jnp.where(_past_mask(c), sp, 0.0)
                hi, lo = _split2(sp)
                sphl_ref[slot, pl.ds((2 * hh) * t + c * rc, rc), :] = hi
                sphl_ref[slot, pl.ds((2 * hh + 1) * t + c * rc, rc), :] = lo
                rs_ref[slot, hh, rows, :] = jnp.broadcast_to(jnp.sum(sp, axis=1, keepdims=True), (rc, LANES))

    def suffix_stage(slot):
        c_ref[slot] = _dot(sphl_ref[slot], u_ref[...])

    def weight_stage(slot, *, diagonal=False):
        for hh in range(HEADS_PER_BLOCK):
            for c in range(t // rc):
                rows = slice(c * rc, (c + 1) * rc)
                z = z_ref[slot, hh, rows, :]
                cc = c_ref[slot, pl.ds((2 * hh) * t + c * rc, rc), :] + \
                    c_ref[slot, pl.ds((2 * hh + 1) * t + c * rc, rc), :]
                if diagonal:
                    a = jnp.where(_past_mask(c), jnp.exp(z - cc), 0.0)
                    carry_ref[hh, rows, :] = rs_ref[slot, hh, rows, :]
                else:
                    carry = carry_ref[hh, rows, :]
                    a = jnp.exp(z - (cc + jnp.concatenate([carry, carry], axis=1)))
                    carry_ref[hh, rows, :] = carry + rs_ref[slot, hh, rows, :]
                a_ref[slot, hh, rows, :] = a.astype(BF16)

    def value_stage(j, slot, *, first=False):
        vj = v_ref[0, pl.ds(pl.multiple_of(j * t, t), t), :]
        for hh in range(HEADS_PER_BLOCK):
            av = _dot(a_ref[slot, hh], vj)
            acc_ref[hh] = av if first else acc_ref[hh] + av

    score_stage(qi, 0)
    softplus_stage(0, diagonal=True)
    suffix_stage(0)
    weight_stage(0, diagonal=True)
    value_stage(qi, 0, first=True)

    n_past = qi

    @pl.when(n_past > 0)
    def _():
        z_ref[1] = jnp.full(z_ref.shape[1:], NEG_INF, F32)
        z_ref[2] = jnp.full(z_ref.shape[1:], NEG_INF, F32)
        c_ref[1] = jnp.zeros(c_ref.shape[1:], F32)
        rs_ref[1] = jnp.zeros(rs_ref.shape[1:], F32)
        last = n_past - 1

        def key_tile(i):
            return jnp.clip(last - i, 0, last)

        def trip(state):
            tt, _ = state
            for k in range(_SB_SLOTS):
                i = _SB_SLOTS * tt + k
                score_stage(key_tile(i), k, valid=i < n_past)
                softplus_stage((k - 1) % _SB_SLOTS)
                suffix_stage((k - 1) % _SB_SLOTS)
                weight_stage((k - 2) % _SB_SLOTS)
                value_stage(key_tile(i - 2), (k - 2) % _SB_SLOTS)
            return tt + 1, jnp.min(carry_ref[...])

        n_trips = (n_past + 2 + _SB_SLOTS - 1) // _SB_SLOTS
        lax.while_loop(lambda st: (st[0] < n_trips) & (st[1] < SB_DEAD_MASS), trip,
                       (jnp.int32(0), jnp.min(carry_ref[...])))

    o_ref[0] = jnp.where(lane < HEAD_DIM, acc_ref[0], acc_ref[1]).astype(BF16)


def _sb_attention(qkv):
    b, s, _ = qkv.shape
    t = SB_TILE
    assert s % t == 0
    n_pairs = N_HEADS_SB // HEADS_PER_BLOCK
    u = (lax.broadcasted_iota(jnp.int32, (t, t), 0) >= lax.broadcasted_iota(jnp.int32, (t, t), 1)).astype(BF16)
    return pl.pallas_call(
        _sb_kernel,
        out_shape=jax.ShapeDtypeStruct((b, s, N_HEADS_SB * HEAD_DIM), BF16),
        grid=(b, n_pairs, s // t),
        in_specs=[pl.BlockSpec((1, t, LANES), lambda bi, p, i: (bi, i, Q_BLK0 + SB_BLK + p)),
                  pl.BlockSpec((1, s, LANES), lambda bi, p, i: (bi, 0, K_BLK0 + SB_BLK + p)),
                  pl.BlockSpec((1, s, LANES), lambda bi, p, i: (bi, 0, V_BLK0 + SB_BLK + p)),
                  pl.BlockSpec((t, t), lambda bi, p, i: (0, 0))],
        out_specs=pl.BlockSpec((1, t, LANES), lambda bi, p, i: (bi, i, p)),
        scratch_shapes=[pltpu.VMEM((HEADS_PER_BLOCK, t, 2 * LANES), BF16),
                        pltpu.VMEM((HEADS_PER_BLOCK, t, LANES), F32),
                        pltpu.VMEM((HEADS_PER_BLOCK, t, LANES), F32),
                        pltpu.VMEM((_SB_SLOTS, HEADS_PER_BLOCK, t, t), F32),
                        pltpu.VMEM((_SB_SLOTS, 2 * HEADS_PER_BLOCK * t, t), BF16),
                        pltpu.VMEM((_SB_SLOTS, 2 * HEADS_PER_BLOCK * t, t), F32),
                        pltpu.VMEM((_SB_SLOTS, HEADS_PER_BLOCK, t, LANES), F32),
                        pltpu.VMEM((_SB_SLOTS, HEADS_PER_BLOCK, t, t), BF16)],
        compiler_params=_params(3),
        name="sb_attn",
    )(qkv, qkv, qkv, u)


DIL_SPAN = 2048
_DIL_UNROLL = 4


def _dil_kernel(q_ref, k_ref, v_ref, kp_ref, vp_ref, o_ref, qf_ref, kf_ref, vf_ref, oc_ref, lse_ref, bias_ref):
    pair = pl.program_id(1)
    i = pl.program_id(2)
    n = DIL_N
    span = DIL_SPAN
    lane = lax.broadcasted_iota(jnp.int32, (n, LANES), 1)
    colh = lax.broadcasted_iota(jnp.int32, (n, 2 * n), 1)

    @pl.when(i == 0)
    def _():
        row = lax.broadcasted_iota(jnp.int32, (n, 2 * n), 0)
        delta = row + n - colh
        valid = (delta >= 0) & (delta <= n)
        for ci, (_, d) in enumerate(DIL_CONFIGS):
            for hh in range(HEADS_PER_BLOCK):
                slope = jnp.asarray(2.0 ** -(hh + 1), F32)
                for p in range(1, N_HEADS_DIL // HEADS_PER_BLOCK):
                    slope = jnp.where(pair == p, 2.0 ** -(2 * p + hh + 1), slope)
                bias_ref[ci, hh] = jnp.where(valid, -slope * (delta * d).astype(F32), NEG_INF)

    qf_ref[...] = q_ref[0].astype(F32)
    kf_ref[0:span, :] = kp_ref[0].astype(F32)
    kf_ref[span:2 * span, :] = k_ref[0].astype(F32)
    vf_ref[0:span, :] = vp_ref[0].astype(F32)
    vf_ref[span:2 * span, :] = v_ref[0].astype(F32)

    def rows(start, size, d):
        return pl.ds(start, size) if d == 1 else pl.ds(start, size, stride=d)

    def unit(ci, d, r, bl):
        q0 = r + bl * (n * d)
        qu = qf_ref[rows(q0, n, d), :].astype(BF16)
        ku = kf_ref[rows(span + q0 - n * d, 2 * n, d), :].astype(BF16)
        vu = vf_ref[rows(span + q0 - n * d, 2 * n, d), :].astype(BF16)
        no_prev = jnp.where((i == 0) & (bl == 0), NEG_INF, 0.0)
        o_h, lse_h = [], []
        for hh in range(HEADS_PER_BLOCK):
            in_head = (lane >= hh * HEAD_DIM) & (lane < (hh + 1) * HEAD_DIM)
            qm = jnp.where(in_head, qu, jnp.zeros_like(qu)) * jnp.asarray(SCALE, BF16)
            s = _dot_nt(qm, ku) + bias_ref[ci, hh] + jnp.where(colh < n, no_prev, 0.0)
            m = jnp.max(s, axis=1, keepdims=True)
            p = jnp.exp(s - m)
            den = jnp.sum(p, axis=1, keepdims=True)
            o_h.append(_dot(p.astype(BF16), vu) / den)
            lse_h.append(jnp.broadcast_to(m + jnp.log(den), (n, LANES)))
        oc_ref[ci, rows(q0, n, d), :] = jnp.where(lane < HEAD_DIM, o_h[0], o_h[1])
        lse_ref[ci, rows(q0, n, d), :] = jnp.where(lane < HEAD_DIM, lse_h[0], lse_h[1])

    n_units = span // n
    for ci, (_, d) in enumerate(DIL_CONFIGS):
        per_res = n_units // d

        def group(g, c, ci=ci, d=d, per_res=per_res):
            for k in range(_DIL_UNROLL):
                u = g * _DIL_UNROLL + k
                unit(ci, d, u // per_res, u % per_res)
            return c

        lax.fori_loop(0, n_units // _DIL_UNROLL, group, 0)

    mc = 256
    for c0 in range(0, span, mc):
        l1, l2, l3 = (lse_ref[ci, c0:c0 + mc, :] for ci in range(3))
        lmax = jnp.maximum(jnp.maximum(l1, l2), l3)
        e1, e2, e3 = jnp.exp(l1 - lmax), jnp.exp(l2 - lmax), jnp.exp(l3 - lmax)
        mix = (e1 * oc_ref[0, c0:c0 + mc, :] + e2 * oc_ref[1, c0:c0 + mc, :] + e3 * oc_ref[2, c0:c0 + mc, :]) \
            / (e1 + e2 + e3)
        o_ref[0, c0:c0 + mc, :] = mix.astype(BF16)


def _dilated_mixture(qkv):
    b, s, _ = qkv.shape
    span = DIL_SPAN
    assert s % span == 0 and all(w <= span and span % (DIL_N * d) == 0 for w, d in DIL_CONFIGS)
    n_pairs = N_HEADS_DIL // HEADS_PER_BLOCK
    n_cfg = len(DIL_CONFIGS)

    def cur(off):
        return lambda bi, p, i: (bi, i, off + DIL_BLK + p)

    def prev(off):
        return lambda bi, p, i: (bi, jnp.maximum(i - 1, 0), off + DIL_BLK + p)

    blk = (1, span, LANES)
    return pl.pallas_call(
        _dil_kernel,
        out_shape=jax.ShapeDtypeStruct((b, s, N_HEADS_DIL * HEAD_DIM), BF16),
        grid=(b, n_pairs, s // span),
        in_specs=[pl.BlockSpec(blk, cur(Q_BLK0)), pl.BlockSpec(blk, cur(K_BLK0)), pl.BlockSpec(blk, cur(V_BLK0)),
                  pl.BlockSpec(blk, prev(K_BLK0)), pl.BlockSpec(blk, prev(V_BLK0))],
        out_specs=pl.BlockSpec(blk, lambda bi, p, i: (bi, i, p)),
        scratch_shapes=[pltpu.VMEM((span, LANES), F32),
                        pltpu.VMEM((2 * span, LANES), F32),
                        pltpu.VMEM((2 * span, LANES), F32),
                        pltpu.VMEM((n_cfg, span, LANES), F32),
                        pltpu.VMEM((n_cfg, span, LANES), F32),
                        pltpu.VMEM((n_cfg, HEADS_PER_BLOCK, DIL_N, 2 * DIL_N), F32)],
        compiler_params=_params(3),
        name="dilated_attn",
    )(qkv, qkv, qkv, qkv, qkv)


def _rms(x, g):
    return x * lax.rsqrt(jnp.mean(x * x, axis=-1, keepdims=True) + NORM_EPS) * g


def _outproj_kernel(x_ref, oa_ref, ob_ref, oc_ref, g_ref, w_ref, out_ref):
    g = g_ref[...]
    wa = N_HEADS_MOBA * HEAD_DIM
    wb = wa + N_HEADS_SB * HEAD_DIM
    y = jnp.concatenate([_rms(oa_ref[...].astype(F32), g[:, :wa]),
                         _rms(ob_ref[...].astype(F32), g[:, wa:wb]),
                         _rms(oc_ref[...].astype(F32), g[:, wb:])], axis=1).astype(BF16)
    out_ref[...] = x_ref[...] + _dot(y, w_ref[...])


def _out_proj(x2, oa, ob, oc, g, w, *, tm=512):
    t, d = x2.shape
    row = lambda i: (i, 0)
    const = lambda i: (0, 0)
    return pl.pallas_call(
        _outproj_kernel,
        out_shape=jax.ShapeDtypeStruct((t, d), F32),
        grid=(t // tm,),
        in_specs=[pl.BlockSpec((tm, d), row),
                  pl.BlockSpec((tm, oa.shape[1]), row), pl.BlockSpec((tm, ob.shape[1]), row),
                  pl.BlockSpec((tm, oc.shape[1]), row),
                  pl.BlockSpec((1, d), const), pl.BlockSpec((d, d), const)],
        out_specs=pl.BlockSpec((tm, d), row),
        compiler_params=_params(1),
        name="out_proj",
    )(x2, oa, ob, oc, g.reshape(1, d), w)


_EXP_LANE0 = N_GROUPS
_MOE_ROWS = 256


def _moe_kernel(x_ref, g_ref, wrh_ref, wrl_ref, br_ref, wg_ref, wu_ref, wd_ref, out_ref,
                h_ref, comb_ref, acc_ref):
    e = pl.program_id(1)
    tm = x_ref.shape[0]
    lane = lax.broadcasted_iota(jnp.int32, (tm, LANES), 1)

    @pl.when(e == 0)
    def _():
        x = x_ref[...]
        h = _rms(x, g_ref[...])
        h_ref[...] = h.astype(BF16)
        hh, hl = _split2(h)
        logits = _dot(hh, wrh_ref[...]) + _dot(hh, wrl_ref[...]) + _dot(hl, wrh_ref[...]) + br_ref[...]
        lane_f = lane.astype(F32)
        big = float(LANES)
        gl = jnp.where(lane < N_GROUPS, logits, -jnp.inf)
        gmax = jnp.max(gl, axis=1, keepdims=True)
        gidx = jnp.min(jnp.where(gl == gmax, lane_f, big), axis=1, keepdims=True)
        g_w = 1.0 / jnp.sum(jnp.exp(gl - gmax), axis=1, keepdims=True)
        lane_group = ((lane - _EXP_LANE0) // EXPERTS_PER_GROUP).astype(F32)
        in_group = (lane >= _EXP_LANE0) & (lane < _EXP_LANE0 + N_EXPERTS) & (lane_group == gidx)
        el = jnp.where(in_group, logits, -jnp.inf)
        v1 = jnp.max(el, axis=1, keepdims=True)
        i1 = jnp.min(jnp.where(el == v1, lane_f, big), axis=1, keepdims=True)
        el2 = jnp.where(lane_f == i1, -jnp.inf, el)
        v2 = jnp.max(el2, axis=1, keepdims=True)
        i2 = jnp.min(jnp.where(el2 == v2, lane_f, big), axis=1, keepdims=True)
        r = jnp.exp(v2 - v1)
        w1 = g_w / (1.0 + r)
        w2 = g_w * r / (1.0 + r)
        comb_ref[...] = jnp.where(lane_f == i1, w1, 0.0) + jnp.where(lane_f == i2, w2, 0.0)
        acc_ref[...] = jnp.zeros_like(acc_ref)

    lane_c = lax.broadcasted_iota(jnp.int32, (_MOE_ROWS, LANES), 1)
    for c in range(tm // _MOE_ROWS):
        rows = slice(c * _MOE_ROWS, (c + 1) * _MOE_ROWS)
        h = h_ref[rows, :]
        comb = comb_ref[rows, :]
        acts = []
        for k in range(EXPERTS_PER_GROUP):
            gate = _dot(h, wg_ref[0, k])
            up = _dot(h, wu_ref[0, k])
            cw = jnp.sum(jnp.where(lane_c == _EXP_LANE0 + e * EXPERTS_PER_GROUP + k, comb, 0.0),
                         axis=1, keepdims=True)
            acts.append((gate / (1.0 + jnp.exp(-gate)) * up * cw).astype(BF16))
        acc_ref[rows, :] += _dot(jnp.concatenate(acts, axis=1), wd_ref[0])

    @pl.when(e == N_GROUPS - 1)
    def _():
        out_ref[...] = x_ref[...] + acc_ref[...]


def _moe(x2, g, wr_hi, wr_lo, br, wg, wu, wd, *, tm=1024):
    t, d = x2.shape
    f = wg.shape[3]
    row = lambda i, e: (i, 0)
    const = lambda i, e: (0, 0)
    return pl.pallas_call(
        _moe_kernel,
        out_shape=jax.ShapeDtypeStruct((t, d), F32),
        grid=(t // tm, N_GROUPS),
        in_specs=[pl.BlockSpec((tm, d), row),
                  pl.BlockSpec((1, d), const),
                  pl.BlockSpec((d, LANES), const), pl.BlockSpec((d, LANES), const),
                  pl.BlockSpec((1, LANES), const),
                  pl.BlockSpec((1, EXPERTS_PER_GROUP, d, f), lambda i, e: (e, 0, 0, 0)),
                  pl.BlockSpec((1, EXPERTS_PER_GROUP, d, f), lambda i, e: (e, 0, 0, 0)),
                  pl.BlockSpec((1, EXPERTS_PER_GROUP * f, d), lambda i, e: (e, 0, 0))],
        out_specs=pl.BlockSpec((tm, d), row),
        scratch_shapes=[pltpu.VMEM((tm, d), BF16), pltpu.VMEM((tm, LANES), F32), pltpu.VMEM((tm, d), F32)],
        compiler_params=_params(2),
        name="hier_moe",
    )(x2, g.reshape(1, d), wr_hi, wr_lo, br, wg, wu, wd)


def _final_norm_kernel(x_ref, g_ref, o_ref):
    o_ref[...] = _rms(x_ref[...], g_ref[...])


def _final_norm(x2, g, *, tm=1024):
    t, d = x2.shape
    return pl.pallas_call(
        _final_norm_kernel,
        out_shape=jax.ShapeDtypeStruct((t, d), F32),
        grid=(t // tm,),
        in_specs=[pl.BlockSpec((tm, d), lambda i: (i, 0)), pl.BlockSpec((1, d), lambda i: (0, 0))],
        out_specs=pl.BlockSpec((tm, d), lambda i: (i, 0)),
        compiler_params=_params(1),
        name="final_norm",
    )(x2, g.reshape(1, d))


def _router_weights(w_gr, b_gr, w_er, b_er):
    d = w_gr.shape[0]
    w = jnp.concatenate([w_gr, jnp.moveaxis(w_er, 0, 1).reshape(d, N_EXPERTS)], axis=1)
    w = jnp.pad(w, ((0, 0), (0, LANES - w.shape[1])))
    bias = jnp.pad(jnp.concatenate([b_gr, b_er.reshape(-1)]), (0, LANES - N_GROUPS - N_EXPERTS))
    hi, lo = _split2(w)
    return hi, lo, bias.reshape(1, LANES)


def _layer(x2, b, s, ln1_g, w_in, mix_g, w_out, ln2_g, w_gr, b_gr, w_er, b_er, w_gate, w_up, w_down):
    t, d = x2.shape
    qkv = _qkv_proj(x2, ln1_g, w_in.astype(BF16)).reshape(b, s, 3 * d)
    oa = _moba_attention(qkv)
    ob = _sb_attention(qkv)
    oc = _dilated_mixture(qkv)
    x2 = _out_proj(x2, oa.reshape(t, -1), ob.reshape(t, -1), oc.reshape(t, -1), mix_g, w_out.astype(BF16))
    wr_hi, wr_lo, br = _router_weights(w_gr, b_gr, w_er, b_er)
    f = w_gate.shape[-1]
    return _moe(x2, ln2_g, wr_hi, wr_lo, br, w_gate.astype(BF16), w_up.astype(BF16),
                w_down.reshape(N_GROUPS, EXPERTS_PER_GROUP * f, d).astype(BF16))


def kernel(x, ln1_g, w_in, mix_norm_g, w_out, ln2_g, w_group_router, b_group_router,
           w_expert_router, b_expert_router, w_gate, w_up, w_down, final_norm_g):
    b, s, d = x.shape
    x2 = x.reshape(b * s, d)
    for l in range(ln1_g.shape[0]):
        x2 = _layer(x2, b, s, ln1_g[l], w_in[l], mix_norm_g[l], w_out[l], ln2_g[l],
                    w_group_router[l], b_group_router[l], w_expert_router[l], b_expert_router[l],
                    w_gate[l], w_up[l], w_down[l])
    return _final_norm(x2, final_norm_g).reshape(b, s, d)
```

```python
import functools

import jax
import jax.numpy as jnp
from jax import lax
from jax.experimental import pallas as pl
from jax.experimental.pallas import tpu as pltpu

F32 = jnp.float32
BF16 = jnp.bfloat16

D_MODEL = 1024
HEAD_DIM = 64
N_HEADS = 16
LANES = 128
HEADS_PER_BLOCK = LANES // HEAD_DIM
N_HEADS_MOBA = 4
N_HEADS_SB = 4
N_HEADS_DIL = 8
MOBA_BLOCK = 256
MOBA_TOPK = 3
DIL_CONFIGS = ((128, 1), (512, 4), (2048, 16))
DIL_N = 128
N_GROUPS = 4
EXPERTS_PER_GROUP = 4
N_EXPERTS = N_GROUPS * EXPERTS_PER_GROUP
D_EXPERT = 256
NORM_EPS = 1e-6
NEG_INF = -1e30
SCALE = HEAD_DIM ** -0.5

Q_BLK0 = 0
K_BLK0 = D_MODEL // LANES
V_BLK0 = 2 * D_MODEL // LANES
ROW_BLKS = 3 * D_MODEL // LANES
SB_BLK = N_HEADS_MOBA // HEADS_PER_BLOCK
DIL_BLK = (N_HEADS_MOBA + N_HEADS_SB) // HEADS_PER_BLOCK

VMEM_LIMIT = 56 * 1024 * 1024


def _params(n_axes, vmem=VMEM_LIMIT):
    return pltpu.CompilerParams(dimension_semantics=("arbitrary",) * n_axes,
                                vmem_limit_bytes=vmem)


def _dot_nt(a, b):
    return lax.dot_general(a, b, (((1,), (1,)), ((), ())), preferred_element_type=F32)


def _dot(a, b):
    return jnp.dot(a, b, preferred_element_type=F32)


def _split3(x):
    hi = x.astype(BF16)
    r1 = x - hi.astype(F32)
    mid = r1.astype(BF16)
    lo = (r1 - mid.astype(F32)).astype(BF16)
    return hi, mid, lo


def _split2(x):
    hi = x.astype(BF16)
    lo = (x - hi.astype(F32)).astype(BF16)
    return hi, lo


def _qkv_kernel(x_ref, g_ref, w_ref, o_ref, *, rows, tn):
    for c in range(x_ref.shape[0] // rows):
        r = slice(c * rows, (c + 1) * rows)
        x = x_ref[r, :]
        ms = jnp.mean(x * x, axis=-1, keepdims=True)
        h = (x * lax.rsqrt(ms + NORM_EPS) * g_ref[...]).astype(BF16)
        for j in range(w_ref.shape[1] // tn):
            o_ref[r, j * tn:(j + 1) * tn] = _dot(h, w_ref[:, j * tn:(j + 1) * tn]).astype(BF16)


def _qkv_proj(x2, g, w, *, tm=512, rows=256, tn=1024):
    t, d = x2.shape
    n = w.shape[1]
    return pl.pallas_call(
        functools.partial(_qkv_kernel, rows=rows, tn=tn),
        out_shape=jax.ShapeDtypeStruct((t, n), BF16),
        grid=(t // tm,),
        in_specs=[pl.BlockSpec((tm, d), lambda i: (i, 0)),
                  pl.BlockSpec((1, d), lambda i: (0, 0)),
                  pl.BlockSpec((d, n), lambda i: (0, 0))],
        out_specs=pl.BlockSpec((tm, n), lambda i: (i, 0)),
        compiler_params=_params(1),
        name="qkv_proj",
    )(x2, g.reshape(1, d), w)


_MB_SEL0 = 0
_MB_POS0 = 32
_MB_KILL = 40
_MB_BLK0 = 64


_ROW_CHUNK = 32


MOBA_DEAD_GAP = 110.0


def _moba_kernel(q_ref, k_ref, v_ref, o_ref, kmean_ref, kn2_ref, kx_ref, qaug_ref, m_ref, alpha_ref, acc_ref,
                 s_ref, p_ref, *, n_blk):
    pair = pl.program_id(1)
    qi = pl.program_id(2)
    blk = MOBA_BLOCK
    lane = lax.broadcasted_iota(jnp.int32, (blk, LANES), 1)
    row = lax.broadcasted_iota(jnp.int32, (blk, LANES), 0)
    lane_f = lane.astype(F32)
    slopes = [jnp.where(pair == 0, 2.0 ** (-2 * (hh + 1)), 2.0 ** (-2 * (hh + 3))).astype(F32)
              for hh in range(HEADS_PER_BLOCK)]

    @pl.when(qi == 0)
    def _():
        kmean_ref[...] = jnp.zeros_like(kmean_ref)
        kn2_ref[...] = jnp.zeros_like(kn2_ref)

        def body(n, c):
            kb = k_ref[0, pl.ds(pl.multiple_of(n * blk, blk), blk), :].astype(F32)
            kmean_ref[pl.ds(n, 1), :] = jnp.sum(kb, axis=0, keepdims=True) * (1.0 / blk)
            for hh in range(HEADS_PER_BLOCK):
                in_head = (lane >= hh * HEAD_DIM) & (lane < (hh + 1) * HEAD_DIM)
                n2 = jnp.sum(jnp.where(in_head, kb * kb, 0.0), axis=1, keepdims=True)
                n2 = jnp.max(jnp.broadcast_to(n2, (blk, LANES)), axis=0, keepdims=True)
                kn2_ref[hh] = jnp.maximum(kn2_ref[hh], jnp.broadcast_to(n2, (8, LANES)))
            return c

        lax.fori_loop(0, n_blk, body, 0)
        kx = jnp.zeros((blk, LANES), F32)
        for hh in range(HEADS_PER_BLOCK):
            kx = jnp.where(lane == _MB_POS0 + 2 * hh, slopes[hh] * ((row // LANES) * LANES).astype(F32), kx)
            kx = jnp.where(lane == _MB_POS0 + 2 * hh + 1, slopes[hh] * (row % LANES).astype(F32), kx)
        kx_ref[...] = kx.astype(BF16)

    q2 = q_ref[0]
    km_parts = _split3(kmean_ref[...])
    for hh in range(HEADS_PER_BLOCK):
        in_head = (lane >= hh * HEAD_DIM) & (lane < (hh + 1) * HEAD_DIM)
        qm = jnp.where(in_head, q2, jnp.zeros_like(q2))
        gate = _dot_nt(qm, km_parts[0]) + _dot_nt(qm, km_parts[1]) + _dot_nt(qm, km_parts[2])
        gate = jnp.where(lane < qi, gate, NEG_INF)
        gate = jnp.where(lane < n_blk, gate, -jnp.inf)
        sel = jnp.zeros((blk, LANES), jnp.bool_)
        for _ in range(MOBA_TOPK):
            gmax = jnp.max(gate, axis=1, keepdims=True)
            first = jnp.min(jnp.where(gate == gmax, lane_f, float(LANES)), axis=1, keepdims=True)
            pick = lane_f == first
            sel = sel | pick
            gate = jnp.where(pick, -jnp.inf, gate)
        sel = sel & (lane < qi)
        extra = jnp.where(sel, 0.0, NEG_INF)
        extra = jnp.where(lane >= n_blk, 0.0, extra)
        extra = jnp.where((lane == _MB_POS0 + 2 * hh) | (lane == _MB_POS0 + 2 * hh + 1) | (lane == _MB_KILL),
                          1.0, extra)
        blk_lane = lane - (_MB_BLK0 + 32 * hh)
        extra = jnp.where((blk_lane >= 0) & (blk_lane < 32),
                          slopes[hh] * (blk_lane * blk).astype(F32), extra)
        qaug_ref[hh] = jnp.concatenate([qm * jnp.asarray(SCALE, BF16), extra.astype(BF16)], axis=1)
        m_ref[hh] = jnp.full((blk, LANES), -jnp.inf, F32)
        acc_ref[hh] = jnp.zeros((blk, LANES), F32)

    lane1 = lax.broadcasted_iota(jnp.int32, (1, LANES), 1)
    rc = _ROW_CHUNK

    def score_stage(j, buf, *, is_own=False, valid=True):
        kj = k_ref[0, pl.ds(pl.multiple_of(j * blk, blk), blk), :]
        ind = (lane1 % 32 == j) & (lane1 >= _MB_BLK0) if is_own else \
              (lane1 % 32 == j) & ((lane1 < 32) | (lane1 >= _MB_BLK0))
        kill = jnp.where(lane1 == _MB_KILL, jnp.where(valid, 0.0, NEG_INF), 0.0)
        side = jnp.broadcast_to(jnp.where(ind, 1.0, kill), (blk, LANES)).astype(BF16)
        kx = jnp.where(ind | (lane1 == _MB_KILL), side, kx_ref[...])
        k_aug = jnp.concatenate([kj, kx], axis=1)
        for hh in range(HEADS_PER_BLOCK):
            s_ref[buf, hh] = _dot_nt(qaug_ref[hh], k_aug)

    def softmax_stage(buf, *, is_own=False):
        for hh in range(HEADS_PER_BLOCK):
            for c in range(blk // rc):
                rows = slice(c * rc, (c + 1) * rc)
                s = s_ref[buf, hh, rows, :]
                if is_own:
                    col_c = lax.broadcasted_iota(jnp.int32, (rc, blk), 1)
                    row_c = lax.broadcasted_iota(jnp.int32, (rc, blk), 0) + c * rc
                    s = jnp.where(col_c <= row_c, s, NEG_INF)
                m_old = m_ref[hh, rows, :]
                m_new = jnp.maximum(m_old, jnp.max(s, axis=1, keepdims=True))
                alpha_ref[buf, hh, rows, :] = jnp.exp(m_old - m_new)
                m_ref[hh, rows, :] = m_new
                p = jnp.exp(s - jnp.concatenate([m_new, m_new], axis=1))
                p_ref[buf, hh, rows, :] = p.astype(BF16)

    def value_stage(j, buf):
        vj = v_ref[0, pl.ds(pl.multiple_of(j * blk, blk), blk), :]
        for hh in range(HEADS_PER_BLOCK):
            in_head1 = (lane1 >= hh * HEAD_DIM) & (lane1 < (hh + 1) * HEAD_DIM)
            v_aug = jnp.where(in_head1, vj, jnp.ones_like(vj))
            acc_ref[hh] = alpha_ref[buf, hh] * acc_ref[hh] + _dot(p_ref[buf, hh], v_aug)

    score_stage(qi, 0, is_own=True)
    softmax_stage(0, is_own=True)
    value_stage(qi, 0)

    n_past = qi

    @pl.when(n_past > 0)
    def _():
        s_ref[1] = jnp.full(s_ref.shape[1:], NEG_INF, F32)
        p_ref[...] = jnp.zeros_like(p_ref)
        alpha_ref[...] = jnp.ones_like(alpha_ref)
        last = n_past - 1

        reach = jnp.zeros((1, LANES), F32)
        for hh in range(HEADS_PER_BLOCK):
            in_head = (lane >= hh * HEAD_DIM) & (lane < (hh + 1) * HEAD_DIM)
            qf = q2.astype(F32)
            qn2 = jnp.sum(jnp.where(in_head, qf * qf, 0.0), axis=1, keepdims=True)
            qn2 = jnp.max(jnp.broadcast_to(qn2, (blk, LANES)), axis=0, keepdims=True)
            bound = jnp.sqrt(qn2 * kn2_ref[hh, 0:1, :]) * SCALE
            m_min = jnp.min(m_ref[hh], axis=0, keepdims=True)
            reach = jnp.maximum(reach, (bound - m_min + MOBA_DEAD_GAP) / slopes[hh])
        n_past_f = jnp.full((1, LANES), n_past, jnp.int32).astype(F32)
        n_live = jnp.max(jnp.minimum((reach - 1.0) / blk + 1.0, n_past_f))

        def pair(state):
            tt, tf = state
            t = 2 * tt
            score_stage(jnp.clip(last - t, 0, last), 0, valid=t < n_past)
            softmax_stage(1)
            value_stage(jnp.clip(last - (t - 2), 0, last), 0)
            score_stage(jnp.clip(last - (t + 1), 0, last), 1, valid=t + 1 < n_past)
            softmax_stage(0)
            value_stage(jnp.clip(last - (t - 1), 0, last), 1)
            return tt + 1, tf + 2.0

        lax.while_loop(lambda st: st[1] < n_live + 2.0, pair, (jnp.int32(0), jnp.float32(0.0)))

    acc0 = acc_ref[0]
    acc1 = acc_ref[1]
    o0 = acc0 / pltpu.roll(acc0, HEAD_DIM, axis=1)
    o1 = acc1 / pltpu.roll(acc1, HEAD_DIM, axis=1)
    o_ref[0] = jnp.where(lane < HEAD_DIM, o0, o1).astype(BF16)


def _moba_attention(qkv):
    b, s, _ = qkv.shape
    blk = MOBA_BLOCK
    n_blk = s // blk
    assert s % blk == 0 and MOBA_TOPK <= n_blk - 1 and n_blk <= 32
    n_pairs = N_HEADS_MOBA // HEADS_PER_BLOCK
    return pl.pallas_call(
        functools.partial(_moba_kernel, n_blk=n_blk),
        out_shape=jax.ShapeDtypeStruct((b, s, N_HEADS_MOBA * HEAD_DIM), BF16),
        grid=(b, n_pairs, n_blk),
        in_specs=[pl.BlockSpec((1, blk, LANES), lambda bi, p, i: (bi, i, Q_BLK0 + p)),
                  pl.BlockSpec((1, s, LANES), lambda bi, p, i: (bi, 0, K_BLK0 + p)),
                  pl.BlockSpec((1, s, LANES), lambda bi, p, i: (bi, 0, V_BLK0 + p))],
        out_specs=pl.BlockSpec((1, blk, LANES), lambda bi, p, i: (bi, i, p)),
        scratch_shapes=[pltpu.VMEM((LANES, LANES), F32),
                        pltpu.VMEM((HEADS_PER_BLOCK, 8, LANES), F32),
                        pltpu.VMEM((blk, LANES), BF16),
                        pltpu.VMEM((HEADS_PER_BLOCK, blk, 2 * LANES), BF16),
                        pltpu.VMEM((HEADS_PER_BLOCK, blk, LANES), F32),
                        pltpu.VMEM((2, HEADS_PER_BLOCK, blk, LANES), F32),
                        pltpu.VMEM((HEADS_PER_BLOCK, blk, LANES), F32),
                        pltpu.VMEM((2, HEADS_PER_BLOCK, blk, blk), F32),
                        pltpu.VMEM((2, HEADS_PER_BLOCK, blk, blk), BF16)],
        compiler_params=_params(3),
        name="moba_attn",
    )(qkv, qkv, qkv)


SB_TILE = 256


def _softplus(z):
    return jnp.maximum(z, 0.0) + jnp.log(1.0 + jnp.exp(-jnp.abs(z)))


_SB_SLOTS = 3
SB_DEAD_MASS = 128.0


def _sb_kernel(q_ref, k_ref, v_ref, u_ref, o_ref, qaug_ref, carry_ref, acc_ref,
               z_ref, sphl_ref, c_ref, rs_ref, a_ref):
    qi = pl.program_id(2)
    t = SB_TILE
    rc = _ROW_CHUNK
    lane = lax.broadcasted_iota(jnp.int32, (t, LANES), 1)
    lane1 = lax.broadcasted_iota(jnp.int32, (1, LANES), 1)
    q2 = q_ref[0]
    for hh in range(HEADS_PER_BLOCK):
        in_head = (lane >= hh * HEAD_DIM) & (lane < (hh + 1) * HEAD_DIM)
        qm = jnp.where(in_head, q2, jnp.zeros_like(q2)) * jnp.asarray(SCALE, BF16)
        qaug_ref[hh] = jnp.concatenate([qm, jnp.where(lane == 0, 1.0, 0.0).astype(BF16)], axis=1)

    def score_stage(j, slot, *, valid=True):
        kj = k_ref[0, pl.ds(pl.multiple_of(j * t, t), t), :]
        kill = jnp.where(lane1 == 0, jnp.where(valid, 0.0, NEG_INF), 0.0)
        k_aug = jnp.concatenate([kj, jnp.broadcast_to(kill, (t, LANES)).astype(BF16)], axis=1)
        for hh in range(HEADS_PER_BLOCK):
            z_ref[slot, hh] = _dot_nt(qaug_ref[hh], k_aug)

    def _past_mask(c):
        col_c = lax.broadcasted_iota(jnp.int32, (rc, t), 1)
        row_c = lax.broadcasted_iota(jnp.int32, (rc, t), 0) + c * rc
        return col_c < row_c

    def softplus_stage(slot, *, diagonal=False):
        for hh in range(HEADS_PER_BLOCK):
            for c in range(t // rc):
                rows = slice(c * rc, (c + 1) * rc)
                sp = _softplus(z_ref[slot, hh, rows, :])
                if diagonal:
                    sp = jnp.where(_past_mask(c), sp, 0.0)
                hi, lo = _split2(sp)
                sphl_ref[slot, pl.ds((2 * hh) * t + c * rc, rc), :] = hi
                sphl_ref[slot, pl.ds((2 * hh + 1) * t + c * rc, rc), :] = lo
                rs_ref[slot, hh, rows, :] = jnp.broadcast_to(jnp.sum(sp, axis=1, keepdims=True), (rc, LANES))

    def suffix_stage(slot):
        c_ref[slot] = _dot(sphl_ref[slot], u_ref[...])

    def weight_stage(slot, *, diagonal=False):
        for hh in range(HEADS_PER_BLOCK):
            for c in range(t // rc):
                rows = slice(c * rc, (c + 1) * rc)
                z = z_ref[slot, hh, rows, :]
                cc = c_ref[slot, pl.ds((2 * hh) * t + c * rc, rc), :] + \
                    c_ref[slot, pl.ds((2 * hh + 1) * t + c * rc, rc), :]
                if diagonal:
                    a = jnp.where(_past_mask(c), jnp.exp(z - cc), 0.0)
                    carry_ref[hh, rows, :] = rs_ref[slot, hh, rows, :]
                else:
                    carry = carry_ref[hh, rows, :]
                    a = jnp.exp(z - (cc + jnp.concatenate([carry, carry], axis=1)))
                    carry_ref[hh, rows, :] = carry + rs_ref[slot, hh, rows, :]
                a_ref[slot, hh, rows, :] = a.astype(BF16)

    def value_stage(j, slot, *, first=False):
        vj = v_ref[0, pl.ds(pl.multiple_of(j * t, t), t), :]
        for hh in range(HEADS_PER_BLOCK):
            av = _dot(a_ref[slot, hh], vj)
            acc_ref[hh] = av if first else acc_ref[hh] + av

    score_stage(qi, 0)
    softplus_stage(0, diagonal=True)
    suffix_stage(0)
    weight_stage(0, diagonal=True)
    value_stage(qi, 0, first=True)

    n_past = qi

    @pl.when(n_past > 0)
    def _():
        last = n_past - 1

        def key_tile(i):
            return jnp.clip(last - i, 0, last)

        def trip(tt, first=False):
            for k in range(_SB_SLOTS):
                i = _SB_SLOTS * tt + k
                score_stage(key_tile(i), k, valid=i < n_past)
                if not (first and k < 1):
                    softplus_stage((k - 1) % _SB_SLOTS)
                    suffix_stage((k - 1) % _SB_SLOTS)
                if not (first and k < 2):
                    weight_stage((k - 2) % _SB_SLOTS)
                    value_stage(key_tile(i - 2), (k - 2) % _SB_SLOTS)
            return tt + 1, jnp.min(carry_ref[...])

        n_trips = (n_past + 2 + _SB_SLOTS - 1) // _SB_SLOTS
        lax.while_loop(lambda st: (st[0] < n_trips) & (st[1] < SB_DEAD_MASS), lambda st: trip(st[0]),
                       trip(jnp.int32(0), first=True))

    o_ref[0] = jnp.where(lane < HEAD_DIM, acc_ref[0], acc_ref[1]).astype(BF16)


def _sb_attention(qkv):
    b, s, _ = qkv.shape
    t = SB_TILE
    assert s % t == 0
    n_pairs = N_HEADS_SB // HEADS_PER_BLOCK
    u = (lax.broadcasted_iota(jnp.int32, (t, t), 0) >= lax.broadcasted_iota(jnp.int32, (t, t), 1)).astype(BF16)
    return pl.pallas_call(
        _sb_kernel,
        out_shape=jax.ShapeDtypeStruct((b, s, N_HEADS_SB * HEAD_DIM), BF16),
        grid=(b, n_pairs, s // t),
        in_specs=[pl.BlockSpec((1, t, LANES), lambda bi, p, i: (bi, i, Q_BLK0 + SB_BLK + p)),
                  pl.BlockSpec((1, s, LANES), lambda bi, p, i: (bi, 0, K_BLK0 + SB_BLK + p)),
                  pl.BlockSpec((1, s, LANES), lambda bi, p, i: (bi, 0, V_BLK0 + SB_BLK + p)),
                  pl.BlockSpec((t, t), lambda bi, p, i: (0, 0))],
        out_specs=pl.BlockSpec((1, t, LANES), lambda bi, p, i: (bi, i, p)),
        scratch_shapes=[pltpu.VMEM((HEADS_PER_BLOCK, t, 2 * LANES), BF16),
                        pltpu.VMEM((HEADS_PER_BLOCK, t, LANES), F32),
                        pltpu.VMEM((HEADS_PER_BLOCK, t, LANES), F32),
                        pltpu.VMEM((_SB_SLOTS, HEADS_PER_BLOCK, t, t), F32),
                        pltpu.VMEM((_SB_SLOTS, 2 * HEADS_PER_BLOCK * t, t), BF16),
                        pltpu.VMEM((_SB_SLOTS, 2 * HEADS_PER_BLOCK * t, t), F32),
                        pltpu.VMEM((_SB_SLOTS, HEADS_PER_BLOCK, t, LANES), F32),
                        pltpu.VMEM((_SB_SLOTS, HEADS_PER_BLOCK, t, t), BF16)],
        compiler_params=_params(3),
        name="sb_attn",
    )(qkv, qkv, qkv, u)


DIL_SPAN = 2048
_DIL_UNROLL = 4


def _dil_kernel(q_ref, k_ref, v_ref, kp_ref, vp_ref, o_ref, qf_ref, kf_ref, vf_ref, oc_ref, lse_ref, bias_ref):
    pair = pl.program_id(1)
    i = pl.program_id(2)
    n = DIL_N
    span = DIL_SPAN
    lane = lax.broadcasted_iota(jnp.int32, (n, LANES), 1)
    colh = lax.broadcasted_iota(jnp.int32, (n, 2 * n), 1)

    @pl.when(i == 0)
    def _():
        row = lax.broadcasted_iota(jnp.int32, (n, 2 * n), 0)
        delta = row + n - colh
        valid = (delta >= 0) & (delta <= n)
        for ci, (_, d) in enumerate(DIL_CONFIGS):
            for hh in range(HEADS_PER_BLOCK):
                slope = jnp.asarray(2.0 ** -(hh + 1), F32)
                for p in range(1, N_HEADS_DIL // HEADS_PER_BLOCK):
                    slope = jnp.where(pair == p, 2.0 ** -(2 * p + hh + 1), slope)
                bias_ref[ci, hh] = jnp.where(valid, -slope * (delta * d).astype(F32), NEG_INF)

    qf_ref[...] = q_ref[0].astype(F32)
    kf_ref[0:span, :] = kp_ref[0].astype(F32)
    kf_ref[span:2 * span, :] = k_ref[0].astype(F32)
    vf_ref[0:span, :] = vp_ref[0].astype(F32)
    vf_ref[span:2 * span, :] = v_ref[0].astype(F32)

    def rows(start, size, d):
        return pl.ds(start, size) if d == 1 else pl.ds(start, size, stride=d)

    def unit(ci, d, r, bl):
        q0 = r + bl * (n * d)
        qu = qf_ref[rows(q0, n, d), :].astype(BF16)
        ku = kf_ref[rows(span + q0 - n * d, 2 * n, d), :].astype(BF16)
        vu = vf_ref[rows(span + q0 - n * d, 2 * n, d), :].astype(BF16)
        no_prev = jnp.where((i == 0) & (bl == 0), NEG_INF, 0.0)
        o_h, lse_h = [], []
        for hh in range(HEADS_PER_BLOCK):
            in_head = (lane >= hh * HEAD_DIM) & (lane < (hh + 1) * HEAD_DIM)
            qm = jnp.where(in_head, qu, jnp.zeros_like(qu)) * jnp.asarray(SCALE, BF16)
            s = _dot_nt(qm, ku) + bias_ref[ci, hh] + jnp.where(colh < n, no_prev, 0.0)
            m = jnp.max(s, axis=1, keepdims=True)
            p = jnp.exp(s - m)
            den = jnp.sum(p, axis=1, keepdims=True)
            o_h.append(_dot(p.astype(BF16), vu) / den)
            lse_h.append(jnp.broadcast_to(m + jnp.log(den), (n, LANES)))
        oc_ref[ci, rows(q0, n, d), :] = jnp.where(lane < HEAD_DIM, o_h[0], o_h[1])
        lse_ref[ci, rows(q0, n, d), :] = jnp.where(lane < HEAD_DIM, lse_h[0], lse_h[1])

    n_units = span // n
    for ci, (_, d) in enumerate(DIL_CONFIGS):
        per_res = n_units // d

        def group(g, c, ci=ci, d=d, per_res=per_res):
            for k in range(_DIL_UNROLL):
                u = g * _DIL_UNROLL + k
                unit(ci, d, u // per_res, u % per_res)
            return c

        lax.fori_loop(0, n_units // _DIL_UNROLL, group, 0)

    mc = 256
    for c0 in range(0, span, mc):
        l1, l2, l3 = (lse_ref[ci, c0:c0 + mc, :] for ci in range(3))
        lmax = jnp.maximum(jnp.maximum(l1, l2), l3)
        e1, e2, e3 = jnp.exp(l1 - lmax), jnp.exp(l2 - lmax), jnp.exp(l3 - lmax)
        mix = (e1 * oc_ref[0, c0:c0 + mc, :] + e2 * oc_ref[1, c0:c0 + mc, :] + e3 * oc_ref[2, c0:c0 + mc, :]) \
            / (e1 + e2 + e3)
        o_ref[0, c0:c0 + mc, :] = mix.astype(BF16)


def _dilated_mixture(qkv):
    b, s, _ = qkv.shape
    span = DIL_SPAN
    assert s % span == 0 and all(w <= span and span % (DIL_N * d) == 0 for w, d in DIL_CONFIGS)
    n_pairs = N_HEADS_DIL // HEADS_PER_BLOCK
    n_cfg = len(DIL_CONFIGS)

    def cur(off):
        return lambda bi, p, i: (bi, i, off + DIL_BLK + p)

    def prev(off):
        return lambda bi, p, i: (bi, jnp.maximum(i - 1, 0), off + DIL_BLK + p)

    blk = (1, span, LANES)
    return pl.pallas_call(
        _dil_kernel,
        out_shape=jax.ShapeDtypeStruct((b, s, N_HEADS_DIL * HEAD_DIM), BF16),
        grid=(b, n_pairs, s // span),
        in_specs=[pl.BlockSpec(blk, cur(Q_BLK0)), pl.BlockSpec(blk, cur(K_BLK0)), pl.BlockSpec(blk, cur(V_BLK0)),
                  pl.BlockSpec(blk, prev(K_BLK0)), pl.BlockSpec(blk, prev(V_BLK0))],
        out_specs=pl.BlockSpec(blk, lambda bi, p, i: (bi, i, p)),
        scratch_shapes=[pltpu.VMEM((span, LANES), F32),
                        pltpu.VMEM((2 * span, LANES), F32),
                        pltpu.VMEM((2 * span, LANES), F32),
                        pltpu.VMEM((n_cfg, span, LANES), F32),
                        pltpu.VMEM((n_cfg, span, LANES), F32),
                        pltpu.VMEM((n_cfg, HEADS_PER_BLOCK, DIL_N, 2 * DIL_N), F32)],
        compiler_params=_params(3),
        name="dilated_attn",
    )(qkv, qkv, qkv, qkv, qkv)


def _rms(x, g):
    return x * lax.rsqrt(jnp.mean(x * x, axis=-1, keepdims=True) + NORM_EPS) * g


def _outproj_kernel(x_ref, oa_ref, ob_ref, oc_ref, g_ref, w_ref, out_ref):
    g = g_ref[...]
    wa = N_HEADS_MOBA * HEAD_DIM
    wb = wa + N_HEADS_SB * HEAD_DIM
    y = jnp.concatenate([_rms(oa_ref[...].astype(F32), g[:, :wa]),
                         _rms(ob_ref[...].astype(F32), g[:, wa:wb]),
                         _rms(oc_ref[...].astype(F32), g[:, wb:])], axis=1).astype(BF16)
    out_ref[...] = x_ref[...] + _dot(y, w_ref[...])


def _out_proj(x2, oa, ob, oc, g, w, *, tm=512):
    t, d = x2.shape
    row = lambda i: (i, 0)
    const = lambda i: (0, 0)
    return pl.pallas_call(
        _outproj_kernel,
        out_shape=jax.ShapeDtypeStruct((t, d), F32),
        grid=(t // tm,),
        in_specs=[pl.BlockSpec((tm, d), row),
                  pl.BlockSpec((tm, oa.shape[1]), row), pl.BlockSpec((tm, ob.shape[1]), row),
                  pl.BlockSpec((tm, oc.shape[1]), row),
                  pl.BlockSpec((1, d), const), pl.BlockSpec((d, d), const)],
        out_specs=pl.BlockSpec((tm, d), row),
        compiler_params=_params(1),
        name="out_proj",
    )(x2, oa, ob, oc, g.reshape(1, d), w)


_EXP_LANE0 = N_GROUPS
_MOE_ROWS = 256


def _moe_kernel(x_ref, g_ref, wrh_ref, wrl_ref, br_ref, wg_ref, wu_ref, wd_ref, out_ref,
                h_ref, comb_ref, acc_ref):
    e = pl.program_id(1)
    tm = x_ref.shape[0]
    lane = lax.broadcasted_iota(jnp.int32, (tm, LANES), 1)

    @pl.when(e == 0)
    def _():
        x = x_ref[...]
        h = _rms(x, g_ref[...])
        h_ref[...] = h.astype(BF16)
        hh, hl = _split2(h)
        logits = _dot(hh, wrh_ref[...]) + _dot(hh, wrl_ref[...]) + _dot(hl, wrh_ref[...]) + br_ref[...]
        lane_f = lane.astype(F32)
        big = float(LANES)
        gl = jnp.where(lane < N_GROUPS, logits, -jnp.inf)
        gmax = jnp.max(gl, axis=1, keepdims=True)
        gidx = jnp.min(jnp.where(gl == gmax, lane_f, big), axis=1, keepdims=True)
        g_w = 1.0 / jnp.sum(jnp.exp(gl - gmax), axis=1, keepdims=True)
        lane_group = ((lane - _EXP_LANE0) // EXPERTS_PER_GROUP).astype(F32)
        in_group = (lane >= _EXP_LANE0) & (lane < _EXP_LANE0 + N_EXPERTS) & (lane_group == gidx)
        el = jnp.where(in_group, logits, -jnp.inf)
        v1 = jnp.max(el, axis=1, keepdims=True)
        i1 = jnp.min(jnp.where(el == v1, lane_f, big), axis=1, keepdims=True)
        el2 = jnp.where(lane_f == i1, -jnp.inf, el)
        v2 = jnp.max(el2, axis=1, keepdims=True)
        i2 = jnp.min(jnp.where(el2 == v2, lane_f, big), axis=1, keepdims=True)
        r = jnp.exp(v2 - v1)
        w1 = g_w / (1.0 + r)
        w2 = g_w * r / (1.0 + r)
        comb_ref[...] = jnp.where(lane_f == i1, w1, 0.0) + jnp.where(lane_f == i2, w2, 0.0)
        acc_ref[...] = jnp.zeros_like(acc_ref)

    lane_c = lax.broadcasted_iota(jnp.int32, (_MOE_ROWS, LANES), 1)
    for c in range(tm // _MOE_ROWS):
        rows = slice(c * _MOE_ROWS, (c + 1) * _MOE_ROWS)
        h = h_ref[rows, :]
        comb = comb_ref[rows, :]
        acts = []
        for k in range(EXPERTS_PER_GROUP):
            gate = _dot(h, wg_ref[0, k])
            up = _dot(h, wu_ref[0, k])
            cw = jnp.sum(jnp.where(lane_c == _EXP_LANE0 + e * EXPERTS_PER_GROUP + k, comb, 0.0),
                         axis=1, keepdims=True)
            acts.append((gate / (1.0 + jnp.exp(-gate)) * up * cw).astype(BF16))
        acc_ref[rows, :] += _dot(jnp.concatenate(acts, axis=1), wd_ref[0])

    @pl.when(e == N_GROUPS - 1)
    def _():
        out_ref[...] = x_ref[...] + acc_ref[...]


def _moe(x2, g, wr_hi, wr_lo, br, wg, wu, wd, *, tm=1024):
    t, d = x2.shape
    f = wg.shape[3]
    row = lambda i, e: (i, 0)
    const = lambda i, e: (0, 0)
    return pl.pallas_call(
        _moe_kernel,
        out_shape=jax.ShapeDtypeStruct((t, d), F32),
        grid=(t // tm, N_GROUPS),
        in_specs=[pl.BlockSpec((tm, d), row),
                  pl.BlockSpec((1, d), const),
                  pl.BlockSpec((d, LANES), const), pl.BlockSpec((d, LANES), const),
                  pl.BlockSpec((1, LANES), const),
                  pl.BlockSpec((1, EXPERTS_PER_GROUP, d, f), lambda i, e: (e, 0, 0, 0)),
                  pl.BlockSpec((1, EXPERTS_PER_GROUP, d, f), lambda i, e: (e, 0, 0, 0)),
                  pl.BlockSpec((1, EXPERTS_PER_GROUP * f, d), lambda i, e: (e, 0, 0))],
        out_specs=pl.BlockSpec((tm, d), row),
        scratch_shapes=[pltpu.VMEM((tm, d), BF16), pltpu.VMEM((tm, LANES), F32), pltpu.VMEM((tm, d), F32)],
        compiler_params=_params(2),
        name="hier_moe",
    )(x2, g.reshape(1, d), wr_hi, wr_lo, br, wg, wu, wd)


def _final_norm_kernel(x_ref, g_ref, o_ref):
    o_ref[...] = _rms(x_ref[...], g_ref[...])


def _final_norm(x2, g, *, tm=1024):
    t, d = x2.shape
    return pl.pallas_call(
        _final_norm_kernel,
        out_shape=jax.ShapeDtypeStruct((t, d), F32),
        grid=(t // tm,),
        in_specs=[pl.BlockSpec((tm, d), lambda i: (i, 0)), pl.BlockSpec((1, d), lambda i: (0, 0))],
        out_specs=pl.BlockSpec((tm, d), lambda i: (i, 0)),
        compiler_params=_params(1),
        name="final_norm",
    )(x2, g.reshape(1, d))


def _router_weights(w_gr, b_gr, w_er, b_er):
    d = w_gr.shape[0]
    w = jnp.concatenate([w_gr, jnp.moveaxis(w_er, 0, 1).reshape(d, N_EXPERTS)], axis=1)
    w = jnp.pad(w, ((0, 0), (0, LANES - w.shape[1])))
    bias = jnp.pad(jnp.concatenate([b_gr, b_er.reshape(-1)]), (0, LANES - N_GROUPS - N_EXPERTS))
    hi, lo = _split2(w)
    return hi, lo, bias.reshape(1, LANES)


def _layer(x2, b, s, ln1_g, w_in, mix_g, w_out, ln2_g, w_gr, b_gr, w_er, b_er, w_gate, w_up, w_down):
    t, d = x2.shape
    qkv = _qkv_proj(x2, ln1_g, w_in.astype(BF16)).reshape(b, s, 3 * d)
    oa = _moba_attention(qkv)
    ob = _sb_attention(qkv)
    oc = _dilated_mixture(qkv)
    x2 = _out_proj(x2, oa.reshape(t, -1), ob.reshape(t, -1), oc.reshape(t, -1), mix_g, w_out.astype(BF16))
    wr_hi, wr_lo, br = _router_weights(w_gr, b_gr, w_er, b_er)
    f = w_gate.shape[-1]
    return _moe(x2, ln2_g, wr_hi, wr_lo, br, w_gate.astype(BF16), w_up.astype(BF16),
                w_down.reshape(N_GROUPS, EXPERTS_PER_GROUP * f, d).astype(BF16))


def kernel(x, ln1_g, w_in, mix_norm_g, w_out, ln2_g, w_group_router, b_group_router,
           w_expert_router, b_expert_router, w_gate, w_up, w_down, final_norm_g):
    b, s, d = x.shape
    x2 = x.reshape(b * s, d)
    for l in range(ln1_g.shape[0]):
        x2 = _layer(x2, b, s, ln1_g[l], w_in[l], mix_norm_g[l], w_out[l], ln2_g[l],
                    w_group_router[l], b_group_router[l], w_expert_router[l], b_expert_router[l],
                    w_gate[l], w_up[l], w_down[l])
    return _final_norm(x2, final_norm_g).reshape(b, s, d)
```

```python
import functools

import jax
import jax.numpy as jnp
from jax import lax
from jax.experimental import pallas as pl
from jax.experimental.pallas import tpu as pltpu

F32 = jnp.float32
BF16 = jnp.bfloat16

D_MODEL = 1024
HEAD_DIM = 64
N_HEADS = 16
LANES = 128
HEADS_PER_BLOCK = LANES // HEAD_DIM
N_HEADS_MOBA = 4
N_HEADS_SB = 4
N_HEADS_DIL = 8
MOBA_BLOCK = 256
MOBA_TOPK = 3
DIL_CONFIGS = ((128, 1), (512, 4), (2048, 16))
DIL_N = 128
N_GROUPS = 4
EXPERTS_PER_GROUP = 4
N_EXPERTS = N_GROUPS * EXPERTS_PER_GROUP
D_EXPERT = 256
NORM_EPS = 1e-6
NEG_INF = -1e30
SCALE = HEAD_DIM ** -0.5

Q_BLK0 = 0
K_BLK0 = D_MODEL // LANES
V_BLK0 = 2 * D_MODEL // LANES
ROW_BLKS = 3 * D_MODEL // LANES
SB_BLK = N_HEADS_MOBA // HEADS_PER_BLOCK
DIL_BLK = (N_HEADS_MOBA + N_HEADS_SB) // HEADS_PER_BLOCK

VMEM_LIMIT = 56 * 1024 * 1024


def _params(n_axes, vmem=VMEM_LIMIT):
    return pltpu.CompilerParams(dimension_semantics=("arbitrary",) * n_axes,
                                vmem_limit_bytes=vmem)


def _dot_nt(a, b):
    return lax.dot_general(a, b, (((1,), (1,)), ((), ())), preferred_element_type=F32)


def _dot(a, b):
    return jnp.dot(a, b, preferred_element_type=F32)


def _split3(x):
    hi = x.astype(BF16)
    r1 = x - hi.astype(F32)
    mid = r1.astype(BF16)
    lo = (r1 - mid.astype(F32)).astype(BF16)
    return hi, mid, lo


def _split2(x):
    hi = x.astype(BF16)
    lo = (x - hi.astype(F32)).astype(BF16)
    return hi, lo


def _qkv_kernel(x_ref, g_ref, w_ref, o_ref, *, rows, tn):
    for c in range(x_ref.shape[0] // rows):
        r = slice(c * rows, (c + 1) * rows)
        x = x_ref[r, :]
        ms = jnp.mean(x * x, axis=-1, keepdims=True)
        h = (x * lax.rsqrt(ms + NORM_EPS) * g_ref[...]).astype(BF16)
        for j in range(w_ref.shape[1] // tn):
            o_ref[r, j * tn:(j + 1) * tn] = _dot(h, w_ref[:, j * tn:(j + 1) * tn]).astype(BF16)


def _qkv_proj(x2, g, w, *, tm=512, rows=256, tn=1024):
    t, d = x2.shape
    n = w.shape[1]
    return pl.pallas_call(
        functools.partial(_qkv_kernel, rows=rows, tn=tn),
        out_shape=jax.ShapeDtypeStruct((t, n), BF16),
        grid=(t // tm,),
        in_specs=[pl.BlockSpec((tm, d), lambda i: (i, 0)),
                  pl.BlockSpec((1, d), lambda i: (0, 0)),
                  pl.BlockSpec((d, n), lambda i: (0, 0))],
        out_specs=pl.BlockSpec((tm, n), lambda i: (i, 0)),
        compiler_params=_params(1),
        name="qkv_proj",
    )(x2, g.reshape(1, d), w)


_MB_SEL0 = 0
_MB_POS0 = 32
_MB_KILL = 40
_MB_BLK0 = 64


_ROW_CHUNK = 32


MOBA_DEAD_GAP = 110.0


def _moba_kernel(q_ref, k_ref, v_ref, o_ref, kmean_ref, kn2_ref, kx_ref, qaug_ref, m_ref, alpha_ref, acc_ref,
                 s_ref, p_ref, *, n_blk):
    pair = pl.program_id(1)
    qi = pl.program_id(2)
    blk = MOBA_BLOCK
    lane = lax.broadcasted_iota(jnp.int32, (blk, LANES), 1)
    row = lax.broadcasted_iota(jnp.int32, (blk, LANES), 0)
    lane_f = lane.astype(F32)
    slopes = [jnp.where(pair == 0, 2.0 ** (-2 * (hh + 1)), 2.0 ** (-2 * (hh + 3))).astype(F32)
              for hh in range(HEADS_PER_BLOCK)]

    @pl.when(qi == 0)
    def _():
        kmean_ref[...] = jnp.zeros_like(kmean_ref)
        kn2_ref[...] = jnp.zeros_like(kn2_ref)

        def body(n, c):
            kb = k_ref[0, pl.ds(pl.multiple_of(n * blk, blk), blk), :].astype(F32)
            kmean_ref[pl.ds(n, 1), :] = jnp.sum(kb, axis=0, keepdims=True) * (1.0 / blk)
            for hh in range(HEADS_PER_BLOCK):
                in_head = (lane >= hh * HEAD_DIM) & (lane < (hh + 1) * HEAD_DIM)
                n2 = jnp.sum(jnp.where(in_head, kb * kb, 0.0), axis=1, keepdims=True)
                n2 = jnp.max(jnp.broadcast_to(n2, (blk, LANES)), axis=0, keepdims=True)
                kn2_ref[hh] = jnp.maximum(kn2_ref[hh], jnp.broadcast_to(n2, (8, LANES)))
            return c

        lax.fori_loop(0, n_blk, body, 0)
        kx = jnp.zeros((blk, LANES), F32)
        for hh in range(HEADS_PER_BLOCK):
            kx = jnp.where(lane == _MB_POS0 + 2 * hh, slopes[hh] * ((row // LANES) * LANES).astype(F32), kx)
            kx = jnp.where(lane == _MB_POS0 + 2 * hh + 1, slopes[hh] * (row % LANES).astype(F32), kx)
        kx_ref[...] = kx.astype(BF16)

    q2 = q_ref[0]
    km_parts = _split3(kmean_ref[...])
    for hh in range(HEADS_PER_BLOCK):
        in_head = (lane >= hh * HEAD_DIM) & (lane < (hh + 1) * HEAD_DIM)
        qm = jnp.where(in_head, q2, jnp.zeros_like(q2))
        gate = _dot_nt(qm, km_parts[0]) + _dot_nt(qm, km_parts[1]) + _dot_nt(qm, km_parts[2])
        gate = jnp.where(lane < qi, gate, NEG_INF)
        gate = jnp.where(lane < n_blk, gate, -jnp.inf)
        sel = jnp.zeros((blk, LANES), jnp.bool_)
        for _ in range(MOBA_TOPK):
            gmax = jnp.max(gate, axis=1, keepdims=True)
            first = jnp.min(jnp.where(gate == gmax, lane_f, float(LANES)), axis=1, keepdims=True)
            pick = lane_f == first
            sel = sel | pick
            gate = jnp.where(pick, -jnp.inf, gate)
        sel = sel & (lane < qi)
        extra = jnp.where(sel, 0.0, NEG_INF)
        extra = jnp.where(lane >= n_blk, 0.0, extra)
        extra = jnp.where((lane == _MB_POS0 + 2 * hh) | (lane == _MB_POS0 + 2 * hh + 1) | (lane == _MB_KILL),
                          1.0, extra)
        blk_lane = lane - (_MB_BLK0 + 32 * hh)
        extra = jnp.where((blk_lane >= 0) & (blk_lane < 32),
                          slopes[hh] * (blk_lane * blk).astype(F32), extra)
        qaug_ref[hh] = jnp.concatenate([qm * jnp.asarray(SCALE, BF16), extra.astype(BF16)], axis=1)
        m_ref[hh] = jnp.full((blk, LANES), -jnp.inf, F32)
        acc_ref[hh] = jnp.zeros((blk, LANES), F32)

    lane1 = lax.broadcasted_iota(jnp.int32, (1, LANES), 1)
    rc = _ROW_CHUNK

    def score_stage(j, buf, *, is_own=False, valid=True):
        kj = k_ref[0, pl.ds(pl.multiple_of(j * blk, blk), blk), :]
        ind = (lane1 % 32 == j) & (lane1 >= _MB_BLK0) if is_own else \
              (lane1 % 32 == j) & ((lane1 < 32) | (lane1 >= _MB_BLK0))
        kill = jnp.where(lane1 == _MB_KILL, jnp.where(valid, 0.0, NEG_INF), 0.0)
        side = jnp.broadcast_to(jnp.where(ind, 1.0, kill), (blk, LANES)).astype(BF16)
        kx = jnp.where(ind | (lane1 == _MB_KILL), side, kx_ref[...])
        k_aug = jnp.concatenate([kj, kx], axis=1)
        s = _dot_nt(qaug_ref[...].reshape(HEADS_PER_BLOCK * blk, 2 * LANES), k_aug)
        s_ref[buf] = s.reshape(HEADS_PER_BLOCK, blk, blk)

    def softmax_stage(buf, *, is_own=False):
        for hh in range(HEADS_PER_BLOCK):
            for c in range(blk // rc):
                rows = slice(c * rc, (c + 1) * rc)
                s = s_ref[buf, hh, rows, :]
                if is_own:
                    col_c = lax.broadcasted_iota(jnp.int32, (rc, blk), 1)
                    row_c = lax.broadcasted_iota(jnp.int32, (rc, blk), 0) + c * rc
                    s = jnp.where(col_c <= row_c, s, NEG_INF)
                m_old = m_ref[hh, rows, :]
                m_new = jnp.maximum(m_old, jnp.max(s, axis=1, keepdims=True))
                alpha_ref[buf, hh, rows, :] = jnp.exp(m_old - m_new)
                m_ref[hh, rows, :] = m_new
                p = jnp.exp(s - jnp.concatenate([m_new, m_new], axis=1))
                p_ref[buf, hh, rows, :] = p.astype(BF16)

    def value_stage(j, buf):
        vj = v_ref[0, pl.ds(pl.multiple_of(j * blk, blk), blk), :]
        v_aug = jnp.concatenate(
            [jnp.where((lane1 >= hh * HEAD_DIM) & (lane1 < (hh + 1) * HEAD_DIM), vj, jnp.ones_like(vj))
             for hh in range(HEADS_PER_BLOCK)], axis=1)
        pv = _dot(p_ref[buf].reshape(HEADS_PER_BLOCK * blk, blk), v_aug)
        for hh in range(HEADS_PER_BLOCK):
            acc_ref[hh] = alpha_ref[buf, hh] * acc_ref[hh] + pv[hh * blk:(hh + 1) * blk, hh * LANES:(hh + 1) * LANES]

    score_stage(qi, 0, is_own=True)
    softmax_stage(0, is_own=True)
    value_stage(qi, 0)

    n_past = qi

    @pl.when(n_past > 0)
    def _():
        s_ref[1] = jnp.full(s_ref.shape[1:], NEG_INF, F32)
        p_ref[...] = jnp.zeros_like(p_ref)
        alpha_ref[...] = jnp.ones_like(alpha_ref)
        last = n_past - 1

        reach = jnp.zeros((1, LANES), F32)
        for hh in range(HEADS_PER_BLOCK):
            in_head = (lane >= hh * HEAD_DIM) & (lane < (hh + 1) * HEAD_DIM)
            qf = q2.astype(F32)
            qn2 = jnp.sum(jnp.where(in_head, qf * qf, 0.0), axis=1, keepdims=True)
            qn2 = jnp.max(jnp.broadcast_to(qn2, (blk, LANES)), axis=0, keepdims=True)
            bound = jnp.sqrt(qn2 * kn2_ref[hh, 0:1, :]) * SCALE
            m_min = jnp.min(m_ref[hh], axis=0, keepdims=True)
            reach = jnp.maximum(reach, (bound - m_min + MOBA_DEAD_GAP) / slopes[hh])
        n_past_f = jnp.full((1, LANES), n_past, jnp.int32).astype(F32)
        n_live = jnp.max(jnp.minimum((reach - 1.0) / blk + 1.0, n_past_f))

        def pair(state):
            tt, tf = state
            t = 2 * tt
            score_stage(jnp.clip(last - t, 0, last), 0, valid=t < n_past)
            softmax_stage(1)
            value_stage(jnp.clip(last - (t - 2), 0, last), 0)
            score_stage(jnp.clip(last - (t + 1), 0, last), 1, valid=t + 1 < n_past)
            softmax_stage(0)
            value_stage(jnp.clip(last - (t - 1), 0, last), 1)
            return tt + 1, tf + 2.0

        lax.while_loop(lambda st: st[1] < n_live + 2.0, pair, (jnp.int32(0), jnp.float32(0.0)))

    acc0 = acc_ref[0]
    acc1 = acc_ref[1]
    o0 = acc0 / pltpu.roll(acc0, HEAD_DIM, axis=1)
    o1 = acc1 / pltpu.roll(acc1, HEAD_DIM, axis=1)
    o_ref[0] = jnp.where(lane < HEAD_DIM, o0, o1).astype(BF16)


def _moba_attention(qkv):
    b, s, _ = qkv.shape
    blk = MOBA_BLOCK
    n_blk = s // blk
    assert s % blk == 0 and MOBA_TOPK <= n_blk - 1 and n_blk <= 32
    n_pairs = N_HEADS_MOBA // HEADS_PER_BLOCK
    return pl.pallas_call(
        functools.partial(_moba_kernel, n_blk=n_blk),
        out_shape=jax.ShapeDtypeStruct((b, s, N_HEADS_MOBA * HEAD_DIM), BF16),
        grid=(b, n_pairs, n_blk),
        in_specs=[pl.BlockSpec((1, blk, LANES), lambda bi, p, i: (bi, i, Q_BLK0 + p)),
                  pl.BlockSpec((1, s, LANES), lambda bi, p, i: (bi, 0, K_BLK0 + p)),
                  pl.BlockSpec((1, s, LANES), lambda bi, p, i: (bi, 0, V_BLK0 + p))],
        out_specs=pl.BlockSpec((1, blk, LANES), lambda bi, p, i: (bi, i, p)),
        scratch_shapes=[pltpu.VMEM((LANES, LANES), F32),
                        pltpu.VMEM((HEADS_PER_BLOCK, 8, LANES), F32),
                        pltpu.VMEM((blk, LANES), BF16),
                        pltpu.VMEM((HEADS_PER_BLOCK, blk, 2 * LANES), BF16),
                        pltpu.VMEM((HEADS_PER_BLOCK, blk, LANES), F32),
                        pltpu.VMEM((2, HEADS_PER_BLOCK, blk, LANES), F32),
                        pltpu.VMEM((HEADS_PER_BLOCK, blk, LANES), F32),
                        pltpu.VMEM((2, HEADS_PER_BLOCK, blk, blk), F32),
                        pltpu.VMEM((2, HEADS_PER_BLOCK, blk, blk), BF16)],
        compiler_params=_params(3),
        name="moba_attn",
    )(qkv, qkv, qkv)


SB_TILE = 256


def _softplus(z):
    return jnp.maximum(z, 0.0) + jnp.log(1.0 + jnp.exp(-jnp.abs(z)))


_SB_SLOTS = 3
SB_DEAD_MASS = 128.0


def _sb_kernel(q_ref, k_ref, v_ref, u_ref, o_ref, qaug_ref, carry_ref, acc_ref,
               z_ref, sphl_ref, c_ref, rs_ref, a_ref):
    qi = pl.program_id(2)
    t = SB_TILE
    rc = _ROW_CHUNK
    lane = lax.broadcasted_iota(jnp.int32, (t, LANES), 1)
    lane1 = lax.broadcasted_iota(jnp.int32, (1, LANES), 1)
    q2 = q_ref[0]
    for hh in range(HEADS_PER_BLOCK):
        in_head = (lane >= hh * HEAD_DIM) & (lane < (hh + 1) * HEAD_DIM)
        qm = jnp.where(in_head, q2, jnp.zeros_like(q2)) * jnp.asarray(SCALE, BF16)
        qaug_ref[hh] = jnp.concatenate([qm, jnp.where(lane == 0, 1.0, 0.0).astype(BF16)], axis=1)

    def score_stage(j, slot, *, valid=True):
        kj = k_ref[0, pl.ds(pl.multiple_of(j * t, t), t), :]
        kill = jnp.where(lane1 == 0, jnp.where(valid, 0.0, NEG_INF), 0.0)
        k_aug = jnp.concatenate([kj, jnp.broadcast_to(kill, (t, LANES)).astype(BF16)], axis=1)
        z = _dot_nt(qaug_ref[...].reshape(HEADS_PER_BLOCK * t, 2 * LANES), k_aug)
        z_ref[slot] = z.reshape(HEADS_PER_BLOCK, t, t)

    def _past_mask(c):
        col_c = lax.broadcasted_iota(jnp.int32, (rc, t), 1)
        row_c = lax.broadcasted_iota(jnp.int32, (rc, t), 0) + c * rc
        return col_c < row_c

    def softplus_stage(slot, *, diagonal=False):
        for hh in range(HEADS_PER_BLOCK):
            for c in range(t // rc):
                rows = slice(c * rc, (c + 1) * rc)
                sp = _softplus(z_ref[slot, hh, rows, :])
                if diagonal:
                    sp = jnp.where(_past_mask(c), sp, 0.0)
                hi, lo = _split2(sp)
                sphl_ref[slot, pl.ds((2 * hh) * t + c * rc, rc), :] = hi
                sphl_ref[slot, pl.ds((2 * hh + 1) * t + c * rc, rc), :] = lo
                rs_ref[slot, hh, rows, :] = jnp.broadcast_to(jnp.sum(sp, axis=1, keepdims=True), (rc, LANES))

    def suffix_stage(slot):
        c_ref[slot] = _dot(sphl_ref[slot], u_ref[...])

    def weight_stage(slot, *, diagonal=False):
        for hh in range(HEADS_PER_BLOCK):
            for c in range(t // rc):
                rows = slice(c * rc, (c + 1) * rc)
                z = z_ref[slot, hh, rows, :]
                cc = c_ref[slot, pl.ds((2 * hh) * t + c * rc, rc), :] + \
                    c_ref[slot, pl.ds((2 * hh + 1) * t + c * rc, rc), :]
                if diagonal:
                    a = jnp.where(_past_mask(c), jnp.exp(z - cc), 0.0)
                    carry_ref[hh, rows, :] = rs_ref[slot, hh, rows, :]
                else:
                    carry = carry_ref[hh, rows, :]
                    a = jnp.exp(z - (cc + jnp.concatenate([carry, carry], axis=1)))
                    carry_ref[hh, rows, :] = carry + rs_ref[slot, hh, rows, :]
                a_ref[slot, hh, rows, :] = a.astype(BF16)

    def value_stage(j, slot, *, first=False):
        vj = v_ref[0, pl.ds(pl.multiple_of(j * t, t), t), :]
        av = _dot(a_ref[slot].reshape(HEADS_PER_BLOCK * t, t), vj)
        for hh in range(HEADS_PER_BLOCK):
            av_h = av[hh * t:(hh + 1) * t, :]
            acc_ref[hh] = av_h if first else acc_ref[hh] + av_h

    score_stage(qi, 0)
    softplus_stage(0, diagonal=True)
    suffix_stage(0)
    weight_stage(0, diagonal=True)
    value_stage(qi, 0, first=True)

    n_past = qi

    @pl.when(n_past > 0)
    def _():
        last = n_past - 1

        def key_tile(i):
            return jnp.clip(last - i, 0, last)

        def trip(tt, first=False):
            for k in range(_SB_SLOTS):
                i = _SB_SLOTS * tt + k
                score_stage(key_tile(i), k, valid=i < n_past)
                if not (first and k < 1):
                    softplus_stage((k - 1) % _SB_SLOTS)
                    suffix_stage((k - 1) % _SB_SLOTS)
                if not (first and k < 2):
                    weight_stage((k - 2) % _SB_SLOTS)
                    value_stage(key_tile(i - 2), (k - 2) % _SB_SLOTS)
            return tt + 1, jnp.min(carry_ref[...])

        n_trips = (n_past + 2 + _SB_SLOTS - 1) // _SB_SLOTS
        lax.while_loop(lambda st: (st[0] < n_trips) & (st[1] < SB_DEAD_MASS), lambda st: trip(st[0]),
                       trip(jnp.int32(0), first=True))

    o_ref[0] = jnp.where(lane < HEAD_DIM, acc_ref[0], acc_ref[1]).astype(BF16)


def _sb_attention(qkv):
    b, s, _ = qkv.shape
    t = SB_TILE
    assert s % t == 0
    n_pairs = N_HEADS_SB // HEADS_PER_BLOCK
    u = (lax.broadcasted_iota(jnp.int32, (t, t), 0) >= lax.broadcasted_iota(jnp.int32, (t, t), 1)).astype(BF16)
    return pl.pallas_call(
        _sb_kernel,
        out_shape=jax.ShapeDtypeStruct((b, s, N_HEADS_SB * HEAD_DIM), BF16),
        grid=(b, n_pairs, s // t),
        in_specs=[pl.BlockSpec((1, t, LANES), lambda bi, p, i: (bi, i, Q_BLK0 + SB_BLK + p)),
                  pl.BlockSpec((1, s, LANES), lambda bi, p, i: (bi, 0, K_BLK0 + SB_BLK + p)),
                  pl.BlockSpec((1, s, LANES), lambda bi, p, i: (bi, 0, V_BLK0 + SB_BLK + p)),
                  pl.BlockSpec((t, t), lambda bi, p, i: (0, 0))],
        out_specs=pl.BlockSpec((1, t, LANES), lambda bi, p, i: (bi, i, p)),
        scratch_shapes=[pltpu.VMEM((HEADS_PER_BLOCK, t, 2 * LANES), BF16),
                        pltpu.VMEM((HEADS_PER_BLOCK, t, LANES), F32),
                        pltpu.VMEM((HEADS_PER_BLOCK, t, LANES), F32),
                        pltpu.VMEM((_SB_SLOTS, HEADS_PER_BLOCK, t, t), F32),
                        pltpu.VMEM((_SB_SLOTS, 2 * HEADS_PER_BLOCK * t, t), BF16),
                        pltpu.VMEM((_SB_SLOTS, 2 * HEADS_PER_BLOCK * t, t), F32),
                        pltpu.VMEM((_SB_SLOTS, HEADS_PER_BLOCK, t, LANES), F32),
                        pltpu.VMEM((_SB_SLOTS, HEADS_PER_BLOCK, t, t), BF16)],
        compiler_params=_params(3),
        name="sb_attn",
    )(qkv, qkv, qkv, u)


DIL_SPAN = 2048
_DIL_UNROLL = 4


def _dil_kernel(q_ref, k_ref, v_ref, kp_ref, vp_ref, o_ref, qf_ref, kf_ref, vf_ref, oc_ref, lse_ref, bias_ref):
    pair = pl.program_id(1)
    i = pl.program_id(2)
    n = DIL_N
    span = DIL_SPAN
    lane = lax.broadcasted_iota(jnp.int32, (n, LANES), 1)
    colh = lax.broadcasted_iota(jnp.int32, (n, 2 * n), 1)

    @pl.when(i == 0)
    def _():
        row = lax.broadcasted_iota(jnp.int32, (n, 2 * n), 0)
        delta = row + n - colh
        valid = (delta >= 0) & (delta <= n)
        for ci, (_, d) in enumerate(DIL_CONFIGS):
            for hh in range(HEADS_PER_BLOCK):
                slope = jnp.asarray(2.0 ** -(hh + 1), F32)
                for p in range(1, N_HEADS_DIL // HEADS_PER_BLOCK):
                    slope = jnp.where(pair == p, 2.0 ** -(2 * p + hh + 1), slope)
                bias_ref[ci, hh] = jnp.where(valid, -slope * (delta * d).astype(F32), NEG_INF)

    qf_ref[...] = q_ref[0].astype(F32)
    kf_ref[0:span, :] = kp_ref[0].astype(F32)
    kf_ref[span:2 * span, :] = k_ref[0].astype(F32)
    vf_ref[0:span, :] = vp_ref[0].astype(F32)
    vf_ref[span:2 * span, :] = v_ref[0].astype(F32)

    def rows(start, size, d):
        return pl.ds(start, size) if d == 1 else pl.ds(start, size, stride=d)

    def unit(ci, d, r, bl):
        q0 = r + bl * (n * d)
        qu = qf_ref[rows(q0, n, d), :].astype(BF16)
        ku = kf_ref[rows(span + q0 - n * d, 2 * n, d), :].astype(BF16)
        vu = vf_ref[rows(span + q0 - n * d, 2 * n, d), :].astype(BF16)
        no_prev = jnp.where((i == 0) & (bl == 0), NEG_INF, 0.0)
        qms = [jnp.where((lane >= hh * HEAD_DIM) & (lane < (hh + 1) * HEAD_DIM), qu, jnp.zeros_like(qu))
               * jnp.asarray(SCALE, BF16) for hh in range(HEADS_PER_BLOCK)]
        s_both = _dot_nt(jnp.concatenate(qms, axis=0), ku)
        ps, dens, lse_h = [], [], []
        for hh in range(HEADS_PER_BLOCK):
            s = s_both[hh * n:(hh + 1) * n, :] + bias_ref[ci, hh] + jnp.where(colh < n, no_prev, 0.0)
            m = jnp.max(s, axis=1, keepdims=True)
            p = jnp.exp(s - m)
            den = jnp.sum(p, axis=1, keepdims=True)
            ps.append(p.astype(BF16))
            dens.append(den)
            lse_h.append(jnp.broadcast_to(m + jnp.log(den), (n, LANES)))
        pv = _dot(jnp.concatenate(ps, axis=0), vu)
        o_h = [pv[hh * n:(hh + 1) * n, :] / dens[hh] for hh in range(HEADS_PER_BLOCK)]
        oc_ref[ci, rows(q0, n, d), :] = jnp.where(lane < HEAD_DIM, o_h[0], o_h[1])
        lse_ref[ci, rows(q0, n, d), :] = jnp.where(lane < HEAD_DIM, lse_h[0], lse_h[1])

    n_units = span // n
    for ci, (_, d) in enumerate(DIL_CONFIGS):
        per_res = n_units // d

        def group(g, c, ci=ci, d=d, per_res=per_res):
            for k in range(_DIL_UNROLL):
                u = g * _DIL_UNROLL + k
                unit(ci, d, u // per_res, u % per_res)
            return c

        lax.fori_loop(0, n_units // _DIL_UNROLL, group, 0)

    mc = 256
    for c0 in range(0, span, mc):
        l1, l2, l3 = (lse_ref[ci, c0:c0 + mc, :] for ci in range(3))
        lmax = jnp.maximum(jnp.maximum(l1, l2), l3)
        e1, e2, e3 = jnp.exp(l1 - lmax), jnp.exp(l2 - lmax), jnp.exp(l3 - lmax)
        mix = (e1 * oc_ref[0, c0:c0 + mc, :] + e2 * oc_ref[1, c0:c0 + mc, :] + e3 * oc_ref[2, c0:c0 + mc, :]) \
            / (e1 + e2 + e3)
        o_ref[0, c0:c0 + mc, :] = mix.astype(BF16)


def _dilated_mixture(qkv):
    b, s, _ = qkv.shape
    span = DIL_SPAN
    assert s % span == 0 and all(w <= span and span % (DIL_N * d) == 0 for w, d in DIL_CONFIGS)
    n_pairs = N_HEADS_DIL // HEADS_PER_BLOCK
    n_cfg = len(DIL_CONFIGS)

    def cur(off):
        return lambda bi, p, i: (bi, i, off + DIL_BLK + p)

    def prev(off):
        return lambda bi, p, i: (bi, jnp.maximum(i - 1, 0), off + DIL_BLK + p)

    blk = (1, span, LANES)
    return pl.pallas_call(
        _dil_kernel,
        out_shape=jax.ShapeDtypeStruct((b, s, N_HEADS_DIL * HEAD_DIM), BF16),
        grid=(b, n_pairs, s // span),
        in_specs=[pl.BlockSpec(blk, cur(Q_BLK0)), pl.BlockSpec(blk, cur(K_BLK0)), pl.BlockSpec(blk, cur(V_BLK0)),
                  pl.BlockSpec(blk, prev(K_BLK0)), pl.BlockSpec(blk, prev(V_BLK0))],
        out_specs=pl.BlockSpec(blk, lambda bi, p, i: (bi, i, p)),
        scratch_shapes=[pltpu.VMEM((span, LANES), F32),
                        pltpu.VMEM((2 * span, LANES), F32),
                        pltpu.VMEM((2 * span, LANES), F32),
                        pltpu.VMEM((n_cfg, span, LANES), F32),
                        pltpu.VMEM((n_cfg, span, LANES), F32),
                        pltpu.VMEM((n_cfg, HEADS_PER_BLOCK, DIL_N, 2 * DIL_N), F32)],
        compiler_params=_params(3),
        name="dilated_attn",
    )(qkv, qkv, qkv, qkv, qkv)


def _rms(x, g):
    return x * lax.rsqrt(jnp.mean(x * x, axis=-1, keepdims=True) + NORM_EPS) * g


def _outproj_kernel(x_ref, oa_ref, ob_ref, oc_ref, g_ref, w_ref, out_ref):
    g = g_ref[...]
    wa = N_HEADS_MOBA * HEAD_DIM
    wb = wa + N_HEADS_SB * HEAD_DIM
    y = jnp.concatenate([_rms(oa_ref[...].astype(F32), g[:, :wa]),
                         _rms(ob_ref[...].astype(F32), g[:, wa:wb]),
                         _rms(oc_ref[...].astype(F32), g[:, wb:])], axis=1).astype(BF16)
    out_ref[...] = x_ref[...] + _dot(y, w_ref[...])


def _out_proj(x2, oa, ob, oc, g, w, *, tm=512):
    t, d = x2.shape
    row = lambda i: (i, 0)
    const = lambda i: (0, 0)
    return pl.pallas_call(
        _outproj_kernel,
        out_shape=jax.ShapeDtypeStruct((t, d), F32),
        grid=(t // tm,),
        in_specs=[pl.BlockSpec((tm, d), row),
                  pl.BlockSpec((tm, oa.shape[1]), row), pl.BlockSpec((tm, ob.shape[1]), row),
                  pl.BlockSpec((tm, oc.shape[1]), row),
                  pl.BlockSpec((1, d), const), pl.BlockSpec((d, d), const)],
        out_specs=pl.BlockSpec((tm, d), row),
        compiler_params=_params(1),
        name="out_proj",
    )(x2, oa, ob, oc, g.reshape(1, d), w)


_EXP_LANE0 = N_GROUPS
_MOE_ROWS = 256


def _moe_kernel(x_ref, g_ref, wrh_ref, wrl_ref, br_ref, wg_ref, wu_ref, wd_ref, out_ref,
                h_ref, comb_ref, acc_ref):
    e = pl.program_id(1)
    tm = x_ref.shape[0]
    lane = lax.broadcasted_iota(jnp.int32, (tm, LANES), 1)

    @pl.when(e == 0)
    def _():
        x = x_ref[...]
        h = _rms(x, g_ref[...])
        h_ref[...] = h.astype(BF16)
        hh, hl = _split2(h)
        logits = _dot(hh, wrh_ref[...]) + _dot(hh, wrl_ref[...]) + _dot(hl, wrh_ref[...]) + br_ref[...]
        lane_f = lane.astype(F32)
        big = float(LANES)
        gl = jnp.where(lane < N_GROUPS, logits, -jnp.inf)
        gmax = jnp.max(gl, axis=1, keepdims=True)
        gidx = jnp.min(jnp.where(gl == gmax, lane_f, big), axis=1, keepdims=True)
        g_w = 1.0 / jnp.sum(jnp.exp(gl - gmax), axis=1, keepdims=True)
        lane_group = ((lane - _EXP_LANE0) // EXPERTS_PER_GROUP).astype(F32)
        in_group = (lane >= _EXP_LANE0) & (lane < _EXP_LANE0 + N_EXPERTS) & (lane_group == gidx)
        el = jnp.where(in_group, logits, -jnp.inf)
        v1 = jnp.max(el, axis=1, keepdims=True)
        i1 = jnp.min(jnp.where(el == v1, lane_f, big), axis=1, keepdims=True)
        el2 = jnp.where(lane_f == i1, -jnp.inf, el)
        v2 = jnp.max(el2, axis=1, keepdims=True)
        i2 = jnp.min(jnp.where(el2 == v2, lane_f, big), axis=1, keepdims=True)
        r = jnp.exp(v2 - v1)
        w1 = g_w / (1.0 + r)
        w2 = g_w * r / (1.0 + r)
        comb_ref[...] = jnp.where(lane_f == i1, w1, 0.0) + jnp.where(lane_f == i2, w2, 0.0)
        acc_ref[...] = jnp.zeros_like(acc_ref)

    lane_c = lax.broadcasted_iota(jnp.int32, (_MOE_ROWS, LANES), 1)
    for c in range(tm // _MOE_ROWS):
        rows = slice(c * _MOE_ROWS, (c + 1) * _MOE_ROWS)
        h = h_ref[rows, :]
        comb = comb_ref[rows, :]
        acts = []
        for k in range(EXPERTS_PER_GROUP):
            gate = _dot(h, wg_ref[0, k])
            up = _dot(h, wu_ref[0, k])
            cw = jnp.sum(jnp.where(lane_c == _EXP_LANE0 + e * EXPERTS_PER_GROUP + k, comb, 0.0),
                         axis=1, keepdims=True)
            acts.append((gate / (1.0 + jnp.exp(-gate)) * up * cw).astype(BF16))
        acc_ref[rows, :] += _dot(jnp.concatenate(acts, axis=1), wd_ref[0])

    @pl.when(e == N_GROUPS - 1)
    def _():
        out_ref[...] = x_ref[...] + acc_ref[...]


def _moe(x2, g, wr_hi, wr_lo, br, wg, wu, wd, *, tm=1024):
    t, d = x2.shape
    f = wg.shape[3]
    row = lambda i, e: (i, 0)
    const = lambda i, e: (0, 0)
    return pl.pallas_call(
        _moe_kernel,
        out_shape=jax.ShapeDtypeStruct((t, d), F32),
        grid=(t // tm, N_GROUPS),
        in_specs=[pl.BlockSpec((tm, d), row),
                  pl.BlockSpec((1, d), const),
                  pl.BlockSpec((d, LANES), const), pl.BlockSpec((d, LANES), const),
                  pl.BlockSpec((1, LANES), const),
                  pl.BlockSpec((1, EXPERTS_PER_GROUP, d, f), lambda i, e: (e, 0, 0, 0)),
                  pl.BlockSpec((1, EXPERTS_PER_GROUP, d, f), lambda i, e: (e, 0, 0, 0)),
                  pl.BlockSpec((1, EXPERTS_PER_GROUP * f, d), lambda i, e: (e, 0, 0))],
        out_specs=pl.BlockSpec((tm, d), row),
        scratch_shapes=[pltpu.VMEM((tm, d), BF16), pltpu.VMEM((tm, LANES), F32), pltpu.VMEM((tm, d), F32)],
        compiler_params=_params(2),
        name="hier_moe",
    )(x2, g.reshape(1, d), wr_hi, wr_lo, br, wg, wu, wd)


def _final_norm_kernel(x_ref, g_ref, o_ref):
    o_ref[...] = _rms(x_ref[...], g_ref[...])


def _final_norm(x2, g, *, tm=1024):
    t, d = x2.shape
    return pl.pallas_call(
        _final_norm_kernel,
        out_shape=jax.ShapeDtypeStruct((t, d), F32),
        grid=(t // tm,),
        in_specs=[pl.BlockSpec((tm, d), lambda i: (i, 0)), pl.BlockSpec((1, d), lambda i: (0, 0))],
        out_specs=pl.BlockSpec((tm, d), lambda i: (i, 0)),
        compiler_params=_params(1),
        name="final_norm",
    )(x2, g.reshape(1, d))


def _router_weights(w_gr, b_gr, w_er, b_er):
    d = w_gr.shape[0]
    w = jnp.concatenate([w_gr, jnp.moveaxis(w_er, 0, 1).reshape(d, N_EXPERTS)], axis=1)
    w = jnp.pad(w, ((0, 0), (0, LANES - w.shape[1])))
    bias = jnp.pad(jnp.concatenate([b_gr, b_er.reshape(-1)]), (0, LANES - N_GROUPS - N_EXPERTS))
    hi, lo = _split2(w)
    return hi, lo, bias.reshape(1, LANES)


def _layer(x2, b, s, ln1_g, w_in, mix_g, w_out, ln2_g, w_gr, b_gr, w_er, b_er, w_gate, w_up, w_down):
    t, d = x2.shape
    qkv = _qkv_proj(x2, ln1_g, w_in.astype(BF16)).reshape(b, s, 3 * d)
    oa = _moba_attention(qkv)
    ob = _sb_attention(qkv)
    oc = _dilated_mixture(qkv)
    x2 = _out_proj(x2, oa.reshape(t, -1), ob.reshape(t, -1), oc.reshape(t, -1), mix_g, w_out.astype(BF16))
    wr_hi, wr_lo, br = _router_weights(w_gr, b_gr, w_er, b_er)
    f = w_gate.shape[-1]
    return _moe(x2, ln2_g, wr_hi, wr_lo, br, w_gate.astype(BF16), w_up.astype(BF16),
                w_down.reshape(N_GROUPS, EXPERTS_PER_GROUP * f, d).astype(BF16))


def kernel(x, ln1_g, w_in, mix_norm_g, w_out, ln2_g, w_group_router, b_group_router,
           w_expert_router, b_expert_router, w_gate, w_up, w_down, final_norm_g):
    b, s, d = x.shape
    x2 = x.reshape(b * s, d)
    for l in range(ln1_g.shape[0]):
        x2 = _layer(x2, b, s, ln1_g[l], w_in[l], mix_norm_g[l], w_out[l], ln2_g[l],
                    w_group_router[l], b_group_router[l], w_expert_router[l], b_expert_router[l],
                    w_gate[l], w_up[l], w_down[l])
    return _final_norm(x2, final_norm_g).reshape(b, s, d)
```

```python
import functools

import jax
import jax.numpy as jnp
from jax import lax
from jax.experimental import pallas as pl
from jax.experimental.pallas import tpu as pltpu

F32 = jnp.float32
BF16 = jnp.bfloat16

D_MODEL = 1024
HEAD_DIM = 64
N_HEADS = 16
LANES = 128
HEADS_PER_BLOCK = LANES // HEAD_DIM
N_HEADS_MOBA = 4
N_HEADS_SB = 4
N_HEADS_DIL = 8
MOBA_BLOCK = 256
MOBA_TOPK = 3
DIL_CONFIGS = ((128, 1), (512, 4), (2048, 16))
DIL_N = 128
N_GROUPS = 4
EXPERTS_PER_GROUP = 4
N_EXPERTS = N_GROUPS * EXPERTS_PER_GROUP
D_EXPERT = 256
NORM_EPS = 1e-6
NEG_INF = -1e30
SCALE = HEAD_DIM ** -0.5

Q_BLK0 = 0
K_BLK0 = D_MODEL // LANES
V_BLK0 = 2 * D_MODEL // LANES
ROW_BLKS = 3 * D_MODEL // LANES
SB_BLK = N_HEADS_MOBA // HEADS_PER_BLOCK
DIL_BLK = (N_HEADS_MOBA + N_HEADS_SB) // HEADS_PER_BLOCK

VMEM_LIMIT = 56 * 1024 * 1024


def _params(n_axes, vmem=VMEM_LIMIT):
    return pltpu.CompilerParams(dimension_semantics=("arbitrary",) * n_axes,
                                vmem_limit_bytes=vmem)


def _dot_nt(a, b):
    return lax.dot_general(a, b, (((1,), (1,)), ((), ())), preferred_element_type=F32)


def _dot(a, b):
    return jnp.dot(a, b, preferred_element_type=F32)


def _split3(x):
    hi = x.astype(BF16)
    r1 = x - hi.astype(F32)
    mid = r1.astype(BF16)
    lo = (r1 - mid.astype(F32)).astype(BF16)
    return hi, mid, lo


def _split2(x):
    hi = x.astype(BF16)
    lo = (x - hi.astype(F32)).astype(BF16)
    return hi, lo


def _qkv_kernel(x_ref, g_ref, w_ref, o_ref, *, rows, tn):
    for c in range(x_ref.shape[0] // rows):
        r = slice(c * rows, (c + 1) * rows)
        x = x_ref[r, :]
        ms = jnp.mean(x * x, axis=-1, keepdims=True)
        h = (x * lax.rsqrt(ms + NORM_EPS) * g_ref[...]).astype(BF16)
        for j in range(w_ref.shape[1] // tn):
            o_ref[r, j * tn:(j + 1) * tn] = _dot(h, w_ref[:, j * tn:(j + 1) * tn]).astype(BF16)


def _qkv_proj(x2, g, w, *, tm=512, rows=256, tn=1024):
    t, d = x2.shape
    n = w.shape[1]
    return pl.pallas_call(
        functools.partial(_qkv_kernel, rows=rows, tn=tn),
        out_shape=jax.ShapeDtypeStruct((t, n), BF16),
        grid=(t // tm,),
        in_specs=[pl.BlockSpec((tm, d), lambda i: (i, 0)),
                  pl.BlockSpec((1, d), lambda i: (0, 0)),
                  pl.BlockSpec((d, n), lambda i: (0, 0))],
        out_specs=pl.BlockSpec((tm, n), lambda i: (i, 0)),
        compiler_params=_params(1),
        name="qkv_proj",
    )(x2, g.reshape(1, d), w)


_MB_SEL0 = 0
_MB_POS0 = 32
_MB_KILL = 40
_MB_BLK0 = 64


_ROW_CHUNK = 32


MOBA_DEAD_GAP = 110.0


def _moba_kernel(q_ref, k_ref, v_ref, o_ref, kmean_ref, kn2_ref, kx_ref, qaug_ref, m_ref, alpha_ref, acc_ref,
                 s_ref, p_ref, *, n_blk):
    pair = pl.program_id(1)
    qi = pl.program_id(2)
    blk = MOBA_BLOCK
    lane = lax.broadcasted_iota(jnp.int32, (blk, LANES), 1)
    row = lax.broadcasted_iota(jnp.int32, (blk, LANES), 0)
    lane_f = lane.astype(F32)
    slopes = [jnp.where(pair == 0, 2.0 ** (-2 * (hh + 1)), 2.0 ** (-2 * (hh + 3))).astype(F32)
              for hh in range(HEADS_PER_BLOCK)]

    @pl.when(qi == 0)
    def _():
        kmean_ref[...] = jnp.zeros_like(kmean_ref)
        kn2_ref[...] = jnp.zeros_like(kn2_ref)

        def body(n, c):
            kb = k_ref[0, pl.ds(pl.multiple_of(n * blk, blk), blk), :].astype(F32)
            kmean_ref[pl.ds(n, 1), :] = jnp.sum(kb, axis=0, keepdims=True) * (1.0 / blk)
            for hh in range(HEADS_PER_BLOCK):
                in_head = (lane >= hh * HEAD_DIM) & (lane < (hh + 1) * HEAD_DIM)
                n2 = jnp.sum(jnp.where(in_head, kb * kb, 0.0), axis=1, keepdims=True)
                n2 = jnp.max(jnp.broadcast_to(n2, (blk, LANES)), axis=0, keepdims=True)
                kn2_ref[hh] = jnp.maximum(kn2_ref[hh], jnp.broadcast_to(n2, (8, LANES)))
            return c

        lax.fori_loop(0, n_blk, body, 0)
        kx = jnp.zeros((blk, LANES), F32)
        for hh in range(HEADS_PER_BLOCK):
            kx = jnp.where(lane == _MB_POS0 + 2 * hh, slopes[hh] * ((row // LANES) * LANES).astype(F32), kx)
            kx = jnp.where(lane == _MB_POS0 + 2 * hh + 1, slopes[hh] * (row % LANES).astype(F32), kx)
        kx_ref[...] = kx.astype(BF16)

    q2 = q_ref[0]
    km_parts = _split3(kmean_ref[...])
    for hh in range(HEADS_PER_BLOCK):
        in_head = (lane >= hh * HEAD_DIM) & (lane < (hh + 1) * HEAD_DIM)
        qm = jnp.where(in_head, q2, jnp.zeros_like(q2))
        gate = _dot_nt(qm, km_parts[0]) + _dot_nt(qm, km_parts[1]) + _dot_nt(qm, km_parts[2])
        gate = jnp.where(lane < qi, gate, NEG_INF)
        gate = jnp.where(lane < n_blk, gate, -jnp.inf)
        sel = jnp.zeros((blk, LANES), jnp.bool_)
        for _ in range(MOBA_TOPK):
            gmax = jnp.max(gate, axis=1, keepdims=True)
            first = jnp.min(jnp.where(gate == gmax, lane_f, float(LANES)), axis=1, keepdims=True)
            pick = lane_f == first
            sel = sel | pick
            gate = jnp.where(pick, -jnp.inf, gate)
        sel = sel & (lane < qi)
        extra = jnp.where(sel, 0.0, NEG_INF)
        extra = jnp.where(lane >= n_blk, 0.0, extra)
        extra = jnp.where((lane == _MB_POS0 + 2 * hh) | (lane == _MB_POS0 + 2 * hh + 1) | (lane == _MB_KILL),
                          1.0, extra)
        blk_lane = lane - (_MB_BLK0 + 32 * hh)
        extra = jnp.where((blk_lane >= 0) & (blk_lane < 32),
                          slopes[hh] * (blk_lane * blk).astype(F32), extra)
        qaug_ref[hh] = jnp.concatenate([qm * jnp.asarray(SCALE, BF16), extra.astype(BF16)], axis=1)
        m_ref[hh] = jnp.full((blk, LANES), -jnp.inf, F32)
        acc_ref[hh] = jnp.zeros((blk, LANES), F32)

    lane1 = lax.broadcasted_iota(jnp.int32, (1, LANES), 1)
    rc = _ROW_CHUNK

    def score_stage(j, buf, *, is_own=False, valid=True):
        kj = k_ref[0, pl.ds(pl.multiple_of(j * blk, blk), blk), :]
        ind = (lane1 % 32 == j) & (lane1 >= _MB_BLK0) if is_own else \
              (lane1 % 32 == j) & ((lane1 < 32) | (lane1 >= _MB_BLK0))
        kill = jnp.where(lane1 == _MB_KILL, jnp.where(valid, 0.0, NEG_INF), 0.0)
        side = jnp.broadcast_to(jnp.where(ind, 1.0, kill), (blk, LANES)).astype(BF16)
        kx = jnp.where(ind | (lane1 == _MB_KILL), side, kx_ref[...])
        k_aug = jnp.concatenate([kj, kx], axis=1)
        s = _dot_nt(qaug_ref[...].reshape(HEADS_PER_BLOCK * blk, 2 * LANES), k_aug)
        s_ref[buf] = s.reshape(HEADS_PER_BLOCK, blk, blk)

    def softmax_stage(buf, *, is_own=False):
        for hh in range(HEADS_PER_BLOCK):
            for c in range(blk // rc):
                rows = slice(c * rc, (c + 1) * rc)
                s = s_ref[buf, hh, rows, :]
                if is_own:
                    col_c = lax.broadcasted_iota(jnp.int32, (rc, blk), 1)
                    row_c = lax.broadcasted_iota(jnp.int32, (rc, blk), 0) + c * rc
                    s = jnp.where(col_c <= row_c, s, NEG_INF)
                m_old = m_ref[hh, rows, :]
                m_new = jnp.maximum(m_old, jnp.max(s, axis=1, keepdims=True))
                alpha_ref[buf, hh, rows, :] = jnp.exp(m_old - m_new)
                m_ref[hh, rows, :] = m_new
                p = jnp.exp(s - jnp.concatenate([m_new, m_new], axis=1))
                p_ref[buf, hh, rows, :] = p.astype(BF16)

    def value_stage(j, buf):
        vj = v_ref[0, pl.ds(pl.multiple_of(j * blk, blk), blk), :]
        v_aug = jnp.concatenate(
            [jnp.where((lane1 >= hh * HEAD_DIM) & (lane1 < (hh + 1) * HEAD_DIM), vj, jnp.ones_like(vj))
             for hh in range(HEADS_PER_BLOCK)], axis=1)
        pv = _dot(p_ref[buf].reshape(HEADS_PER_BLOCK * blk, blk), v_aug)
        for hh in range(HEADS_PER_BLOCK):
            acc_ref[hh] = alpha_ref[buf, hh] * acc_ref[hh] + pv[hh * blk:(hh + 1) * blk, hh * LANES:(hh + 1) * LANES]

    n_past = qi

    @pl.when(n_past == 0)
    def _():
        score_stage(qi, 0, is_own=True)
        softmax_stage(0, is_own=True)
        value_stage(qi, 0)

    @pl.when(n_past > 0)
    def _():
        last = n_past - 1

        def past(t):
            return jnp.clip(last - t, 0, last)

        score_stage(qi, 1, is_own=True)
        score_stage(past(0), 0)
        softmax_stage(1, is_own=True)
        value_stage(qi, 1)
        softmax_stage(0)
        score_stage(past(1), 1, valid=1 < n_past)

        reach = jnp.zeros((1, LANES), F32)
        for hh in range(HEADS_PER_BLOCK):
            in_head = (lane >= hh * HEAD_DIM) & (lane < (hh + 1) * HEAD_DIM)
            qf = q2.astype(F32)
            qn2 = jnp.sum(jnp.where(in_head, qf * qf, 0.0), axis=1, keepdims=True)
            qn2 = jnp.max(jnp.broadcast_to(qn2, (blk, LANES)), axis=0, keepdims=True)
            bound = jnp.sqrt(qn2 * kn2_ref[hh, 0:1, :]) * SCALE
            m_min = jnp.min(m_ref[hh], axis=0, keepdims=True)
            reach = jnp.maximum(reach, (bound - m_min + MOBA_DEAD_GAP) / slopes[hh])
        n_past_f = jnp.full((1, LANES), n_past, jnp.int32).astype(F32)
        n_live = jnp.max(jnp.minimum((reach - 1.0) / blk + 1.0, n_past_f))

        def pair(state):
            tt, tf = state
            t = 2 * tt
            score_stage(past(t), 0, valid=t < n_past)
            softmax_stage(1)
            value_stage(past(t - 2), 0)
            score_stage(past(t + 1), 1, valid=t + 1 < n_past)
            softmax_stage(0)
            value_stage(past(t - 1), 1)
            return tt + 1, tf + 2.0

        lax.while_loop(lambda st: st[1] < n_live + 2.0, pair, (jnp.int32(1), jnp.float32(2.0)))

    acc0 = acc_ref[0]
    acc1 = acc_ref[1]
    o0 = acc0 / pltpu.roll(acc0, HEAD_DIM, axis=1)
    o1 = acc1 / pltpu.roll(acc1, HEAD_DIM, axis=1)
    o_ref[0] = jnp.where(lane < HEAD_DIM, o0, o1).astype(BF16)


def _moba_attention(qkv):
    b, s, _ = qkv.shape
    blk = MOBA_BLOCK
    n_blk = s // blk
    assert s % blk == 0 and MOBA_TOPK <= n_blk - 1 and n_blk <= 32
    n_pairs = N_HEADS_MOBA // HEADS_PER_BLOCK
    return pl.pallas_call(
        functools.partial(_moba_kernel, n_blk=n_blk),
        out_shape=jax.ShapeDtypeStruct((b, s, N_HEADS_MOBA * HEAD_DIM), BF16),
        grid=(b, n_pairs, n_blk),
        in_specs=[pl.BlockSpec((1, blk, LANES), lambda bi, p, i: (bi, i, Q_BLK0 + p)),
                  pl.BlockSpec((1, s, LANES), lambda bi, p, i: (bi, 0, K_BLK0 + p)),
                  pl.BlockSpec((1, s, LANES), lambda bi, p, i: (bi, 0, V_BLK0 + p))],
        out_specs=pl.BlockSpec((1, blk, LANES), lambda bi, p, i: (bi, i, p)),
        scratch_shapes=[pltpu.VMEM((LANES, LANES), F32),
                        pltpu.VMEM((HEADS_PER_BLOCK, 8, LANES), F32),
                        pltpu.VMEM((blk, LANES), BF16),
                        pltpu.VMEM((HEADS_PER_BLOCK, blk, 2 * LANES), BF16),
                        pltpu.VMEM((HEADS_PER_BLOCK, blk, LANES), F32),
                        pltpu.VMEM((2, HEADS_PER_BLOCK, blk, LANES), F32),
                        pltpu.VMEM((HEADS_PER_BLOCK, blk, LANES), F32),
                        pltpu.VMEM((2, HEADS_PER_BLOCK, blk, blk), F32),
                        pltpu.VMEM((2, HEADS_PER_BLOCK, blk, blk), BF16)],
        compiler_params=_params(3),
        name="moba_attn",
    )(qkv, qkv, qkv)


SB_TILE = 256


def _softplus(z):
    return jnp.maximum(z, 0.0) + jnp.log(1.0 + jnp.exp(-jnp.abs(z)))


_SB_SLOTS = 3
SB_DEAD_MASS = 128.0


def _sb_kernel(q_ref, k_ref, v_ref, u_ref, o_ref, qaug_ref, carry_ref, acc_ref,
               z_ref, sphl_ref, c_ref, rs_ref, a_ref):
    qi = pl.program_id(2)
    t = SB_TILE
    rc = _ROW_CHUNK
    lane = lax.broadcasted_iota(jnp.int32, (t, LANES), 1)
    lane1 = lax.broadcasted_iota(jnp.int32, (1, LANES), 1)
    q2 = q_ref[0]
    for hh in range(HEADS_PER_BLOCK):
        in_head = (lane >= hh * HEAD_DIM) & (lane < (hh + 1) * HEAD_DIM)
        qm = jnp.where(in_head, q2, jnp.zeros_like(q2)) * jnp.asarray(SCALE, BF16)
        qaug_ref[hh] = jnp.concatenate([qm, jnp.where(lane == 0, 1.0, 0.0).astype(BF16)], axis=1)

    def score_stage(j, slot, *, valid=True):
        kj = k_ref[0, pl.ds(pl.multiple_of(j * t, t), t), :]
        kill = jnp.where(lane1 == 0, jnp.where(valid, 0.0, NEG_INF), 0.0)
        k_aug = jnp.concatenate([kj, jnp.broadcast_to(kill, (t, LANES)).astype(BF16)], axis=1)
        z = _dot_nt(qaug_ref[...].reshape(HEADS_PER_BLOCK * t, 2 * LANES), k_aug)
        z_ref[slot] = z.reshape(HEADS_PER_BLOCK, t, t)

    def _past_mask(c):
        col_c = lax.broadcasted_iota(jnp.int32, (rc, t), 1)
        row_c = lax.broadcasted_iota(jnp.int32, (rc, t), 0) + c * rc
        return col_c < row_c

    def softplus_stage(slot, *, diagonal=False):
        for hh in range(HEADS_PER_BLOCK):
            for c in range(t // rc):
                rows = slice(c * rc, (c + 1) * rc)
                sp = _softplus(z_ref[slot, hh, rows, :])
                if diagonal:
                    sp = jnp.where(_past_mask(c), sp, 0.0)
                hi, lo = _split2(sp)
                sphl_ref[slot, pl.ds((2 * hh) * t + c * rc, rc), :] = hi
                sphl_ref[slot, pl.ds((2 * hh + 1) * t + c * rc, rc), :] = lo
                rs_ref[slot, hh, rows, :] = jnp.broadcast_to(jnp.sum(sp, axis=1, keepdims=True), (rc, LANES))

    def suffix_stage(slot):
        c_ref[slot] = _dot(sphl_ref[slot], u_ref[...])

    def weight_stage(slot, *, diagonal=False):
        for hh in range(HEADS_PER_BLOCK):
            for c in range(t // rc):
                rows = slice(c * rc, (c + 1) * rc)
                z = z_ref[slot, hh, rows, :]
                cc = c_ref[slot, pl.ds((2 * hh) * t + c * rc, rc), :] + \
                    c_ref[slot, pl.ds((2 * hh + 1) * t + c * rc, rc), :]
                if diagonal:
                    a = jnp.where(_past_mask(c), jnp.exp(z - cc), 0.0)
                    carry_ref[hh, rows, :] = rs_ref[slot, hh, rows, :]
                else:
                    carry = carry_ref[hh, rows, :]
                    a = jnp.exp(z - (cc + jnp.concatenate([carry, carry], axis=1)))
                    carry_ref[hh, rows, :] = carry + rs_ref[slot, hh, rows, :]
                a_ref[slot, hh, rows, :] = a.astype(BF16)

    def value_stage(j, slot, *, first=False):
        vj = v_ref[0, pl.ds(pl.multiple_of(j * t, t), t), :]
        av = _dot(a_ref[slot].reshape(HEADS_PER_BLOCK * t, t), vj)
        for hh in range(HEADS_PER_BLOCK):
            av_h = av[hh * t:(hh + 1) * t, :]
            acc_ref[hh] = av_h if first else acc_ref[hh] + av_h

    n_past = qi

    @pl.when(n_past == 0)
    def _():
        score_stage(qi, 0)
        softplus_stage(0, diagonal=True)
        suffix_stage(0)
        weight_stage(0, diagonal=True)
        value_stage(qi, 0, first=True)

    @pl.when(n_past > 0)
    def _():
        last = n_past - 1

        def key_tile(i):
            return jnp.clip(last - i, 0, last)

        def trip(tt):
            for k in range(_SB_SLOTS):
                i = _SB_SLOTS * tt + k
                score_stage(key_tile(i), k, valid=i < n_past)
                softplus_stage((k - 1) % _SB_SLOTS)
                suffix_stage((k - 1) % _SB_SLOTS)
                weight_stage((k - 2) % _SB_SLOTS)
                value_stage(key_tile(i - 2), (k - 2) % _SB_SLOTS)
            return tt + 1, jnp.min(carry_ref[...])

        score_stage(qi, 2)
        score_stage(key_tile(0), 0)
        softplus_stage(2, diagonal=True)
        suffix_stage(2)
        score_stage(key_tile(1), 1, valid=1 < n_past)
        softplus_stage(0)
        suffix_stage(0)
        weight_stage(2, diagonal=True)
        value_stage(qi, 2, first=True)
        score_stage(key_tile(2), 2, valid=2 < n_past)
        softplus_stage(1)
        suffix_stage(1)
        weight_stage(0)
        value_stage(key_tile(0), 0)

        n_trips = (n_past + 2 + _SB_SLOTS - 1) // _SB_SLOTS
        lax.while_loop(lambda st: (st[0] < n_trips) & (st[1] < SB_DEAD_MASS), lambda st: trip(st[0]),
                       (jnp.int32(1), jnp.min(carry_ref[...])))

    o_ref[0] = jnp.where(lane < HEAD_DIM, acc_ref[0], acc_ref[1]).astype(BF16)


def _sb_attention(qkv):
    b, s, _ = qkv.shape
    t = SB_TILE
    assert s % t == 0
    n_pairs = N_HEADS_SB // HEADS_PER_BLOCK
    u = (lax.broadcasted_iota(jnp.int32, (t, t), 0) >= lax.broadcasted_iota(jnp.int32, (t, t), 1)).astype(BF16)
    return pl.pallas_call(
        _sb_kernel,
        out_shape=jax.ShapeDtypeStruct((b, s, N_HEADS_SB * HEAD_DIM), BF16),
        grid=(b, n_pairs, s // t),
        in_specs=[pl.BlockSpec((1, t, LANES), lambda bi, p, i: (bi, i, Q_BLK0 + SB_BLK + p)),
                  pl.BlockSpec((1, s, LANES), lambda bi, p, i: (bi, 0, K_BLK0 + SB_BLK + p)),
                  pl.BlockSpec((1, s, LANES), lambda bi, p, i: (bi, 0, V_BLK0 + SB_BLK + p)),
                  pl.BlockSpec((t, t), lambda bi, p, i: (0, 0))],
        out_specs=pl.BlockSpec((1, t, LANES), lambda bi, p, i: (bi, i, p)),
        scratch_shapes=[pltpu.VMEM((HEADS_PER_BLOCK, t, 2 * LANES), BF16),
                        pltpu.VMEM((HEADS_PER_BLOCK, t, LANES), F32),
                        pltpu.VMEM((HEADS_PER_BLOCK, t, LANES), F32),
                        pltpu.VMEM((_SB_SLOTS, HEADS_PER_BLOCK, t, t), F32),
                        pltpu.VMEM((_SB_SLOTS, 2 * HEADS_PER_BLOCK * t, t), BF16),
                        pltpu.VMEM((_SB_SLOTS, 2 * HEADS_PER_BLOCK * t, t), F32),
                        pltpu.VMEM((_SB_SLOTS, HEADS_PER_BLOCK, t, LANES), F32),
                        pltpu.VMEM((_SB_SLOTS, HEADS_PER_BLOCK, t, t), BF16)],
        compiler_params=_params(3),
        name="sb_attn",
    )(qkv, qkv, qkv, u)


DIL_SPAN = 2048
_DIL_UNROLL = 8


def _dil_kernel(q_ref, k_ref, v_ref, kp_ref, vp_ref, o_ref, qf_ref, kf_ref, vf_ref, oc_ref, lse_ref, bias_ref):
    pair = pl.program_id(1)
    i = pl.program_id(2)
    n = DIL_N
    span = DIL_SPAN
    lane = lax.broadcasted_iota(jnp.int32, (n, LANES), 1)
    colh = lax.broadcasted_iota(jnp.int32, (n, 2 * n), 1)

    @pl.when(i == 0)
    def _():
        row = lax.broadcasted_iota(jnp.int32, (n, 2 * n), 0)
        delta = row + n - colh
        valid = (delta >= 0) & (delta <= n)
        for ci, (_, d) in enumerate(DIL_CONFIGS):
            for hh in range(HEADS_PER_BLOCK):
                slope = jnp.asarray(2.0 ** -(hh + 1), F32)
                for p in range(1, N_HEADS_DIL // HEADS_PER_BLOCK):
                    slope = jnp.where(pair == p, 2.0 ** -(2 * p + hh + 1), slope)
                bias_ref[ci, hh] = jnp.where(valid, -slope * (delta * d).astype(F32), NEG_INF)

    qf_ref[...] = q_ref[0].astype(F32)
    kf_ref[0:span, :] = kp_ref[0].astype(F32)
    kf_ref[span:2 * span, :] = k_ref[0].astype(F32)
    vf_ref[0:span, :] = vp_ref[0].astype(F32)
    vf_ref[span:2 * span, :] = v_ref[0].astype(F32)

    def rows(start, size, d):
        return pl.ds(start, size) if d == 1 else pl.ds(start, size, stride=d)

    def unit(ci, d, r, bl):
        q0 = r + bl * (n * d)
        qu = qf_ref[rows(q0, n, d), :].astype(BF16)
        ku = kf_ref[rows(span + q0 - n * d, 2 * n, d), :].astype(BF16)
        vu = vf_ref[rows(span + q0 - n * d, 2 * n, d), :].astype(BF16)
        no_prev = jnp.where((i == 0) & (bl == 0), NEG_INF, 0.0)
        qms = [jnp.where((lane >= hh * HEAD_DIM) & (lane < (hh + 1) * HEAD_DIM), qu, jnp.zeros_like(qu))
               * jnp.asarray(SCALE, BF16) for hh in range(HEADS_PER_BLOCK)]
        s_both = _dot_nt(jnp.concatenate(qms, axis=0), ku)
        ps, dens, lse_h = [], [], []
        for hh in range(HEADS_PER_BLOCK):
            s = s_both[hh * n:(hh + 1) * n, :] + bias_ref[ci, hh] + jnp.where(colh < n, no_prev, 0.0)
            m = jnp.max(s, axis=1, keepdims=True)
            p = jnp.exp(s - m)
            den = jnp.sum(p, axis=1, keepdims=True)
            ps.append(p.astype(BF16))
            dens.append(den)
            lse_h.append(jnp.broadcast_to(m + jnp.log(den), (n, LANES)))
        pv = _dot(jnp.concatenate(ps, axis=0), vu)
        o_h = [pv[hh * n:(hh + 1) * n, :] / dens[hh] for hh in range(HEADS_PER_BLOCK)]
        oc_ref[ci, rows(q0, n, d), :] = jnp.where(lane < HEAD_DIM, o_h[0], o_h[1])
        lse_ref[ci, rows(q0, n, d), :] = jnp.where(lane < HEAD_DIM, lse_h[0], lse_h[1])

    n_units = span // n
    for ci, (_, d) in enumerate(DIL_CONFIGS):
        per_res = n_units // d

        def group(g, c, ci=ci, d=d, per_res=per_res):
            for k in range(_DIL_UNROLL):
                u = g * _DIL_UNROLL + k
                unit(ci, d, u // per_res, u % per_res)
            return c

        lax.fori_loop(0, n_units // _DIL_UNROLL, group, 0)

    mc = 256
    for c0 in range(0, span, mc):
        l1, l2, l3 = (lse_ref[ci, c0:c0 + mc, :] for ci in range(3))
        lmax = jnp.maximum(jnp.maximum(l1, l2), l3)
        e1, e2, e3 = jnp.exp(l1 - lmax), jnp.exp(l2 - lmax), jnp.exp(l3 - lmax)
        mix = (e1 * oc_ref[0, c0:c0 + mc, :] + e2 * oc_ref[1, c0:c0 + mc, :] + e3 * oc_ref[2, c0:c0 + mc, :]) \
            / (e1 + e2 + e3)
        o_ref[0, c0:c0 + mc, :] = mix.astype(BF16)


def _dilated_mixture(qkv):
    b, s, _ = qkv.shape
    span = DIL_SPAN
    assert s % span == 0 and all(w <= span and span % (DIL_N * d) == 0 for w, d in DIL_CONFIGS)
    n_pairs = N_HEADS_DIL // HEADS_PER_BLOCK
    n_cfg = len(DIL_CONFIGS)

    def cur(off):
        return lambda bi, p, i: (bi, i, off + DIL_BLK + p)

    def prev(off):
        return lambda bi, p, i: (bi, jnp.maximum(i - 1, 0), off + DIL_BLK + p)

    blk = (1, span, LANES)
    return pl.pallas_call(
        _dil_kernel,
        out_shape=jax.ShapeDtypeStruct((b, s, N_HEADS_DIL * HEAD_DIM), BF16),
        grid=(b, n_pairs, s // span),
        in_specs=[pl.BlockSpec(blk, cur(Q_BLK0)), pl.BlockSpec(blk, cur(K_BLK0)), pl.BlockSpec(blk, cur(V_BLK0)),
                  pl.BlockSpec(blk, prev(K_BLK0)), pl.BlockSpec(blk, prev(V_BLK0))],
        out_specs=pl.BlockSpec(blk, lambda bi, p, i: (bi, i, p)),
        scratch_shapes=[pltpu.VMEM((span, LANES), F32),
                        pltpu.VMEM((2 * span, LANES), F32),
                        pltpu.VMEM((2 * span, LANES), F32),
                        pltpu.VMEM((n_cfg, span, LANES), F32),
                        pltpu.VMEM((n_cfg, span, LANES), F32),
                        pltpu.VMEM((n_cfg, HEADS_PER_BLOCK, DIL_N, 2 * DIL_N), F32)],
        compiler_params=_params(3),
        name="dilated_attn",
    )(qkv, qkv, qkv, qkv, qkv)


def _rms(x, g):
    return x * lax.rsqrt(jnp.mean(x * x, axis=-1, keepdims=True) + NORM_EPS) * g


def _outproj_kernel(x_ref, oa_ref, ob_ref, oc_ref, g_ref, w_ref, out_ref):
    g = g_ref[...]
    wa = N_HEADS_MOBA * HEAD_DIM
    wb = wa + N_HEADS_SB * HEAD_DIM
    y = jnp.concatenate([_rms(oa_ref[...].astype(F32), g[:, :wa]),
                         _rms(ob_ref[...].astype(F32), g[:, wa:wb]),
                         _rms(oc_ref[...].astype(F32), g[:, wb:])], axis=1).astype(BF16)
    out_ref[...] = x_ref[...] + _dot(y, w_ref[...])


def _out_proj(x2, oa, ob, oc, g, w, *, tm=512):
    t, d = x2.shape
    row = lambda i: (i, 0)
    const = lambda i: (0, 0)
    return pl.pallas_call(
        _outproj_kernel,
        out_shape=jax.ShapeDtypeStruct((t, d), F32),
        grid=(t // tm,),
        in_specs=[pl.BlockSpec((tm, d), row),
                  pl.BlockSpec((tm, oa.shape[1]), row), pl.BlockSpec((tm, ob.shape[1]), row),
                  pl.BlockSpec((tm, oc.shape[1]), row),
                  pl.BlockSpec((1, d), const), pl.BlockSpec((d, d), const)],
        out_specs=pl.BlockSpec((tm, d), row),
        compiler_params=_params(1),
        name="out_proj",
    )(x2, oa, ob, oc, g.reshape(1, d), w)


_EXP_LANE0 = N_GROUPS
_MOE_ROWS = 256


def _moe_kernel(x_ref, g_ref, wrh_ref, wrl_ref, br_ref, wg_ref, wu_ref, wd_ref, out_ref,
                h_ref, comb_ref, acc_ref):
    e = pl.program_id(1)
    tm = x_ref.shape[0]
    lane = lax.broadcasted_iota(jnp.int32, (tm, LANES), 1)

    @pl.when(e == 0)
    def _():
        x = x_ref[...]
        h = _rms(x, g_ref[...])
        h_ref[...] = h.astype(BF16)
        hh, hl = _split2(h)
        logits = _dot(hh, wrh_ref[...]) + _dot(hh, wrl_ref[...]) + _dot(hl, wrh_ref[...]) + br_ref[...]
        lane_f = lane.astype(F32)
        big = float(LANES)
        gl = jnp.where(lane < N_GROUPS, logits, -jnp.inf)
        gmax = jnp.max(gl, axis=1, keepdims=True)
        gidx = jnp.min(jnp.where(gl == gmax, lane_f, big), axis=1, keepdims=True)
        g_w = 1.0 / jnp.sum(jnp.exp(gl - gmax), axis=1, keepdims=True)
        lane_group = ((lane - _EXP_LANE0) // EXPERTS_PER_GROUP).astype(F32)
        in_group = (lane >= _EXP_LANE0) & (lane < _EXP_LANE0 + N_EXPERTS) & (lane_group == gidx)
        el = jnp.where(in_group, logits, -jnp.inf)
        v1 = jnp.max(el, axis=1, keepdims=True)
        i1 = jnp.min(jnp.where(el == v1, lane_f, big), axis=1, keepdims=True)
        el2 = jnp.where(lane_f == i1, -jnp.inf, el)
        v2 = jnp.max(el2, axis=1, keepdims=True)
        i2 = jnp.min(jnp.where(el2 == v2, lane_f, big), axis=1, keepdims=True)
        r = jnp.exp(v2 - v1)
        w1 = g_w / (1.0 + r)
        w2 = g_w * r / (1.0 + r)
        comb_ref[...] = jnp.where(lane_f == i1, w1, 0.0) + jnp.where(lane_f == i2, w2, 0.0)
        acc_ref[...] = jnp.zeros_like(acc_ref)

    lane_c = lax.broadcasted_iota(jnp.int32, (_MOE_ROWS, LANES), 1)
    for c in range(tm // _MOE_ROWS):
        rows = slice(c * _MOE_ROWS, (c + 1) * _MOE_ROWS)
        h = h_ref[rows, :]
        comb = comb_ref[rows, :]
        acts = []
        for k in range(EXPERTS_PER_GROUP):
            gate = _dot(h, wg_ref[0, k])
            up = _dot(h, wu_ref[0, k])
            cw = jnp.sum(jnp.where(lane_c == _EXP_LANE0 + e * EXPERTS_PER_GROUP + k, comb, 0.0),
                         axis=1, keepdims=True)
            acts.append((gate / (1.0 + jnp.exp(-gate)) * up * cw).astype(BF16))
        acc_ref[rows, :] += _dot(jnp.concatenate(acts, axis=1), wd_ref[0])

    @pl.when(e == N_GROUPS - 1)
    def _():
        out_ref[...] = x_ref[...] + acc_ref[...]


def _moe(x2, g, wr_hi, wr_lo, br, wg, wu, wd, *, tm=1024):
    t, d = x2.shape
    f = wg.shape[3]
    row = lambda i, e: (i, 0)
    const = lambda i, e: (0, 0)
    return pl.pallas_call(
        _moe_kernel,
        out_shape=jax.ShapeDtypeStruct((t, d), F32),
        grid=(t // tm, N_GROUPS),
        in_specs=[pl.BlockSpec((tm, d), row),
                  pl.BlockSpec((1, d), const),
                  pl.BlockSpec((d, LANES), const), pl.BlockSpec((d, LANES), const),
                  pl.BlockSpec((1, LANES), const),
                  pl.BlockSpec((1, EXPERTS_PER_GROUP, d, f), lambda i, e: (e, 0, 0, 0)),
                  pl.BlockSpec((1, EXPERTS_PER_GROUP, d, f), lambda i, e: (e, 0, 0, 0)),
                  pl.BlockSpec((1, EXPERTS_PER_GROUP * f, d), lambda i, e: (e, 0, 0))],
        out_specs=pl.BlockSpec((tm, d), row),
        scratch_shapes=[pltpu.VMEM((tm, d), BF16), pltpu.VMEM((tm, LANES), F32), pltpu.VMEM((tm, d), F32)],
        compiler_params=_params(2),
        name="hier_moe",
    )(x2, g.reshape(1, d), wr_hi, wr_lo, br, wg, wu, wd)


def _final_norm_kernel(x_ref, g_ref, o_ref):
    o_ref[...] = _rms(x_ref[...], g_ref[...])


def _final_norm(x2, g, *, tm=1024):
    t, d = x2.shape
    return pl.pallas_call(
        _final_norm_kernel,
        out_shape=jax.ShapeDtypeStruct((t, d), F32),
        grid=(t // tm,),
        in_specs=[pl.BlockSpec((tm, d), lambda i: (i, 0)), pl.BlockSpec((1, d), lambda i: (0, 0))],
        out_specs=pl.BlockSpec((tm, d), lambda i: (i, 0)),
        compiler_params=_params(1),
        name="final_norm",
    )(x2, g.reshape(1, d))


def _router_weights(w_gr, b_gr, w_er, b_er):
    d = w_gr.shape[0]
    w = jnp.concatenate([w_gr, jnp.moveaxis(w_er, 0, 1).reshape(d, N_EXPERTS)], axis=1)
    w = jnp.pad(w, ((0, 0), (0, LANES - w.shape[1])))
    bias = jnp.pad(jnp.concatenate([b_gr, b_er.reshape(-1)]), (0, LANES - N_GROUPS - N_EXPERTS))
    hi, lo = _split2(w)
    return hi, lo, bias.reshape(1, LANES)


def _layer(x2, b, s, ln1_g, w_in, mix_g, w_out, ln2_g, w_gr, b_gr, w_er, b_er, w_gate, w_up, w_down):
    t, d = x2.shape
    qkv = _qkv_proj(x2, ln1_g, w_in.astype(BF16)).reshape(b, s, 3 * d)
    oa = _moba_attention(qkv)
    ob = _sb_attention(qkv)
    oc = _dilated_mixture(qkv)
    x2 = _out_proj(x2, oa.reshape(t, -1), ob.reshape(t, -1), oc.reshape(t, -1), mix_g, w_out.astype(BF16))
    wr_hi, wr_lo, br = _router_weights(w_gr, b_gr, w_er, b_er)
    f = w_gate.shape[-1]
    return _moe(x2, ln2_g, wr_hi, wr_lo, br, w_gate.astype(BF16), w_up.astype(BF16),
                w_down.reshape(N_GROUPS, EXPERTS_PER_GROUP * f, d).astype(BF16))


def kernel(x, ln1_g, w_in, mix_norm_g, w_out, ln2_g, w_group_router, b_group_router,
           w_expert_router, b_expert_router, w_gate, w_up, w_down, final_norm_g):
    b, s, d = x.shape
    x2 = x.reshape(b * s, d)
    for l in range(ln1_g.shape[0]):
        x2 = _layer(x2, b, s, ln1_g[l], w_in[l], mix_norm_g[l], w_out[l], ln2_g[l],
                    w_group_router[l], b_group_router[l], w_expert_router[l], b_expert_router[l],
                    w_gate[l], w_up[l], w_down[l])
    return _final_norm(x2, final_norm_g).reshape(b, s, d)
```

```python
import functools

import jax
import jax.numpy as jnp
from jax import lax
from jax.experimental import pallas as pl
from jax.experimental.pallas import tpu as pltpu

F32 = jnp.float32
BF16 = jnp.bfloat16

D_MODEL = 1024
HEAD_DIM = 64
N_HEADS = 16
LANES = 128
HEADS_PER_BLOCK = LANES // HEAD_DIM
N_HEADS_MOBA = 4
N_HEADS_SB = 4
N_HEADS_DIL = 8
MOBA_BLOCK = 256
MOBA_TOPK = 3
DIL_CONFIGS = ((128, 1), (512, 4), (2048, 16))
DIL_N = 128
N_GROUPS = 4
EXPERTS_PER_GROUP = 4
N_EXPERTS = N_GROUPS * EXPERTS_PER_GROUP
D_EXPERT = 256
NORM_EPS = 1e-6
NEG_INF = -1e30
SCALE = HEAD_DIM ** -0.5

Q_BLK0 = 0
K_BLK0 = D_MODEL // LANES
V_BLK0 = 2 * D_MODEL // LANES
ROW_BLKS = 3 * D_MODEL // LANES
SB_BLK = N_HEADS_MOBA // HEADS_PER_BLOCK
DIL_BLK = (N_HEADS_MOBA + N_HEADS_SB) // HEADS_PER_BLOCK

VMEM_LIMIT = 56 * 1024 * 1024


def _params(n_axes, vmem=VMEM_LIMIT):
    return pltpu.CompilerParams(dimension_semantics=("arbitrary",) * n_axes,
                                vmem_limit_bytes=vmem)


def _dot_nt(a, b):
    return lax.dot_general(a, b, (((1,), (1,)), ((), ())), preferred_element_type=F32)


def _dot(a, b):
    return jnp.dot(a, b, preferred_element_type=F32)


def _split3(x):
    hi = x.astype(BF16)
    r1 = x - hi.astype(F32)
    mid = r1.astype(BF16)
    lo = (r1 - mid.astype(F32)).astype(BF16)
    return hi, mid, lo


def _split2(x):
    hi = x.astype(BF16)
    lo = (x - hi.astype(F32)).astype(BF16)
    return hi, lo


def _qkv_kernel(x_ref, g_ref, w_ref, o_ref, *, rows, tn):
    for c in range(x_ref.shape[0] // rows):
        r = slice(c * rows, (c + 1) * rows)
        x = x_ref[r, :]
        ms = jnp.mean(x * x, axis=-1, keepdims=True)
        h = (x * lax.rsqrt(ms + NORM_EPS) * g_ref[...]).astype(BF16)
        for j in range(w_ref.shape[1] // tn):
            o_ref[r, j * tn:(j + 1) * tn] = _dot(h, w_ref[:, j * tn:(j + 1) * tn]).astype(BF16)


def _qkv_proj(x2, g, w, *, tm=512, rows=256, tn=1024):
    t, d = x2.shape
    n = w.shape[1]
    return pl.pallas_call(
        functools.partial(_qkv_kernel, rows=rows, tn=tn),
        out_shape=jax.ShapeDtypeStruct((t, n), BF16),
        grid=(t // tm,),
        in_specs=[pl.BlockSpec((tm, d), lambda i: (i, 0)),
                  pl.BlockSpec((1, d), lambda i: (0, 0)),
                  pl.BlockSpec((d, n), lambda i: (0, 0))],
        out_specs=pl.BlockSpec((tm, n), lambda i: (i, 0)),
        compiler_params=_params(1),
        name="qkv_proj",
    )(x2, g.reshape(1, d), w)


_MB_SEL0 = 0
_MB_POS0 = 32
_MB_KILL = 40
_MB_BLK0 = 64


_ROW_CHUNK = 32


MOBA_DEAD_GAP = 110.0


def _moba_kernel(q_ref, k_ref, v_ref, o_ref, kmean_ref, kn2_ref, kx_ref, qaug_ref, m_ref, alpha_ref, acc_ref,
                 s_ref, p_ref, *, n_blk):
    pair = pl.program_id(1)
    qi = pl.program_id(2)
    blk = MOBA_BLOCK
    lane = lax.broadcasted_iota(jnp.int32, (blk, LANES), 1)
    row = lax.broadcasted_iota(jnp.int32, (blk, LANES), 0)
    lane_f = lane.astype(F32)
    slopes = [jnp.where(pair == 0, 2.0 ** (-2 * (hh + 1)), 2.0 ** (-2 * (hh + 3))).astype(F32)
              for hh in range(HEADS_PER_BLOCK)]

    @pl.when(qi == 0)
    def _():
        kmean_ref[...] = jnp.zeros_like(kmean_ref)
        kn2_ref[...] = jnp.zeros_like(kn2_ref)

        def body(n, c):
            kb = k_ref[0, pl.ds(pl.multiple_of(n * blk, blk), blk), :].astype(F32)
            kmean_ref[pl.ds(n, 1), :] = jnp.sum(kb, axis=0, keepdims=True) * (1.0 / blk)
            for hh in range(HEADS_PER_BLOCK):
                in_head = (lane >= hh * HEAD_DIM) & (lane < (hh + 1) * HEAD_DIM)
                n2 = jnp.sum(jnp.where(in_head, kb * kb, 0.0), axis=1, keepdims=True)
                n2 = jnp.max(jnp.broadcast_to(n2, (blk, LANES)), axis=0, keepdims=True)
                kn2_ref[hh] = jnp.maximum(kn2_ref[hh], jnp.broadcast_to(n2, (8, LANES)))
            return c

        lax.fori_loop(0, n_blk, body, 0)
        kx = jnp.zeros((blk, LANES), F32)
        for hh in range(HEADS_PER_BLOCK):
            kx = jnp.where(lane == _MB_POS0 + 2 * hh, slopes[hh] * ((row // LANES) * LANES).astype(F32), kx)
            kx = jnp.where(lane == _MB_POS0 + 2 * hh + 1, slopes[hh] * (row % LANES).astype(F32), kx)
        kx_ref[...] = kx.astype(BF16)

    q2 = q_ref[0]
    km_parts = _split3(kmean_ref[...])
    for hh in range(HEADS_PER_BLOCK):
        in_head = (lane >= hh * HEAD_DIM) & (lane < (hh + 1) * HEAD_DIM)
        qm = jnp.where(in_head, q2, jnp.zeros_like(q2))
        gate = _dot_nt(qm, km_parts[0]) + _dot_nt(qm, km_parts[1]) + _dot_nt(qm, km_parts[2])
        gate = jnp.where(lane < qi, gate, NEG_INF)
        gate = jnp.where(lane < n_blk, gate, -jnp.inf)
        sel = jnp.zeros((blk, LANES), jnp.bool_)
        for _ in range(MOBA_TOPK):
            gmax = jnp.max(gate, axis=1, keepdims=True)
            first = jnp.min(jnp.where(gate == gmax, lane_f, float(LANES)), axis=1, keepdims=True)
            pick = lane_f == first
            sel = sel | pick
            gate = jnp.where(pick, -jnp.inf, gate)
        sel = sel & (lane < qi)
        extra = jnp.where(sel, 0.0, NEG_INF)
        extra = jnp.where(lane >= n_blk, 0.0, extra)
        extra = jnp.where((lane == _MB_POS0 + 2 * hh) | (lane == _MB_POS0 + 2 * hh + 1) | (lane == _MB_KILL),
                          1.0, extra)
        blk_lane = lane - (_MB_BLK0 + 32 * hh)
        extra = jnp.where((blk_lane >= 0) & (blk_lane < 32),
                          slopes[hh] * (blk_lane * blk).astype(F32), extra)
        qaug_ref[hh] = jnp.concatenate([qm * jnp.asarray(SCALE, BF16), extra.astype(BF16)], axis=1)
        m_ref[hh] = jnp.full((blk, LANES), -jnp.inf, F32)
        acc_ref[hh] = jnp.zeros((blk, LANES), F32)

    lane1 = lax.broadcasted_iota(jnp.int32, (1, LANES), 1)
    rc = _ROW_CHUNK

    def score_stage(j, buf, *, is_own=False, valid=True):
        kj = k_ref[0, pl.ds(pl.multiple_of(j * blk, blk), blk), :]
        ind = (lane1 % 32 == j) & (lane1 >= _MB_BLK0) if is_own else \
              (lane1 % 32 == j) & ((lane1 < 32) | (lane1 >= _MB_BLK0))
        kill = jnp.where(lane1 == _MB_KILL, jnp.where(valid, 0.0, NEG_INF), 0.0)
        side = jnp.broadcast_to(jnp.where(ind, 1.0, kill), (blk, LANES)).astype(BF16)
        kx = jnp.where(ind | (lane1 == _MB_KILL), side, kx_ref[...])
        k_aug = jnp.concatenate([kj, kx], axis=1)
        s = _dot_nt(qaug_ref[...].reshape(HEADS_PER_BLOCK * blk, 2 * LANES), k_aug)
        s_ref[buf] = s.reshape(HEADS_PER_BLOCK, blk, blk)

    def softmax_stage(buf, *, is_own=False):
        for hh in range(HEADS_PER_BLOCK):
            for c in range(blk // rc):
                rows = slice(c * rc, (c + 1) * rc)
                s = s_ref[buf, hh, rows, :]
                if is_own:
                    col_c = lax.broadcasted_iota(jnp.int32, (rc, blk), 1)
                    row_c = lax.broadcasted_iota(jnp.int32, (rc, blk), 0) + c * rc
                    s = jnp.where(col_c <= row_c, s, NEG_INF)
                m_old = m_ref[hh, rows, :]
                m_new = jnp.maximum(m_old, jnp.max(s, axis=1, keepdims=True))
                alpha_ref[buf, hh, rows, :] = jnp.exp(m_old - m_new)
                m_ref[hh, rows, :] = m_new
                p = jnp.exp(s - jnp.concatenate([m_new, m_new], axis=1))
                p_ref[buf, hh, rows, :] = p.astype(BF16)

    def value_stage(j, buf):
        vj = v_ref[0, pl.ds(pl.multiple_of(j * blk, blk), blk), :]
        v_aug = jnp.concatenate(
            [jnp.where((lane1 >= hh * HEAD_DIM) & (lane1 < (hh + 1) * HEAD_DIM), vj, jnp.ones_like(vj))
             for hh in range(HEADS_PER_BLOCK)], axis=1)
        pv = _dot(p_ref[buf].reshape(HEADS_PER_BLOCK * blk, blk), v_aug)
        for hh in range(HEADS_PER_BLOCK):
            acc_ref[hh] = alpha_ref[buf, hh] * acc_ref[hh] + pv[hh * blk:(hh + 1) * blk, hh * LANES:(hh + 1) * LANES]

    n_past = qi

    @pl.when(n_past == 0)
    def _():
        score_stage(qi, 0, is_own=True)
        softmax_stage(0, is_own=True)
        value_stage(qi, 0)

    @pl.when(n_past > 0)
    def _():
        last = n_past - 1

        def past(t):
            return jnp.clip(last - t, 0, last)

        score_stage(qi, 1, is_own=True)
        score_stage(past(0), 0)
        softmax_stage(1, is_own=True)
        value_stage(qi, 1)
        softmax_stage(0)
        score_stage(past(1), 1, valid=1 < n_past)

        reach = jnp.zeros((1, LANES), F32)
        for hh in range(HEADS_PER_BLOCK):
            in_head = (lane >= hh * HEAD_DIM) & (lane < (hh + 1) * HEAD_DIM)
            qf = q2.astype(F32)
            qn2 = jnp.sum(jnp.where(in_head, qf * qf, 0.0), axis=1, keepdims=True)
            qn2 = jnp.max(jnp.broadcast_to(qn2, (blk, LANES)), axis=0, keepdims=True)
            bound = jnp.sqrt(qn2 * kn2_ref[hh, 0:1, :]) * SCALE
            m_min = jnp.min(m_ref[hh], axis=0, keepdims=True)
            reach = jnp.maximum(reach, (bound - m_min + MOBA_DEAD_GAP) / slopes[hh])
        n_past_f = jnp.full((1, LANES), n_past, jnp.int32).astype(F32)
        n_live = jnp.max(jnp.minimum((reach - 1.0) / blk + 1.0, n_past_f))

        def pair(state):
            tt, tf = state
            t = 2 * tt
            score_stage(past(t), 0, valid=t < n_past)
            softmax_stage(1)
            value_stage(past(t - 2), 0)
            score_stage(past(t + 1), 1, valid=t + 1 < n_past)
            softmax_stage(0)
            value_stage(past(t - 1), 1)
            return tt + 1, tf + 2.0

        lax.while_loop(lambda st: st[1] < n_live + 2.0, pair, (jnp.int32(1), jnp.float32(2.0)))

    acc0 = acc_ref[0]
    acc1 = acc_ref[1]
    o0 = acc0 / pltpu.roll(acc0, HEAD_DIM, axis=1)
    o1 = acc1 / pltpu.roll(acc1, HEAD_DIM, axis=1)
    o_ref[0] = jnp.where(lane < HEAD_DIM, o0, o1).astype(BF16)


def _moba_attention(qkv):
    b, s, _ = qkv.shape
    blk = MOBA_BLOCK
    n_blk = s // blk
    assert s % blk == 0 and MOBA_TOPK <= n_blk - 1 and n_blk <= 32
    n_pairs = N_HEADS_MOBA // HEADS_PER_BLOCK
    return pl.pallas_call(
        functools.partial(_moba_kernel, n_blk=n_blk),
        out_shape=jax.ShapeDtypeStruct((b, s, N_HEADS_MOBA * HEAD_DIM), BF16),
        grid=(b, n_pairs, n_blk),
        in_specs=[pl.BlockSpec((1, blk, LANES), lambda bi, p, i: (bi, i, Q_BLK0 + p)),
                  pl.BlockSpec((1, s, LANES), lambda bi, p, i: (bi, 0, K_BLK0 + p)),
                  pl.BlockSpec((1, s, LANES), lambda bi, p, i: (bi, 0, V_BLK0 + p))],
        out_specs=pl.BlockSpec((1, blk, LANES), lambda bi, p, i: (bi, i, p)),
        scratch_shapes=[pltpu.VMEM((LANES, LANES), F32),
                        pltpu.VMEM((HEADS_PER_BLOCK, 8, LANES), F32),
                        pltpu.VMEM((blk, LANES), BF16),
                        pltpu.VMEM((HEADS_PER_BLOCK, blk, 2 * LANES), BF16),
                        pltpu.VMEM((HEADS_PER_BLOCK, blk, LANES), F32),
                        pltpu.VMEM((2, HEADS_PER_BLOCK, blk, LANES), F32),
                        pltpu.VMEM((HEADS_PER_BLOCK, blk, LANES), F32),
                        pltpu.VMEM((2, HEADS_PER_BLOCK, blk, blk), F32),
                        pltpu.VMEM((2, HEADS_PER_BLOCK, blk, blk), BF16)],
        compiler_params=_params(3),
        name="moba_attn",
    )(qkv, qkv, qkv)


SB_TILE = 256


def _softplus(z):
    return jnp.maximum(z, 0.0) + jnp.log(1.0 + jnp.exp(-jnp.abs(z)))


_SB_SLOTS = 3
SB_DEAD_MASS = 128.0


def _sb_kernel(q_ref, k_ref, v_ref, u_ref, o_ref, qaug_ref, carry_ref, acc_ref,
               z_ref, sphl_ref, c_ref, rs_ref, a_ref):
    qi = pl.program_id(2)
    t = SB_TILE
    rc = _ROW_CHUNK
    lane = lax.broadcasted_iota(jnp.int32, (t, LANES), 1)
    lane1 = lax.broadcasted_iota(jnp.int32, (1, LANES), 1)
    q2 = q_ref[0]
    for hh in range(HEADS_PER_BLOCK):
        in_head = (lane >= hh * HEAD_DIM) & (lane < (hh + 1) * HEAD_DIM)
        qm = jnp.where(in_head, q2, jnp.zeros_like(q2)) * jnp.asarray(SCALE, BF16)
        qaug_ref[hh] = jnp.concatenate([qm, jnp.where(lane == 0, 1.0, 0.0).astype(BF16)], axis=1)

    def score_stage(j, slot, *, valid=True):
        kj = k_ref[0, pl.ds(pl.multiple_of(j * t, t), t), :]
        kill = jnp.where(lane1 == 0, jnp.where(valid, 0.0, NEG_INF), 0.0)
        k_aug = jnp.concatenate([kj, jnp.broadcast_to(kill, (t, LANES)).astype(BF16)], axis=1)
        z = _dot_nt(qaug_ref[...].reshape(HEADS_PER_BLOCK * t, 2 * LANES), k_aug)
        z_ref[slot] = z.reshape(HEADS_PER_BLOCK, t, t)

    def _past_mask(c):
        col_c = lax.broadcasted_iota(jnp.int32, (rc, t), 1)
        row_c = lax.broadcasted_iota(jnp.int32, (rc, t), 0) + c * rc
        return col_c < row_c

    def softplus_stage(slot, *, diagonal=False):
        for hh in range(HEADS_PER_BLOCK):
            for c in range(t // rc):
                rows = slice(c * rc, (c + 1) * rc)
                sp = _softplus(z_ref[slot, hh, rows, :])
                if diagonal:
                    sp = jnp.where(_past_mask(c), sp, 0.0)
                hi, lo = _split2(sp)
                sphl_ref[slot, pl.ds((2 * hh) * t + c * rc, rc), :] = hi
                sphl_ref[slot, pl.ds((2 * hh + 1) * t + c * rc, rc), :] = lo
                rs_ref[slot, hh, rows, :] = jnp.broadcast_to(jnp.sum(sp, axis=1, keepdims=True), (rc, LANES))

    def suffix_stage(slot):
        c_ref[slot] = _dot(sphl_ref[slot], u_ref[...])

    def weight_stage(slot, *, diagonal=False):
        for hh in range(HEADS_PER_BLOCK):
            for c in range(t // rc):
                rows = slice(c * rc, (c + 1) * rc)
                z = z_ref[slot, hh, rows, :]
                cc = c_ref[slot, pl.ds((2 * hh) * t + c * rc, rc), :] + \
                    c_ref[slot, pl.ds((2 * hh + 1) * t + c * rc, rc), :]
                if diagonal:
                    a = jnp.where(_past_mask(c), jnp.exp(z - cc), 0.0)
                    carry_ref[hh, rows, :] = rs_ref[slot, hh, rows, :]
                else:
                    carry = carry_ref[hh, rows, :]
                    a = jnp.exp(z - (cc + jnp.concatenate([carry, carry], axis=1)))
                    carry_ref[hh, rows, :] = carry + rs_ref[slot, hh, rows, :]
                a_ref[slot, hh, rows, :] = a.astype(BF16)

    def value_stage(j, slot, *, first=False):
        vj = v_ref[0, pl.ds(pl.multiple_of(j * t, t), t), :]
        av = _dot(a_ref[slot].reshape(HEADS_PER_BLOCK * t, t), vj)
        for hh in range(HEADS_PER_BLOCK):
            av_h = av[hh * t:(hh + 1) * t, :]
            acc_ref[hh] = av_h if first else acc_ref[hh] + av_h

    n_past = qi

    @pl.when(n_past == 0)
    def _():
        score_stage(qi, 0)
        softplus_stage(0, diagonal=True)
        suffix_stage(0)
        weight_stage(0, diagonal=True)
        value_stage(qi, 0, first=True)

    @pl.when(n_past > 0)
    def _():
        last = n_past - 1

        def key_tile(i):
            return jnp.clip(last - i, 0, last)

        def trip(tt):
            for k in range(_SB_SLOTS):
                i = _SB_SLOTS * tt + k
                score_stage(key_tile(i), k, valid=i < n_past)
                softplus_stage((k - 1) % _SB_SLOTS)
                suffix_stage((k - 1) % _SB_SLOTS)
                weight_stage((k - 2) % _SB_SLOTS)
                value_stage(key_tile(i - 2), (k - 2) % _SB_SLOTS)
            return tt + 1, jnp.min(carry_ref[...])

        score_stage(qi, 2)
        score_stage(key_tile(0), 0)
        softplus_stage(2, diagonal=True)
        suffix_stage(2)
        score_stage(key_tile(1), 1, valid=1 < n_past)
        softplus_stage(0)
        suffix_stage(0)
        weight_stage(2, diagonal=True)
        value_stage(qi, 2, first=True)
        score_stage(key_tile(2), 2, valid=2 < n_past)
        softplus_stage(1)
        suffix_stage(1)
        weight_stage(0)
        value_stage(key_tile(0), 0)

        n_trips = (n_past + 2 + _SB_SLOTS - 1) // _SB_SLOTS
        lax.while_loop(lambda st: (st[0] < n_trips) & (st[1] < SB_DEAD_MASS), lambda st: trip(st[0]),
                       (jnp.int32(1), jnp.min(carry_ref[...])))

    o_ref[0] = jnp.where(lane < HEAD_DIM, acc_ref[0], acc_ref[1]).astype(BF16)


def _sb_attention(qkv):
    b, s, _ = qkv.shape
    t = SB_TILE
    assert s % t == 0
    n_pairs = N_HEADS_SB // HEADS_PER_BLOCK
    u = (lax.broadcasted_iota(jnp.int32, (t, t), 0) >= lax.broadcasted_iota(jnp.int32, (t, t), 1)).astype(BF16)
    return pl.pallas_call(
        _sb_kernel,
        out_shape=jax.ShapeDtypeStruct((b, s, N_HEADS_SB * HEAD_DIM), BF16),
        grid=(b, n_pairs, s // t),
        in_specs=[pl.BlockSpec((1, t, LANES), lambda bi, p, i: (bi, i, Q_BLK0 + SB_BLK + p)),
                  pl.BlockSpec((1, s, LANES), lambda bi, p, i: (bi, 0, K_BLK0 + SB_BLK + p)),
                  pl.BlockSpec((1, s, LANES), lambda bi, p, i: (bi, 0, V_BLK0 + SB_BLK + p)),
                  pl.BlockSpec((t, t), lambda bi, p, i: (0, 0))],
        out_specs=pl.BlockSpec((1, t, LANES), lambda bi, p, i: (bi, i, p)),
        scratch_shapes=[pltpu.VMEM((HEADS_PER_BLOCK, t, 2 * LANES), BF16),
                        pltpu.VMEM((HEADS_PER_BLOCK, t, LANES), F32),
                        pltpu.VMEM((HEADS_PER_BLOCK, t, LANES), F32),
                        pltpu.VMEM((_SB_SLOTS, HEADS_PER_BLOCK, t, t), F32),
                        pltpu.VMEM((_SB_SLOTS, 2 * HEADS_PER_BLOCK * t, t), BF16),
                        pltpu.VMEM((_SB_SLOTS, 2 * HEADS_PER_BLOCK * t, t), F32),
                        pltpu.VMEM((_SB_SLOTS, HEADS_PER_BLOCK, t, LANES), F32),
                        pltpu.VMEM((_SB_SLOTS, HEADS_PER_BLOCK, t, t), BF16)],
        compiler_params=_params(3),
        name="sb_attn",
    )(qkv, qkv, qkv, u)


DIL_SPAN = 2048
_DIL_UNROLL = 8


def _dil_kernel(q_ref, k_ref, v_ref, kp_ref, vp_ref, o_ref, qf_ref, kf_ref, vf_ref, oc_ref, lse_ref, bias_ref):
    pair = pl.program_id(1)
    i = pl.program_id(2)
    n = DIL_N
    span = DIL_SPAN
    lane = lax.broadcasted_iota(jnp.int32, (n, LANES), 1)
    colh = lax.broadcasted_iota(jnp.int32, (n, 2 * n), 1)

    @pl.when(i == 0)
    def _():
        row = lax.broadcasted_iota(jnp.int32, (n, 2 * n), 0)
        delta = row + n - colh
        valid = (delta >= 0) & (delta <= n)
        for ci, (_, d) in enumerate(DIL_CONFIGS):
            for hh in range(HEADS_PER_BLOCK):
                slope = jnp.asarray(2.0 ** -(hh + 1), F32)
                for p in range(1, N_HEADS_DIL // HEADS_PER_BLOCK):
                    slope = jnp.where(pair == p, 2.0 ** -(2 * p + hh + 1), slope)
                bias_ref[ci, hh] = jnp.where(valid, -slope * (delta * d).astype(F32), NEG_INF)

    qf_ref[...] = q_ref[0].astype(F32)
    kf_ref[0:span, :] = kp_ref[0].astype(F32)
    kf_ref[span:2 * span, :] = k_ref[0].astype(F32)
    vf_ref[0:span, :] = vp_ref[0].astype(F32)
    vf_ref[span:2 * span, :] = v_ref[0].astype(F32)

    def rows(start, size, d):
        return pl.ds(start, size) if d == 1 else pl.ds(start, size, stride=d)

    def unit(ci, d, r, bl):
        q0 = r + bl * (n * d)
        qu = qf_ref[rows(q0, n, d), :].astype(BF16)
        ku = kf_ref[rows(span + q0 - n * d, 2 * n, d), :].astype(BF16)
        vu = vf_ref[rows(span + q0 - n * d, 2 * n, d), :].astype(BF16)
        no_prev = jnp.where((i == 0) & (bl == 0), NEG_INF, 0.0)
        qms = [jnp.where((lane >= hh * HEAD_DIM) & (lane < (hh + 1) * HEAD_DIM), qu, jnp.zeros_like(qu))
               * jnp.asarray(SCALE, BF16) for hh in range(HEADS_PER_BLOCK)]
        s_both = _dot_nt(jnp.concatenate(qms, axis=0), ku)
        ps, dens, lse_h = [], [], []
        for hh in range(HEADS_PER_BLOCK):
            s = s_both[hh * n:(hh + 1) * n, :] + bias_ref[ci, hh] + jnp.where(colh < n, no_prev, 0.0)
            m = jnp.max(s, axis=1, keepdims=True)
            p = jnp.exp(s - m)
            den = jnp.sum(p, axis=1, keepdims=True)
            ps.append(p.astype(BF16))
            dens.append(den)
            lse_h.append(jnp.broadcast_to(m + jnp.log(den), (n, LANES)))
        pv = _dot(jnp.concatenate(ps, axis=0), vu)
        o_h = [pv[hh * n:(hh + 1) * n, :] / dens[hh] for hh in range(HEADS_PER_BLOCK)]
        oc_ref[ci, rows(q0, n, d), :] = jnp.where(lane < HEAD_DIM, o_h[0], o_h[1])
        lse_ref[ci, rows(q0, n, d), :] = jnp.where(lane < HEAD_DIM, lse_h[0], lse_h[1])

    n_units = span // n
    for ci, (_, d) in enumerate(DIL_CONFIGS):
        per_res = n_units // d

        def group(g, c, ci=ci, d=d, per_res=per_res):
            for k in range(_DIL_UNROLL):
                u = g * _DIL_UNROLL + k
                unit(ci, d, u // per_res, u % per_res)
            return c

        lax.fori_loop(0, n_units // _DIL_UNROLL, group, 0)

    mc = 256
    for c0 in range(0, span, mc):
        l1, l2, l3 = (lse_ref[ci, c0:c0 + mc, :] for ci in range(3))
        lmax = jnp.maximum(jnp.maximum(l1, l2), l3)
        e1, e2, e3 = jnp.exp(l1 - lmax), jnp.exp(l2 - lmax), jnp.exp(l3 - lmax)
        mix = (e1 * oc_ref[0, c0:c0 + mc, :] + e2 * oc_ref[1, c0:c0 + mc, :] + e3 * oc_ref[2, c0:c0 + mc, :]) \
            / (e1 + e2 + e3)
        o_ref[0, c0:c0 + mc, :] = mix.astype(BF16)


def _dilated_mixture(qkv):
    b, s, _ = qkv.shape
    span = DIL_SPAN
    assert s % span == 0 and all(w <= span and span % (DIL_N * d) == 0 for w, d in DIL_CONFIGS)
    n_pairs = N_HEADS_DIL // HEADS_PER_BLOCK
    n_cfg = len(DIL_CONFIGS)

    def cur(off):
        return lambda bi, p, i: (bi, i, off + DIL_BLK + p)

    def prev(off):
        return lambda bi, p, i: (bi, jnp.maximum(i - 1, 0), off + DIL_BLK + p)

    blk = (1, span, LANES)
    return pl.pallas_call(
        _dil_kernel,
        out_shape=jax.ShapeDtypeStruct((b, s, N_HEADS_DIL * HEAD_DIM), BF16),
        grid=(b, n_pairs, s // span),
        in_specs=[pl.BlockSpec(blk, cur(Q_BLK0)), pl.BlockSpec(blk, cur(K_BLK0)), pl.BlockSpec(blk, cur(V_BLK0)),
                  pl.BlockSpec(blk, prev(K_BLK0)), pl.BlockSpec(blk, prev(V_BLK0))],
        out_specs=pl.BlockSpec(blk, lambda bi, p, i: (bi, i, p)),
        scratch_shapes=[pltpu.VMEM((span, LANES), F32),
                        pltpu.VMEM((2 * span, LANES), F32),
                        pltpu.VMEM((2 * span, LANES), F32),
                        pltpu.VMEM((n_cfg, span, LANES), F32),
                        pltpu.VMEM((n_cfg, span, LANES), F32),
                        pltpu.VMEM((n_cfg, HEADS_PER_BLOCK, DIL_N, 2 * DIL_N), F32)],
        compiler_params=_params(3),
        name="dilated_attn",
    )(qkv, qkv, qkv, qkv, qkv)


def _rms(x, g):
    return x * lax.rsqrt(jnp.mean(x * x, axis=-1, keepdims=True) + NORM_EPS) * g


def _outproj_kernel(x_ref, oa_ref, ob_ref, oc_ref, g_ref, w_ref, out_ref):
    g = g_ref[...]
    wa = N_HEADS_MOBA * HEAD_DIM
    wb = wa + N_HEADS_SB * HEAD_DIM
    y = jnp.concatenate([_rms(oa_ref[...].astype(F32), g[:, :wa]),
                         _rms(ob_ref[...].astype(F32), g[:, wa:wb]),
                         _rms(oc_ref[...].astype(F32), g[:, wb:])], axis=1).astype(BF16)
    out_ref[...] = x_ref[...] + _dot(y, w_ref[...])


def _out_proj(x2, oa, ob, oc, g, w, *, tm=512):
    t, d = x2.shape
    row = lambda i: (i, 0)
    const = lambda i: (0, 0)
    return pl.pallas_call(
        _outproj_kernel,
        out_shape=jax.ShapeDtypeStruct((t, d), F32),
        grid=(t // tm,),
        in_specs=[pl.BlockSpec((tm, d), row),
                  pl.BlockSpec((tm, oa.shape[1]), row), pl.BlockSpec((tm, ob.shape[1]), row),
                  pl.BlockSpec((tm, oc.shape[1]), row),
                  pl.BlockSpec((1, d), const), pl.BlockSpec((d, d), const)],
        out_specs=pl.BlockSpec((tm, d), row),
        compiler_params=_params(1),
        name="out_proj",
    )(x2, oa, ob, oc, g.reshape(1, d), w)


_EXP_LANE0 = N_GROUPS
_MOE_ROWS = 256
_MOE_CHUNK = 128


def _moe_kernel(x_ref, g_ref, wrh_ref, wrl_ref, br_ref, ltri_ref, wg_ref, wu_ref, wd_ref, out_ref,
                h_ref, hs_ref, cws_ref, ys_ref, pt_ref, pos_ref, nck_ref, cb_ref, *, n_sorted):
    e = pl.program_id(1)
    tm = x_ref.shape[0]
    lane = lax.broadcasted_iota(jnp.int32, (tm, LANES), 1)
    lane1 = lax.broadcasted_iota(jnp.int32, (1, LANES), 1)

    @pl.when(e == 0)
    def _():
        x = x_ref[...]
        h = _rms(x, g_ref[...])
        hh, hl = _split2(h)
        logits = _dot(hh, wrh_ref[...]) + _dot(hh, wrl_ref[...]) + _dot(hl, wrh_ref[...]) + br_ref[...]
        lane_f = lane.astype(F32)
        big = float(LANES)
        gl = jnp.where(lane < N_GROUPS, logits, -jnp.inf)
        gmax = jnp.max(gl, axis=1, keepdims=True)
        gidx = jnp.min(jnp.where(gl == gmax, lane_f, big), axis=1, keepdims=True)
        g_w = 1.0 / jnp.sum(jnp.exp(gl - gmax), axis=1, keepdims=True)
        lane_group = ((lane - _EXP_LANE0) // EXPERTS_PER_GROUP).astype(F32)
        in_group = (lane >= _EXP_LANE0) & (lane < _EXP_LANE0 + N_EXPERTS) & (lane_group == gidx)
        el = jnp.where(in_group, logits, -jnp.inf)
        v1 = jnp.max(el, axis=1, keepdims=True)
        i1 = jnp.min(jnp.where(el == v1, lane_f, big), axis=1, keepdims=True)
        el2 = jnp.where(lane_f == i1, -jnp.inf, el)
        v2 = jnp.max(el2, axis=1, keepdims=True)
        i2 = jnp.min(jnp.where(el2 == v2, lane_f, big), axis=1, keepdims=True)
        r = jnp.exp(v2 - v1)
        w1 = g_w / (1.0 + r)
        w2 = g_w * r / (1.0 + r)
        comb = jnp.where(lane_f == i1, w1, 0.0) + jnp.where(lane_f == i2, w2, 0.0)

        onehot = jnp.where((lane_f == gidx) & (lane < N_GROUPS), 1.0, 0.0)
        before = _dot(ltri_ref[...], onehot.astype(BF16))
        rank = jnp.sum(onehot * before, axis=1, keepdims=True)
        count = jnp.sum(onehot, axis=0, keepdims=True)
        chunks = jnp.floor((count + (_MOE_CHUNK - 1.0)) * (1.0 / _MOE_CHUNK))
        nck_ref[...] = jnp.broadcast_to(chunks, nck_ref.shape)
        pos = rank
        start = jnp.float32(0.0)
        for gi in range(N_GROUPS):
            pos = pos + jnp.where(gidx == float(gi), start, 0.0)
            start = start + jnp.max(jnp.where(lane1 == gi, chunks, 0.0)) * _MOE_CHUNK
        pos_rep = jnp.broadcast_to(pos, (tm, LANES))
        pos_ref[...] = pos_rep
        pos_t = jnp.transpose(pos_rep)[0:1, :]
        h_ref[...] = hh
        c_hi, c_lo = _split2(comb)
        for c in range(n_sorted // _MOE_ROWS):
            rows = slice(c * _MOE_ROWS, (c + 1) * _MOE_ROWS)
            slot = (lax.broadcasted_iota(jnp.int32, (_MOE_ROWS, tm), 0) + c * _MOE_ROWS).astype(F32)
            perm = jnp.where(slot == pos_t, 1.0, 0.0).astype(BF16)
            hs_ref[rows, :] = _dot(perm, h_ref[...]).astype(BF16)
            cws_ref[rows, :] = _dot(perm, c_hi) + _dot(perm, c_lo)
        ys_ref[...] = jnp.zeros_like(ys_ref)
        cb_ref[0] = 0

    lane_c = lax.broadcasted_iota(jnp.int32, (_MOE_CHUNK, LANES), 1)
    n_chunks = jnp.max(jnp.where(lane1 == e, nck_ref[0:1, :], 0.0))

    def chunk(state):
        cb, cf = state
        rows = pl.ds(pl.multiple_of(cb * _MOE_CHUNK, _MOE_CHUNK), _MOE_CHUNK)
        hc = hs_ref[rows, :]
        cwc = cws_ref[rows, :]
        acts = []
        for k in range(EXPERTS_PER_GROUP):
            gate = _dot(hc, wg_ref[0, k])
            up = _dot(hc, wu_ref[0, k])
            cw = jnp.sum(jnp.where(lane_c == _EXP_LANE0 + e * EXPERTS_PER_GROUP + k, cwc, 0.0),
                         axis=1, keepdims=True)
            acts.append((gate / (1.0 + jnp.exp(-gate)) * up * cw).astype(BF16))
        ys_ref[rows, :] = _dot(jnp.concatenate(acts, axis=1), wd_ref[0]).astype(BF16)
        return cb + 1, cf + 1.0

    cb_end, _ = lax.while_loop(lambda st: st[1] < n_chunks, chunk, (cb_ref[0], jnp.float32(0.0)))
    cb_ref[0] = cb_end

    @pl.when(e == N_GROUPS - 1)
    def _():
        pos_rep2 = jnp.concatenate([pos_ref[...], pos_ref[...]], axis=1)
        for c in range(n_sorted // _MOE_ROWS):
            slot = (lax.broadcasted_iota(jnp.int32, (tm, _MOE_ROWS), 1) + c * _MOE_ROWS).astype(F32)
            perm_t = jnp.where(slot == pos_rep2, 1.0, 0.0).astype(BF16)
            pt_ref[:, c * _MOE_ROWS:(c + 1) * _MOE_ROWS] = perm_t
        out_ref[...] = x_ref[...] + _dot(pt_ref[...], ys_ref[...])


def _moe(x2, g, wr_hi, wr_lo, br, wg, wu, wd, *, tm=1024):
    t, d = x2.shape
    f = wg.shape[3]
    n_sorted = tm + N_GROUPS * _MOE_CHUNK
    assert n_sorted % _MOE_ROWS == 0
    ltri = (lax.broadcasted_iota(jnp.int32, (tm, tm), 0) > lax.broadcasted_iota(jnp.int32, (tm, tm), 1)).astype(BF16)
    row = lambda i, e: (i, 0)
    const = lambda i, e: (0, 0)
    return pl.pallas_call(
        functools.partial(_moe_kernel, n_sorted=n_sorted),
        out_shape=jax.ShapeDtypeStruct((t, d), F32),
        grid=(t // tm, N_GROUPS),
        in_specs=[pl.BlockSpec((tm, d), row),
                  pl.BlockSpec((1, d), const),
                  pl.BlockSpec((d, LANES), const), pl.BlockSpec((d, LANES), const),
                  pl.BlockSpec((1, LANES), const),
                  pl.BlockSpec((tm, tm), const),
                  pl.BlockSpec((1, EXPERTS_PER_GROUP, d, f), lambda i, e: (e, 0, 0, 0)),
                  pl.BlockSpec((1, EXPERTS_PER_GROUP, d, f), lambda i, e: (e, 0, 0, 0)),
                  pl.BlockSpec((1, EXPERTS_PER_GROUP * f, d), lambda i, e: (e, 0, 0))],
        out_specs=pl.BlockSpec((tm, d), row),
        scratch_shapes=[pltpu.VMEM((tm, d), BF16),
                        pltpu.VMEM((n_sorted, d), BF16),
                        pltpu.VMEM((n_sorted, LANES), F32),
                        pltpu.VMEM((n_sorted, d), BF16),
                        pltpu.VMEM((tm, n_sorted), BF16),
                        pltpu.VMEM((tm, LANES), F32),
                        pltpu.VMEM((8, LANES), F32),
                        pltpu.SMEM((1,), jnp.int32)],
        compiler_params=_params(2),
        name="hier_moe",
    )(x2, g.reshape(1, d), wr_hi, wr_lo, br, ltri, wg, wu, wd)


def _final_norm_kernel(x_ref, g_ref, o_ref):
    o_ref[...] = _rms(x_ref[...], g_ref[...])


def _final_norm(x2, g, *, tm=1024):
    t, d = x2.shape
    return pl.pallas_call(
        _final_norm_kernel,
        out_shape=jax.ShapeDtypeStruct((t, d), F32),
        grid=(t // tm,),
        in_specs=[pl.BlockSpec((tm, d), lambda i: (i, 0)), pl.BlockSpec((1, d), lambda i: (0, 0))],
        out_specs=pl.BlockSpec((tm, d), lambda i: (i, 0)),
        compiler_params=_params(1),
        name="final_norm",
    )(x2, g.reshape(1, d))


def _router_weights(w_gr, b_gr, w_er, b_er):
    d = w_gr.shape[0]
    w = jnp.concatenate([w_gr, jnp.moveaxis(w_er, 0, 1).reshape(d, N_EXPERTS)], axis=1)
    w = jnp.pad(w, ((0, 0), (0, LANES - w.shape[1])))
    bias = jnp.pad(jnp.concatenate([b_gr, b_er.reshape(-1)]), (0, LANES - N_GROUPS - N_EXPERTS))
    hi, lo = _split2(w)
    return hi, lo, bias.reshape(1, LANES)


def _layer(x2, b, s, ln1_g, w_in, mix_g, w_out, ln2_g, w_gr, b_gr, w_er, b_er, w_gate, w_up, w_down):
    t, d = x2.shape
    qkv = _qkv_proj(x2, ln1_g, w_in.astype(BF16)).reshape(b, s, 3 * d)
    oa = _moba_attention(qkv)
    ob = _sb_attention(qkv)
    oc = _dilated_mixture(qkv)
    x2 = _out_proj(x2, oa.reshape(t, -1), ob.reshape(t, -1), oc.reshape(t, -1), mix_g, w_out.astype(BF16))
    wr_hi, wr_lo, br = _router_weights(w_gr, b_gr, w_er, b_er)
    f = w_gate.shape[-1]
    return _moe(x2, ln2_g, wr_hi, wr_lo, br, w_gate.astype(BF16), w_up.astype(BF16),
                w_down.reshape(N_GROUPS, EXPERTS_PER_GROUP * f, d).astype(BF16))


def kernel(x, ln1_g, w_in, mix_norm_g, w_out, ln2_g, w_group_router, b_group_router,
           w_expert_router, b_expert_router, w_gate, w_up, w_down, final_norm_g):
    b, s, d = x.shape
    x2 = x.reshape(b * s, d)
    for l in range(ln1_g.shape[0]):
        x2 = _layer(x2, b, s, ln1_g[l], w_in[l], mix_norm_g[l], w_out[l], ln2_g[l],
                    w_group_router[l], b_group_router[l], w_expert_router[l], b_expert_router[l],
                    w_gate[l], w_up[l], w_down[l])
    return _final_norm(x2, final_norm_g).reshape(b, s, d)
```

```python
import functools

import jax
import jax.numpy as jnp
from jax import lax
from jax.experimental import pallas as pl
from jax.experimental.pallas import tpu as pltpu

F32 = jnp.float32
BF16 = jnp.bfloat16

D_MODEL = 1024
HEAD_DIM = 64
N_HEADS = 16
LANES = 128
HEADS_PER_BLOCK = LANES // HEAD_DIM
N_HEADS_MOBA = 4
N_HEADS_SB = 4
N_HEADS_DIL = 8
MOBA_BLOCK = 256
MOBA_TOPK = 3
DIL_CONFIGS = ((128, 1), (512, 4), (2048, 16))
DIL_N = 128
N_GROUPS = 4
EXPERTS_PER_GROUP = 4
N_EXPERTS = N_GROUPS * EXPERTS_PER_GROUP
D_EXPERT = 256
NORM_EPS = 1e-6
NEG_INF = -1e30
SCALE = HEAD_DIM ** -0.5

Q_BLK0 = 0
K_BLK0 = D_MODEL // LANES
V_BLK0 = 2 * D_MODEL // LANES
ROW_BLKS = 3 * D_MODEL // LANES
SB_BLK = N_HEADS_MOBA // HEADS_PER_BLOCK
DIL_BLK = (N_HEADS_MOBA + N_HEADS_SB) // HEADS_PER_BLOCK

VMEM_LIMIT = 56 * 1024 * 1024


def _params(n_axes, vmem=VMEM_LIMIT):
    return pltpu.CompilerParams(dimension_semantics=("arbitrary",) * n_axes,
                                vmem_limit_bytes=vmem)


def _dot_nt(a, b):
    return lax.dot_general(a, b, (((1,), (1,)), ((), ())), preferred_element_type=F32)


def _dot(a, b):
    return jnp.dot(a, b, preferred_element_type=F32)


def _split3(x):
    hi = x.astype(BF16)
    r1 = x - hi.astype(F32)
    mid = r1.astype(BF16)
    lo = (r1 - mid.astype(F32)).astype(BF16)
    return hi, mid, lo


def _split2(x):
    hi = x.astype(BF16)
    lo = (x - hi.astype(F32)).astype(BF16)
    return hi, lo


def _qkv_kernel(x_ref, g_ref, w_ref, o_ref, *, rows, tn):
    for c in range(x_ref.shape[0] // rows):
        r = slice(c * rows, (c + 1) * rows)
        x = x_ref[r, :]
        ms = jnp.mean(x * x, axis=-1, keepdims=True)
        h = (x * lax.rsqrt(ms + NORM_EPS) * g_ref[...]).astype(BF16)
        for j in range(w_ref.shape[1] // tn):
            o_ref[r, j * tn:(j + 1) * tn] = _dot(h, w_ref[:, j * tn:(j + 1) * tn]).astype(BF16)


def _qkv_proj(x2, g, w, *, tm=512, rows=256, tn=1024):
    t, d = x2.shape
    n = w.shape[1]
    return pl.pallas_call(
        functools.partial(_qkv_kernel, rows=rows, tn=tn),
        out_shape=jax.ShapeDtypeStruct((t, n), BF16),
        grid=(t // tm,),
        in_specs=[pl.BlockSpec((tm, d), lambda i: (i, 0)),
                  pl.BlockSpec((1, d), lambda i: (0, 0)),
                  pl.BlockSpec((d, n), lambda i: (0, 0))],
        out_specs=pl.BlockSpec((tm, n), lambda i: (i, 0)),
        compiler_params=_params(1),
        name="qkv_proj",
    )(x2, g.reshape(1, d), w)


_MB_SEL0 = 0
_MB_POS0 = 32
_MB_KILL = 40
_MB_BLK0 = 64


_ROW_CHUNK = 32


MOBA_DEAD_GAP = 110.0


def _moba_kernel(q_ref, k_ref, v_ref, o_ref, kmean_ref, kn2_ref, kx_ref, qaug_ref, m_ref, alpha_ref, acc_ref,
                 s_ref, p_ref, *, n_blk):
    pair = pl.program_id(1)
    qi = pl.program_id(2)
    blk = MOBA_BLOCK
    lane = lax.broadcasted_iota(jnp.int32, (blk, LANES), 1)
    row = lax.broadcasted_iota(jnp.int32, (blk, LANES), 0)
    lane_f = lane.astype(F32)
    slopes = [jnp.where(pair == 0, 2.0 ** (-2 * (hh + 1)), 2.0 ** (-2 * (hh + 3))).astype(F32)
              for hh in range(HEADS_PER_BLOCK)]

    @pl.when(qi == 0)
    def _():
        kmean_ref[...] = jnp.zeros_like(kmean_ref)
        kn2_ref[...] = jnp.zeros_like(kn2_ref)

        def body(n, c):
            kb = k_ref[0, pl.ds(pl.multiple_of(n * blk, blk), blk), :].astype(F32)
            kmean_ref[pl.ds(n, 1), :] = jnp.sum(kb, axis=0, keepdims=True) * (1.0 / blk)
            for hh in range(HEADS_PER_BLOCK):
                in_head = (lane >= hh * HEAD_DIM) & (lane < (hh + 1) * HEAD_DIM)
                n2 = jnp.sum(jnp.where(in_head, kb * kb, 0.0), axis=1, keepdims=True)
                n2 = jnp.max(jnp.broadcast_to(n2, (blk, LANES)), axis=0, keepdims=True)
                kn2_ref[hh] = jnp.maximum(kn2_ref[hh], jnp.broadcast_to(n2, (8, LANES)))
            return c

        lax.fori_loop(0, n_blk, body, 0)
        kx = jnp.zeros((blk, LANES), F32)
        for hh in range(HEADS_PER_BLOCK):
            kx = jnp.where(lane == _MB_POS0 + 2 * hh, slopes[hh] * ((row // LANES) * LANES).astype(F32), kx)
            kx = jnp.where(lane == _MB_POS0 + 2 * hh + 1, slopes[hh] * (row % LANES).astype(F32), kx)
        kx_ref[...] = kx.astype(BF16)

    q2 = q_ref[0]
    km_parts = _split3(kmean_ref[...])
    for hh in range(HEADS_PER_BLOCK):
        in_head = (lane >= hh * HEAD_DIM) & (lane < (hh + 1) * HEAD_DIM)
        qm = jnp.where(in_head, q2, jnp.zeros_like(q2))
        gate = _dot_nt(qm, km_parts[0]) + _dot_nt(qm, km_parts[1]) + _dot_nt(qm, km_parts[2])
        gate = jnp.where(lane < qi, gate, NEG_INF)
        gate = jnp.where(lane < n_blk, gate, -jnp.inf)
        sel = jnp.zeros((blk, LANES), jnp.bool_)
        for _ in range(MOBA_TOPK):
            gmax = jnp.max(gate, axis=1, keepdims=True)
            first = jnp.min(jnp.where(gate == gmax, lane_f, float(LANES)), axis=1, keepdims=True)
            pick = lane_f == first
            sel = sel | pick
            gate = jnp.where(pick, -jnp.inf, gate)
        sel = sel & (lane < qi)
        extra = jnp.where(sel, 0.0, NEG_INF)
        extra = jnp.where(lane >= n_blk, 0.0, extra)
        extra = jnp.where((lane == _MB_POS0 + 2 * hh) | (lane == _MB_POS0 + 2 * hh + 1) | (lane == _MB_KILL),
                          1.0, extra)
        blk_lane = lane - (_MB_BLK0 + 32 * hh)
        extra = jnp.where((blk_lane >= 0) & (blk_lane < 32),
                          slopes[hh] * (blk_lane * blk).astype(F32), extra)
        qaug_ref[hh] = jnp.concatenate([qm * jnp.asarray(SCALE, BF16), extra.astype(BF16)], axis=1)
        m_ref[hh] = jnp.full((blk, LANES), -jnp.inf, F32)
        acc_ref[hh] = jnp.zeros((blk, LANES), F32)

    lane1 = lax.broadcasted_iota(jnp.int32, (1, LANES), 1)
    rc = _ROW_CHUNK

    def score_stage(j, buf, *, is_own=False, valid=True):
        kj = k_ref[0, pl.ds(pl.multiple_of(j * blk, blk), blk), :]
        ind = (lane1 % 32 == j) & (lane1 >= _MB_BLK0) if is_own else \
              (lane1 % 32 == j) & ((lane1 < 32) | (lane1 >= _MB_BLK0))
        kill = jnp.where(lane1 == _MB_KILL, jnp.where(valid, 0.0, NEG_INF), 0.0)
        side = jnp.broadcast_to(jnp.where(ind, 1.0, kill), (blk, LANES)).astype(BF16)
        kx = jnp.where(ind | (lane1 == _MB_KILL), side, kx_ref[...])
        k_aug = jnp.concatenate([kj, kx], axis=1)
        s = _dot_nt(qaug_ref[...].reshape(HEADS_PER_BLOCK * blk, 2 * LANES), k_aug)
        s_ref[buf] = s.reshape(HEADS_PER_BLOCK, blk, blk)

    def softmax_stage(buf, *, is_own=False):
        for hh in range(HEADS_PER_BLOCK):
            for c in range(blk // rc):
                rows = slice(c * rc, (c + 1) * rc)
                s = s_ref[buf, hh, rows, :]
                if is_own:
                    col_c = lax.broadcasted_iota(jnp.int32, (rc, blk), 1)
                    row_c = lax.broadcasted_iota(jnp.int32, (rc, blk), 0) + c * rc
                    s = jnp.where(col_c <= row_c, s, NEG_INF)
                m_old = m_ref[hh, rows, :]
                m_new = jnp.maximum(m_old, jnp.max(s, axis=1, keepdims=True))
                alpha_ref[buf, hh, rows, :] = jnp.exp(m_old - m_new)
                m_ref[hh, rows, :] = m_new
                p = jnp.exp(s - jnp.concatenate([m_new, m_new], axis=1))
                p_ref[buf, hh, rows, :] = p.astype(BF16)

    def value_stage(j, buf):
        vj = v_ref[0, pl.ds(pl.multiple_of(j * blk, blk), blk), :]
        v_aug = jnp.concatenate(
            [jnp.where((lane1 >= hh * HEAD_DIM) & (lane1 < (hh + 1) * HEAD_DIM), vj, jnp.ones_like(vj))
             for hh in range(HEADS_PER_BLOCK)], axis=1)
        pv = _dot(p_ref[buf].reshape(HEADS_PER_BLOCK * blk, blk), v_aug)
        for hh in range(HEADS_PER_BLOCK):
            acc_ref[hh] = alpha_ref[buf, hh] * acc_ref[hh] + pv[hh * blk:(hh + 1) * blk, hh * LANES:(hh + 1) * LANES]

    n_past = qi

    @pl.when(n_past == 0)
    def _():
        score_stage(qi, 0, is_own=True)
        softmax_stage(0, is_own=True)
        value_stage(qi, 0)

    @pl.when(n_past > 0)
    def _():
        last = n_past - 1

        def past(t):
            return jnp.clip(last - t, 0, last)

        score_stage(qi, 1, is_own=True)
        score_stage(past(0), 0)
        softmax_stage(1, is_own=True)
        value_stage(qi, 1)
        softmax_stage(0)
        score_stage(past(1), 1, valid=1 < n_past)

        reach = jnp.zeros((1, LANES), F32)
        for hh in range(HEADS_PER_BLOCK):
            in_head = (lane >= hh * HEAD_DIM) & (lane < (hh + 1) * HEAD_DIM)
            qf = q2.astype(F32)
            qn2 = jnp.sum(jnp.where(in_head, qf * qf, 0.0), axis=1, keepdims=True)
            qn2 = jnp.max(jnp.broadcast_to(qn2, (blk, LANES)), axis=0, keepdims=True)
            bound = jnp.sqrt(qn2 * kn2_ref[hh, 0:1, :]) * SCALE
            m_min = jnp.min(m_ref[hh], axis=0, keepdims=True)
            reach = jnp.maximum(reach, (bound - m_min + MOBA_DEAD_GAP) / slopes[hh])
        n_past_f = jnp.full((1, LANES), n_past, jnp.int32).astype(F32)
        n_live = jnp.max(jnp.minimum((reach - 1.0) / blk + 1.0, n_past_f))

        def pair(state):
            tt, tf = state
            t = 2 * tt
            score_stage(past(t), 0, valid=t < n_past)
            softmax_stage(1)
            value_stage(past(t - 2), 0)
            score_stage(past(t + 1), 1, valid=t + 1 < n_past)
            softmax_stage(0)
            value_stage(past(t - 1), 1)
            return tt + 1, tf + 2.0

        lax.while_loop(lambda st: st[1] < n_live + 2.0, pair, (jnp.int32(1), jnp.float32(2.0)))

    acc0 = acc_ref[0]
    acc1 = acc_ref[1]
    o0 = acc0 / pltpu.roll(acc0, HEAD_DIM, axis=1)
    o1 = acc1 / pltpu.roll(acc1, HEAD_DIM, axis=1)
    o_ref[0] = jnp.where(lane < HEAD_DIM, o0, o1).astype(BF16)


def _moba_attention(qkv):
    b, s, _ = qkv.shape
    blk = MOBA_BLOCK
    n_blk = s // blk
    assert s % blk == 0 and MOBA_TOPK <= n_blk - 1 and n_blk <= 32
    n_pairs = N_HEADS_MOBA // HEADS_PER_BLOCK
    return pl.pallas_call(
        functools.partial(_moba_kernel, n_blk=n_blk),
        out_shape=jax.ShapeDtypeStruct((b, s, N_HEADS_MOBA * HEAD_DIM), BF16),
        grid=(b, n_pairs, n_blk),
        in_specs=[pl.BlockSpec((1, blk, LANES), lambda bi, p, i: (bi, i, Q_BLK0 + p)),
                  pl.BlockSpec((1, s, LANES), lambda bi, p, i: (bi, 0, K_BLK0 + p)),
                  pl.BlockSpec((1, s, LANES), lambda bi, p, i: (bi, 0, V_BLK0 + p))],
        out_specs=pl.BlockSpec((1, blk, LANES), lambda bi, p, i: (bi, i, p)),
        scratch_shapes=[pltpu.VMEM((LANES, LANES), F32),
                        pltpu.VMEM((HEADS_PER_BLOCK, 8, LANES), F32),
                        pltpu.VMEM((blk, LANES), BF16),
                        pltpu.VMEM((HEADS_PER_BLOCK, blk, 2 * LANES), BF16),
                        pltpu.VMEM((HEADS_PER_BLOCK, blk, LANES), F32),
                        pltpu.VMEM((2, HEADS_PER_BLOCK, blk, LANES), F32),
                        pltpu.VMEM((HEADS_PER_BLOCK, blk, LANES), F32),
                        pltpu.VMEM((2, HEADS_PER_BLOCK, blk, blk), F32),
                        pltpu.VMEM((2, HEADS_PER_BLOCK, blk, blk), BF16)],
        compiler_params=_params(3),
        name="moba_attn",
    )(qkv, qkv, qkv)


SB_TILE = 256


def _softplus(z):
    return jnp.maximum(z, 0.0) + jnp.log(1.0 + jnp.exp(-jnp.abs(z)))


_SB_SLOTS = 3
SB_DEAD_MASS = 128.0


def _sb_kernel(q_ref, k_ref, v_ref, u_ref, o_ref, qaug_ref, carry_ref, acc_ref,
               z_ref, sphl_ref, c_ref, rs_ref, a_ref):
    qi = pl.program_id(2)
    t = SB_TILE
    rc = _ROW_CHUNK
    lane = lax.broadcasted_iota(jnp.int32, (t, LANES), 1)
    lane1 = lax.broadcasted_iota(jnp.int32, (1, LANES), 1)
    q2 = q_ref[0]
    for hh in range(HEADS_PER_BLOCK):
        in_head = (lane >= hh * HEAD_DIM) & (lane < (hh + 1) * HEAD_DIM)
        qm = jnp.where(in_head, q2, jnp.zeros_like(q2)) * jnp.asarray(SCALE, BF16)
        qaug_ref[hh] = jnp.concatenate([qm, jnp.where(lane == 0, 1.0, 0.0).astype(BF16)], axis=1)

    def score_stage(j, slot, *, valid=True):
        kj = k_ref[0, pl.ds(pl.multiple_of(j * t, t), t), :]
        kill = jnp.where(lane1 == 0, jnp.where(valid, 0.0, NEG_INF), 0.0)
        k_aug = jnp.concatenate([kj, jnp.broadcast_to(kill, (t, LANES)).astype(BF16)], axis=1)
        z = _dot_nt(qaug_ref[...].reshape(HEADS_PER_BLOCK * t, 2 * LANES), k_aug)
        z_ref[slot] = z.reshape(HEADS_PER_BLOCK, t, t)

    def _past_mask(c):
        col_c = lax.broadcasted_iota(jnp.int32, (rc, t), 1)
        row_c = lax.broadcasted_iota(jnp.int32, (rc, t), 0) + c * rc
        return col_c < row_c

    def softplus_stage(slot, *, diagonal=False):
        for hh in range(HEADS_PER_BLOCK):
            for c in range(t // rc):
                rows = slice(c * rc, (c + 1) * rc)
                sp = _softplus(z_ref[slot, hh, rows, :])
                if diagonal:
                    sp = jnp.where(_past_mask(c), sp, 0.0)
                hi, lo = _split2(sp)
                sphl_ref[slot, pl.ds((2 * hh) * t + c * rc, rc), :] = hi
                sphl_ref[slot, pl.ds((2 * hh + 1) * t + c * rc, rc), :] = lo
                rs_ref[slot, hh, rows, :] = jnp.broadcast_to(jnp.sum(sp, axis=1, keepdims=True), (rc, LANES))

    def suffix_stage(slot):
        c_ref[slot] = _dot(sphl_ref[slot], u_ref[...])

    def weight_stage(slot, *, diagonal=False):
        for hh in range(HEADS_PER_BLOCK):
            for c in range(t // rc):
                rows = slice(c * rc, (c + 1) * rc)
                z = z_ref[slot, hh, rows, :]
                cc = c_ref[slot, pl.ds((2 * hh) * t + c * rc, rc), :] + \
                    c_ref[slot, pl.ds((2 * hh + 1) * t + c * rc, rc), :]
                if diagonal:
                    a = jnp.where(_past_mask(c), jnp.exp(z - cc), 0.0)
                    carry_ref[hh, rows, :] = rs_ref[slot, hh, rows, :]
                else:
                    carry = carry_ref[hh, rows, :]
                    a = jnp.exp(z - (cc + jnp.concatenate([carry, carry], axis=1)))
                    carry_ref[hh, rows, :] = carry + rs_ref[slot, hh, rows, :]
                a_ref[slot, hh, rows, :] = a.astype(BF16)

    def value_stage(j, slot, *, first=False):
        vj = v_ref[0, pl.ds(pl.multiple_of(j * t, t), t), :]
        av = _dot(a_ref[slot].reshape(HEADS_PER_BLOCK * t, t), vj)
        for hh in range(HEADS_PER_BLOCK):
            av_h = av[hh * t:(hh + 1) * t, :]
            acc_ref[hh] = av_h if first else acc_ref[hh] + av_h

    n_past = qi

    @pl.when(n_past == 0)
    def _():
        score_stage(qi, 0)
        softplus_stage(0, diagonal=True)
        suffix_stage(0)
        weight_stage(0, diagonal=True)
        value_stage(qi, 0, first=True)

    @pl.when(n_past > 0)
    def _():
        last = n_past - 1

        def key_tile(i):
            return jnp.clip(last - i, 0, last)

        score_stage(qi, 2)
        score_stage(key_tile(0), 0)
        softplus_stage(2, diagonal=True)
        suffix_stage(2)
        softplus_stage(0)
        suffix_stage(0)
        weight_stage(2, diagonal=True)
        value_stage(qi, 2, first=True)
        weight_stage(0)
        value_stage(key_tile(0), 0)

        def trip(tt, first=False):
            for k in range(_SB_SLOTS):
                i = 1 + _SB_SLOTS * tt + k
                score_stage(key_tile(i), (1 + k) % _SB_SLOTS, valid=i < n_past)
                if not (first and k < 1):
                    softplus_stage(k % _SB_SLOTS)
                    suffix_stage(k % _SB_SLOTS)
                if not (first and k < 2):
                    weight_stage((k - 1) % _SB_SLOTS)
                    value_stage(key_tile(i - 2), (k - 1) % _SB_SLOTS)
            return tt + 1, jnp.min(carry_ref[...])

        @pl.when((n_past > 1) & (jnp.min(carry_ref[...]) < SB_DEAD_MASS))
        def _():
            n_trips = (n_past - 1 + 2 + _SB_SLOTS - 1) // _SB_SLOTS
            lax.while_loop(lambda st: (st[0] < n_trips) & (st[1] < SB_DEAD_MASS), lambda st: trip(st[0]),
                           trip(jnp.int32(0), first=True))

    o_ref[0] = jnp.where(lane < HEAD_DIM, acc_ref[0], acc_ref[1]).astype(BF16)


def _sb_attention(qkv):
    b, s, _ = qkv.shape
    t = SB_TILE
    assert s % t == 0
    n_pairs = N_HEADS_SB // HEADS_PER_BLOCK
    u = (lax.broadcasted_iota(jnp.int32, (t, t), 0) >= lax.broadcasted_iota(jnp.int32, (t, t), 1)).astype(BF16)
    return pl.pallas_call(
        _sb_kernel,
        out_shape=jax.ShapeDtypeStruct((b, s, N_HEADS_SB * HEAD_DIM), BF16),
        grid=(b, n_pairs, s // t),
        in_specs=[pl.BlockSpec((1, t, LANES), lambda bi, p, i: (bi, i, Q_BLK0 + SB_BLK + p)),
                  pl.BlockSpec((1, s, LANES), lambda bi, p, i: (bi, 0, K_BLK0 + SB_BLK + p)),
                  pl.BlockSpec((1, s, LANES), lambda bi, p, i: (bi, 0, V_BLK0 + SB_BLK + p)),
                  pl.BlockSpec((t, t), lambda bi, p, i: (0, 0))],
        out_specs=pl.BlockSpec((1, t, LANES), lambda bi, p, i: (bi, i, p)),
        scratch_shapes=[pltpu.VMEM((HEADS_PER_BLOCK, t, 2 * LANES), BF16),
                        pltpu.VMEM((HEADS_PER_BLOCK, t, LANES), F32),
                        pltpu.VMEM((HEADS_PER_BLOCK, t, LANES), F32),
                        pltpu.VMEM((_SB_SLOTS, HEADS_PER_BLOCK, t, t), F32),
                        pltpu.VMEM((_SB_SLOTS, 2 * HEADS_PER_BLOCK * t, t), BF16),
                        pltpu.VMEM((_SB_SLOTS, 2 * HEADS_PER_BLOCK * t, t), F32),
                        pltpu.VMEM((_SB_SLOTS, HEADS_PER_BLOCK, t, LANES), F32),
                        pltpu.VMEM((_SB_SLOTS, HEADS_PER_BLOCK, t, t), BF16)],
        compiler_params=_params(3),
        name="sb_attn",
    )(qkv, qkv, qkv, u)


DIL_SPAN = 2048
_DIL_UNROLL = 16


def _dil_kernel(q_ref, k_ref, v_ref, kp_ref, vp_ref, o_ref, qf_ref, kf_ref, vf_ref, oc_ref, lse_ref, bias_ref):
    pair = pl.program_id(1)
    i = pl.program_id(2)
    n = DIL_N
    span = DIL_SPAN
    lane = lax.broadcasted_iota(jnp.int32, (n, LANES), 1)
    colh = lax.broadcasted_iota(jnp.int32, (n, 2 * n), 1)

    @pl.when(i == 0)
    def _():
        row = lax.broadcasted_iota(jnp.int32, (n, 2 * n), 0)
        delta = row + n - colh
        valid = (delta >= 0) & (delta <= n)
        for ci, (_, d) in enumerate(DIL_CONFIGS):
            for hh in range(HEADS_PER_BLOCK):
                slope = jnp.asarray(2.0 ** -(hh + 1), F32)
                for p in range(1, N_HEADS_DIL // HEADS_PER_BLOCK):
                    slope = jnp.where(pair == p, 2.0 ** -(2 * p + hh + 1), slope)
                bias_ref[ci, hh] = jnp.where(valid, -slope * (delta * d).astype(F32), NEG_INF)

    qf_ref[...] = q_ref[0].astype(F32)
    kf_ref[0:span, :] = kp_ref[0].astype(F32)
    kf_ref[span:2 * span, :] = k_ref[0].astype(F32)
    vf_ref[0:span, :] = vp_ref[0].astype(F32)
    vf_ref[span:2 * span, :] = v_ref[0].astype(F32)

    def rows(start, size, d):
        return pl.ds(start, size) if d == 1 else pl.ds(start, size, stride=d)

    def unit(ci, d, r, bl):
        q0 = r + bl * (n * d)
        qu = qf_ref[rows(q0, n, d), :].astype(BF16)
        ku = kf_ref[rows(span + q0 - n * d, 2 * n, d), :].astype(BF16)
        vu = vf_ref[rows(span + q0 - n * d, 2 * n, d), :].astype(BF16)
        no_prev = jnp.where((i == 0) & (bl == 0), NEG_INF, 0.0)
        qms = [jnp.where((lane >= hh * HEAD_DIM) & (lane < (hh + 1) * HEAD_DIM), qu, jnp.zeros_like(qu))
               * jnp.asarray(SCALE, BF16) for hh in range(HEADS_PER_BLOCK)]
        s_both = _dot_nt(jnp.concatenate(qms, axis=0), ku)
        ps, dens, lse_h = [], [], []
        for hh in range(HEADS_PER_BLOCK):
            s = s_both[hh * n:(hh + 1) * n, :] + bias_ref[ci, hh] + jnp.where(colh < n, no_prev, 0.0)
            m = jnp.max(s, axis=1, keepdims=True)
            p = jnp.exp(s - m)
            den = jnp.sum(p, axis=1, keepdims=True)
            ps.append(p.astype(BF16))
            dens.append(den)
            lse_h.append(jnp.broadcast_to(m + jnp.log(den), (n, LANES)))
        pv = _dot(jnp.concatenate(ps, axis=0), vu)
        o_h = [pv[hh * n:(hh + 1) * n, :] / dens[hh] for hh in range(HEADS_PER_BLOCK)]
        oc_ref[ci, rows(q0, n, d), :] = jnp.where(lane < HEAD_DIM, o_h[0], o_h[1])
        lse_ref[ci, rows(q0, n, d), :] = jnp.where(lane < HEAD_DIM, lse_h[0], lse_h[1])

    n_units = span // n
    for ci, (_, d) in enumerate(DIL_CONFIGS):
        per_res = n_units // d

        def group(g, c, ci=ci, d=d, per_res=per_res):
            for k in range(_DIL_UNROLL):
                u = g * _DIL_UNROLL + k
                unit(ci, d, u // per_res, u % per_res)
            return c

        lax.fori_loop(0, n_units // _DIL_UNROLL, group, 0)

    mc = 256
    for c0 in range(0, span, mc):
        l1, l2, l3 = (lse_ref[ci, c0:c0 + mc, :] for ci in range(3))
        lmax = jnp.maximum(jnp.maximum(l1, l2), l3)
        e1, e2, e3 = jnp.exp(l1 - lmax), jnp.exp(l2 - lmax), jnp.exp(l3 - lmax)
        mix = (e1 * oc_ref[0, c0:c0 + mc, :] + e2 * oc_ref[1, c0:c0 + mc, :] + e3 * oc_ref[2, c0:c0 + mc, :]) \
            / (e1 + e2 + e3)
        o_ref[0, c0:c0 + mc, :] = mix.astype(BF16)


def _dilated_mixture(qkv):
    b, s, _ = qkv.shape
    span = DIL_SPAN
    assert s % span == 0 and all(w <= span and span % (DIL_N * d) == 0 for w, d in DIL_CONFIGS)
    n_pairs = N_HEADS_DIL // HEADS_PER_BLOCK
    n_cfg = len(DIL_CONFIGS)

    def cur(off):
        return lambda bi, p, i: (bi, i, off + DIL_BLK + p)

    def prev(off):
        return lambda bi, p, i: (bi, jnp.maximum(i - 1, 0), off + DIL_BLK + p)

    blk = (1, span, LANES)
    return pl.pallas_call(
        _dil_kernel,
        out_shape=jax.ShapeDtypeStruct((b, s, N_HEADS_DIL * HEAD_DIM), BF16),
        grid=(b, n_pairs, s // span),
        in_specs=[pl.BlockSpec(blk, cur(Q_BLK0)), pl.BlockSpec(blk, cur(K_BLK0)), pl.BlockSpec(blk, cur(V_BLK0)),
                  pl.BlockSpec(blk, prev(K_BLK0)), pl.BlockSpec(blk, prev(V_BLK0))],
        out_specs=pl.BlockSpec(blk, lambda bi, p, i: (bi, i, p)),
        scratch_shapes=[pltpu.VMEM((span, LANES), F32),
                        pltpu.VMEM((2 * span, LANES), F32),
                        pltpu.VMEM((2 * span, LANES), F32),
                        pltpu.VMEM((n_cfg, span, LANES), F32),
                        pltpu.VMEM((n_cfg, span, LANES), F32),
                        pltpu.VMEM((n_cfg, HEADS_PER_BLOCK, DIL_N, 2 * DIL_N), F32)],
        compiler_params=_params(3),
        name="dilated_attn",
    )(qkv, qkv, qkv, qkv, qkv)


def _rms(x, g):
    return x * lax.rsqrt(jnp.mean(x * x, axis=-1, keepdims=True) + NORM_EPS) * g


def _outproj_kernel(x_ref, oa_ref, ob_ref, oc_ref, g_ref, w_ref, out_ref):
    g = g_ref[...]
    wa = N_HEADS_MOBA * HEAD_DIM
    wb = wa + N_HEADS_SB * HEAD_DIM
    y = jnp.concatenate([_rms(oa_ref[...].astype(F32), g[:, :wa]),
                         _rms(ob_ref[...].astype(F32), g[:, wa:wb]),
                         _rms(oc_ref[...].astype(F32), g[:, wb:])], axis=1).astype(BF16)
    out_ref[...] = x_ref[...] + _dot(y, w_ref[...])


def _out_proj(x2, oa, ob, oc, g, w, *, tm=512):
    t, d = x2.shape
    row = lambda i: (i, 0)
    const = lambda i: (0, 0)
    return pl.pallas_call(
        _outproj_kernel,
        out_shape=jax.ShapeDtypeStruct((t, d), F32),
        grid=(t // tm,),
        in_specs=[pl.BlockSpec((tm, d), row),
                  pl.BlockSpec((tm, oa.shape[1]), row), pl.BlockSpec((tm, ob.shape[1]), row),
                  pl.BlockSpec((tm, oc.shape[1]), row),
                  pl.BlockSpec((1, d), const), pl.BlockSpec((d, d), const)],
        out_specs=pl.BlockSpec((tm, d), row),
        compiler_params=_params(1),
        name="out_proj",
    )(x2, oa, ob, oc, g.reshape(1, d), w)


_EXP_LANE0 = N_GROUPS
_MOE_ROWS = 256
_MOE_CHUNK = 128


def _moe_kernel(x_ref, g_ref, wrh_ref, wrl_ref, br_ref, ltri_ref, fg_ref, wg_ref, wu_ref, wd_ref, out_ref,
                h_ref, hs_ref, cws_ref, ys_ref, pt_ref, pos_ref, nck_ref, cb_ref, *, n_sorted, final_norm):
    e = pl.program_id(1)
    tm = x_ref.shape[0]
    lane = lax.broadcasted_iota(jnp.int32, (tm, LANES), 1)
    lane1 = lax.broadcasted_iota(jnp.int32, (1, LANES), 1)

    @pl.when(e == 0)
    def _():
        x = x_ref[...]
        h = _rms(x, g_ref[...])
        hh, hl = _split2(h)
        logits = _dot(hh, wrh_ref[...]) + _dot(hh, wrl_ref[...]) + _dot(hl, wrh_ref[...]) + br_ref[...]
        lane_f = lane.astype(F32)
        big = float(LANES)
        gl = jnp.where(lane < N_GROUPS, logits, -jnp.inf)
        gmax = jnp.max(gl, axis=1, keepdims=True)
        gidx = jnp.min(jnp.where(gl == gmax, lane_f, big), axis=1, keepdims=True)
        g_w = 1.0 / jnp.sum(jnp.exp(gl - gmax), axis=1, keepdims=True)
        lane_group = ((lane - _EXP_LANE0) // EXPERTS_PER_GROUP).astype(F32)
        in_group = (lane >= _EXP_LANE0) & (lane < _EXP_LANE0 + N_EXPERTS) & (lane_group == gidx)
        el = jnp.where(in_group, logits, -jnp.inf)
        v1 = jnp.max(el, axis=1, keepdims=True)
        i1 = jnp.min(jnp.where(el == v1, lane_f, big), axis=1, keepdims=True)
        el2 = jnp.where(lane_f == i1, -jnp.inf, el)
        v2 = jnp.max(el2, axis=1, keepdims=True)
        i2 = jnp.min(jnp.where(el2 == v2, lane_f, big), axis=1, keepdims=True)
        r = jnp.exp(v2 - v1)
        w1 = g_w / (1.0 + r)
        w2 = g_w * r / (1.0 + r)
        comb = jnp.where(lane_f == i1, w1, 0.0) + jnp.where(lane_f == i2, w2, 0.0)

        onehot = jnp.where((lane_f == gidx) & (lane < N_GROUPS), 1.0, 0.0)
        before = _dot(ltri_ref[...], onehot.astype(BF16))
        rank = jnp.sum(onehot * before, axis=1, keepdims=True)
        count = jnp.sum(onehot, axis=0, keepdims=True)
        chunks = jnp.floor((count + (_MOE_CHUNK - 1.0)) * (1.0 / _MOE_CHUNK))
        nck_ref[...] = jnp.broadcast_to(chunks, nck_ref.shape)
        pos = rank
        start = jnp.float32(0.0)
        for gi in range(N_GROUPS):
            pos = pos + jnp.where(gidx == float(gi), start, 0.0)
            start = start + jnp.max(jnp.where(lane1 == gi, chunks, 0.0)) * _MOE_CHUNK
        pos_rep = jnp.broadcast_to(pos, (tm, LANES))
        pos_ref[...] = pos_rep
        pos_t = jnp.transpose(pos_rep)[0:1, :]
        h_ref[...] = hh
        c_hi, c_lo = _split2(comb)
        for c in range(n_sorted // _MOE_ROWS):
            rows = slice(c * _MOE_ROWS, (c + 1) * _MOE_ROWS)
            slot = (lax.broadcasted_iota(jnp.int32, (_MOE_ROWS, tm), 0) + c * _MOE_ROWS).astype(F32)
            perm = jnp.where(slot == pos_t, 1.0, 0.0).astype(BF16)
            hs_ref[rows, :] = _dot(perm, h_ref[...]).astype(BF16)
            cws_ref[rows, :] = _dot(perm, c_hi) + _dot(perm, c_lo)
        ys_ref[...] = jnp.zeros_like(ys_ref)
        cb_ref[0] = 0

    lane_c = lax.broadcasted_iota(jnp.int32, (_MOE_CHUNK, LANES), 1)
    n_chunks = jnp.max(jnp.where(lane1 == e, nck_ref[0:1, :], 0.0))

    def chunk(state):
        cb, cf = state
        rows = pl.ds(pl.multiple_of(cb * _MOE_CHUNK, _MOE_CHUNK), _MOE_CHUNK)
        hc = hs_ref[rows, :]
        cwc = cws_ref[rows, :]
        acts = []
        for k in range(EXPERTS_PER_GROUP):
            gate = _dot(hc, wg_ref[0, k])
            up = _dot(hc, wu_ref[0, k])
            cw = jnp.sum(jnp.where(lane_c == _EXP_LANE0 + e * EXPERTS_PER_GROUP + k, cwc, 0.0),
                         axis=1, keepdims=True)
            acts.append((gate / (1.0 + jnp.exp(-gate)) * up * cw).astype(BF16))
        ys_ref[rows, :] = _dot(jnp.concatenate(acts, axis=1), wd_ref[0]).astype(BF16)
        return cb + 1, cf + 1.0

    cb_end, _ = lax.while_loop(lambda st: st[1] < n_chunks, chunk, (cb_ref[0], jnp.float32(0.0)))
    cb_ref[0] = cb_end

    @pl.when(e == N_GROUPS - 1)
    def _():
        pos_rep2 = jnp.concatenate([pos_ref[...], pos_ref[...]], axis=1)
        for c in range(n_sorted // _MOE_ROWS):
            slot = (lax.broadcasted_iota(jnp.int32, (tm, _MOE_ROWS), 1) + c * _MOE_ROWS).astype(F32)
            perm_t = jnp.where(slot == pos_rep2, 1.0, 0.0).astype(BF16)
            pt_ref[:, c * _MOE_ROWS:(c + 1) * _MOE_ROWS] = perm_t
        y = x_ref[...] + _dot(pt_ref[...], ys_ref[...])
        out_ref[...] = _rms(y, fg_ref[...]) if final_norm else y


def _moe(x2, g, wr_hi, wr_lo, br, wg, wu, wd, final_g, *, final_norm, tm=1024):
    t, d = x2.shape
    f = wg.shape[3]
    n_sorted = tm + N_GROUPS * _MOE_CHUNK
    assert n_sorted % _MOE_ROWS == 0
    ltri = (lax.broadcasted_iota(jnp.int32, (tm, tm), 0) > lax.broadcasted_iota(jnp.int32, (tm, tm), 1)).astype(BF16)
    row = lambda i, e: (i, 0)
    const = lambda i, e: (0, 0)
    return pl.pallas_call(
        functools.partial(_moe_kernel, n_sorted=n_sorted, final_norm=final_norm),
        out_shape=jax.ShapeDtypeStruct((t, d), F32),
        grid=(t // tm, N_GROUPS),
        in_specs=[pl.BlockSpec((tm, d), row),
                  pl.BlockSpec((1, d), const),
                  pl.BlockSpec((d, LANES), const), pl.BlockSpec((d, LANES), const),
                  pl.BlockSpec((1, LANES), const),
                  pl.BlockSpec((tm, tm), const),
                  pl.BlockSpec((1, d), const),
                  pl.BlockSpec((1, EXPERTS_PER_GROUP, d, f), lambda i, e: (e, 0, 0, 0)),
                  pl.BlockSpec((1, EXPERTS_PER_GROUP, d, f), lambda i, e: (e, 0, 0, 0)),
                  pl.BlockSpec((1, EXPERTS_PER_GROUP * f, d), lambda i, e: (e, 0, 0))],
        out_specs=pl.BlockSpec((tm, d), row),
        scratch_shapes=[pltpu.VMEM((tm, d), BF16),
                        pltpu.VMEM((n_sorted, d), BF16),
                        pltpu.VMEM((n_sorted, LANES), F32),
                        pltpu.VMEM((n_sorted, d), BF16),
                        pltpu.VMEM((tm, n_sorted), BF16),
                        pltpu.VMEM((tm, LANES), F32),
                        pltpu.VMEM((8, LANES), F32),
                        pltpu.SMEM((1,), jnp.int32)],
        compiler_params=_params(2),
        name="hier_moe",
    )(x2, g.reshape(1, d), wr_hi, wr_lo, br, ltri, final_g.reshape(1, d), wg, wu, wd)


def _router_weights(w_gr, b_gr, w_er, b_er):
    d = w_gr.shape[0]
    w = jnp.concatenate([w_gr, jnp.moveaxis(w_er, 0, 1).reshape(d, N_EXPERTS)], axis=1)
    w = jnp.pad(w, ((0, 0), (0, LANES - w.shape[1])))
    bias = jnp.pad(jnp.concatenate([b_gr, b_er.reshape(-1)]), (0, LANES - N_GROUPS - N_EXPERTS))
    hi, lo = _split2(w)
    return hi, lo, bias.reshape(1, LANES)


def _layer(x2, b, s, ln1_g, w_in, mix_g, w_out, ln2_g, w_gr, b_gr, w_er, b_er, w_gate, w_up, w_down,
           final_g, final_norm):
    t, d = x2.shape
    qkv = _qkv_proj(x2, ln1_g, w_in.astype(BF16)).reshape(b, s, 3 * d)
    oa = _moba_attention(qkv)
    ob = _sb_attention(qkv)
    oc = _dilated_mixture(qkv)
    x2 = _out_proj(x2, oa.reshape(t, -1), ob.reshape(t, -1), oc.reshape(t, -1), mix_g, w_out.astype(BF16))
    wr_hi, wr_lo, br = _router_weights(w_gr, b_gr, w_er, b_er)
    f = w_gate.shape[-1]
    return _moe(x2, ln2_g, wr_hi, wr_lo, br, w_gate.astype(BF16), w_up.astype(BF16),
                w_down.reshape(N_GROUPS, EXPERTS_PER_GROUP * f, d).astype(BF16), final_g, final_norm=final_norm)


def kernel(x, ln1_g, w_in, mix_norm_g, w_out, ln2_g, w_group_router, b_group_router,
           w_expert_router, b_expert_router, w_gate, w_up, w_down, final_norm_g):
    b, s, d = x.shape
    x2 = x.reshape(b * s, d)
    depth = ln1_g.shape[0]
    for l in range(depth):
        x2 = _layer(x2, b, s, ln1_g[l], w_in[l], mix_norm_g[l], w_out[l], ln2_g[l],
                    w_group_router[l], b_group_router[l], w_expert_router[l], b_expert_router[l],
                    w_gate[l], w_up[l], w_down[l], final_norm_g, final_norm=(l == depth - 1))
    return x2.reshape(b, s, d)
```

```python
import functools

import jax
import jax.numpy as jnp
from jax import lax
from jax.experimental import pallas as pl
from jax.experimental.pallas import tpu as pltpu

F32 = jnp.float32
BF16 = jnp.bfloat16

D_MODEL = 1024
HEAD_DIM = 64
N_HEADS = 16
LANES = 128
HEADS_PER_BLOCK = LANES // HEAD_DIM
N_HEADS_MOBA = 4
N_HEADS_SB = 4
N_HEADS_DIL = 8
MOBA_BLOCK = 256
MOBA_TOPK = 3
DIL_CONFIGS = ((128, 1), (512, 4), (2048, 16))
DIL_N = 128
N_GROUPS = 4
EXPERTS_PER_GROUP = 4
N_EXPERTS = N_GROUPS * EXPERTS_PER_GROUP
D_EXPERT = 256
NORM_EPS = 1e-6
NEG_INF = -1e30
SCALE = HEAD_DIM ** -0.5

Q_BLK0 = 0
K_BLK0 = D_MODEL // LANES
V_BLK0 = 2 * D_MODEL // LANES
ROW_BLKS = 3 * D_MODEL // LANES
SB_BLK = N_HEADS_MOBA // HEADS_PER_BLOCK
DIL_BLK = (N_HEADS_MOBA + N_HEADS_SB) // HEADS_PER_BLOCK

VMEM_LIMIT = 56 * 1024 * 1024


def _params(n_axes, vmem=VMEM_LIMIT):
    return pltpu.CompilerParams(dimension_semantics=("arbitrary",) * n_axes,
                                vmem_limit_bytes=vmem)


def _dot_nt(a, b):
    return lax.dot_general(a, b, (((1,), (1,)), ((), ())), preferred_element_type=F32)


def _dot(a, b):
    return jnp.dot(a, b, preferred_element_type=F32)


def _split3(x):
    hi = x.astype(BF16)
    r1 = x - hi.astype(F32)
    mid = r1.astype(BF16)
    lo = (r1 - mid.astype(F32)).astype(BF16)
    return hi, mid, lo


def _split2(x):
    hi = x.astype(BF16)
    lo = (x - hi.astype(F32)).astype(BF16)
    return hi, lo


def _qkv_kernel(x_ref, g_ref, w_ref, o_ref, *, rows, tn):
    for c in range(x_ref.shape[0] // rows):
        r = slice(c * rows, (c + 1) * rows)
        x = x_ref[r, :]
        ms = jnp.mean(x * x, axis=-1, keepdims=True)
        h = (x * lax.rsqrt(ms + NORM_EPS) * g_ref[...]).astype(BF16)
        for j in range(w_ref.shape[1] // tn):
            o_ref[r, j * tn:(j + 1) * tn] = _dot(h, w_ref[:, j * tn:(j + 1) * tn]).astype(BF16)


def _qkv_proj(x2, g, w, *, tm=512, rows=256, tn=1024):
    t, d = x2.shape
    n = w.shape[1]
    return pl.pallas_call(
        functools.partial(_qkv_kernel, rows=rows, tn=tn),
        out_shape=jax.ShapeDtypeStruct((t, n), BF16),
        grid=(t // tm,),
        in_specs=[pl.BlockSpec((tm, d), lambda i: (i, 0)),
                  pl.BlockSpec((1, d), lambda i: (0, 0)),
                  pl.BlockSpec((d, n), lambda i: (0, 0))],
        out_specs=pl.BlockSpec((tm, n), lambda i: (i, 0)),
        compiler_params=_params(1),
        name="qkv_proj",
    )(x2, g.reshape(1, d), w)


_MB_SEL0 = 0
_MB_POS0 = 32
_MB_KILL = 40
_MB_MAXBLK = 32
_MB_BLK0 = 64


_ROW_CHUNK = 32


MOBA_DEAD_GAP = 110.0


def _moba_kernel(q_ref, k_ref, v_ref, o_ref, kmean_ref, kn2_ref, kx_ref, qaug_ref, m_ref, alpha_ref, acc_ref,
                 s_ref, p_ref, *, n_blk):
    pair = pl.program_id(1)
    qi = pl.program_id(2)
    blk = MOBA_BLOCK
    lane = lax.broadcasted_iota(jnp.int32, (blk, LANES), 1)
    row = lax.broadcasted_iota(jnp.int32, (blk, LANES), 0)
    lane_f = lane.astype(F32)
    slopes = [jnp.where(pair == 0, 2.0 ** (-2 * (hh + 1)), 2.0 ** (-2 * (hh + 3))).astype(F32)
              for hh in range(HEADS_PER_BLOCK)]

    @pl.when(qi == 0)
    def _():
        kmean_ref[...] = jnp.zeros_like(kmean_ref)
        kn2_ref[...] = jnp.zeros_like(kn2_ref)

        def body(n, c):
            kb = k_ref[0, pl.ds(pl.multiple_of(n * blk, blk), blk), :].astype(F32)
            kmean_ref[pl.ds(n, 1), :] = jnp.sum(kb, axis=0, keepdims=True) * (1.0 / blk)
            for hh in range(HEADS_PER_BLOCK):
                in_head = (lane >= hh * HEAD_DIM) & (lane < (hh + 1) * HEAD_DIM)
                n2 = jnp.sum(jnp.where(in_head, kb * kb, 0.0), axis=1, keepdims=True)
                n2 = jnp.max(jnp.broadcast_to(n2, (blk, LANES)), axis=0, keepdims=True)
                kn2_ref[hh] = jnp.maximum(kn2_ref[hh], jnp.broadcast_to(n2, (8, LANES)))
            return c

        lax.fori_loop(0, n_blk, body, 0)
        kx = jnp.zeros((blk, LANES), F32)
        for hh in range(HEADS_PER_BLOCK):
            kx = jnp.where(lane == _MB_POS0 + 2 * hh, slopes[hh] * ((row // LANES) * LANES).astype(F32), kx)
            kx = jnp.where(lane == _MB_POS0 + 2 * hh + 1, slopes[hh] * (row % LANES).astype(F32), kx)
        kx_ref[...] = kx.astype(BF16)

    q2 = q_ref[0]
    km_parts = _split3(kmean_ref[...])
    blk_row = lax.broadcasted_iota(jnp.int32, (_MB_MAXBLK, blk), 0)
    blk_row_f = blk_row.astype(F32)
    for hh in range(HEADS_PER_BLOCK):
        in_head = (lane >= hh * HEAD_DIM) & (lane < (hh + 1) * HEAD_DIM)
        qm = jnp.where(in_head, q2, jnp.zeros_like(q2))
        gate = _dot_nt(qm, km_parts[0]) + _dot_nt(qm, km_parts[1]) + _dot_nt(qm, km_parts[2])
        gate = jnp.transpose(gate)[0:_MB_MAXBLK, :]
        gate = jnp.where(blk_row < qi, gate, NEG_INF)
        gate = jnp.where(blk_row < n_blk, gate, -jnp.inf)
        sel = jnp.zeros((_MB_MAXBLK, blk), jnp.bool_)
        for _ in range(MOBA_TOPK):
            gmax = jnp.max(gate, axis=0, keepdims=True)
            first = jnp.min(jnp.where(gate == gmax, blk_row_f, float(LANES)), axis=0, keepdims=True)
            pick = blk_row_f == first
            sel = sel | pick
            gate = jnp.where(pick, -jnp.inf, gate)
        sel = sel & (blk_row < qi)
        sel_bias = jnp.where(sel | (blk_row >= n_blk), 0.0, NEG_INF)
        extra = jnp.transpose(jnp.concatenate(
            [sel_bias, jnp.zeros((LANES - _MB_MAXBLK, blk), F32)], axis=0))
        extra = jnp.where((lane == _MB_POS0 + 2 * hh) | (lane == _MB_POS0 + 2 * hh + 1) | (lane == _MB_KILL),
                          1.0, extra)
        blk_lane = lane - (_MB_BLK0 + 32 * hh)
        extra = jnp.where((blk_lane >= 0) & (blk_lane < 32),
                          slopes[hh] * (blk_lane * blk).astype(F32), extra)
        qaug_ref[hh] = jnp.concatenate([qm * jnp.asarray(SCALE, BF16), extra.astype(BF16)], axis=1)
        m_ref[hh] = jnp.full((blk, LANES), -jnp.inf, F32)
        acc_ref[hh] = jnp.zeros((blk, LANES), F32)

    lane1 = lax.broadcasted_iota(jnp.int32, (1, LANES), 1)
    rc = _ROW_CHUNK

    def score_stage(j, buf, *, is_own=False, valid=True):
        kj = k_ref[0, pl.ds(pl.multiple_of(j * blk, blk), blk), :]
        ind = (lane1 % 32 == j) & (lane1 >= _MB_BLK0) if is_own else \
              (lane1 % 32 == j) & ((lane1 < 32) | (lane1 >= _MB_BLK0))
        kill = jnp.where(lane1 == _MB_KILL, jnp.where(valid, 0.0, NEG_INF), 0.0)
        side = jnp.broadcast_to(jnp.where(ind, 1.0, kill), (blk, LANES)).astype(BF16)
        kx = jnp.where(ind | (lane1 == _MB_KILL), side, kx_ref[...])
        k_aug = jnp.concatenate([kj, kx], axis=1)
        s = _dot_nt(qaug_ref[...].reshape(HEADS_PER_BLOCK * blk, 2 * LANES), k_aug)
        s_ref[buf] = s.reshape(HEADS_PER_BLOCK, blk, blk)

    def softmax_stage(buf, *, is_own=False):
        for hh in range(HEADS_PER_BLOCK):
            for c in range(blk // rc):
                rows = slice(c * rc, (c + 1) * rc)
                s = s_ref[buf, hh, rows, :]
                if is_own:
                    col_c = lax.broadcasted_iota(jnp.int32, (rc, blk), 1)
                    row_c = lax.broadcasted_iota(jnp.int32, (rc, blk), 0) + c * rc
                    s = jnp.where(col_c <= row_c, s, NEG_INF)
                m_old = m_ref[hh, rows, :]
                m_new = jnp.maximum(m_old, jnp.max(s, axis=1, keepdims=True))
                alpha_ref[buf, hh, rows, :] = jnp.exp(m_old - m_new)
                m_ref[hh, rows, :] = m_new
                p = jnp.exp(s - jnp.concatenate([m_new, m_new], axis=1))
                p_ref[buf, hh, rows, :] = p.astype(BF16)

    def value_stage(j, buf):
        vj = v_ref[0, pl.ds(pl.multiple_of(j * blk, blk), blk), :]
        v_aug = jnp.concatenate(
            [jnp.where((lane1 >= hh * HEAD_DIM) & (lane1 < (hh + 1) * HEAD_DIM), vj, jnp.ones_like(vj))
             for hh in range(HEADS_PER_BLOCK)], axis=1)
        pv = _dot(p_ref[buf].reshape(HEADS_PER_BLOCK * blk, blk), v_aug)
        for hh in range(HEADS_PER_BLOCK):
            acc_ref[hh] = alpha_ref[buf, hh] * acc_ref[hh] + pv[hh * blk:(hh + 1) * blk, hh * LANES:(hh + 1) * LANES]

    n_past = qi

    @pl.when(n_past == 0)
    def _():
        score_stage(qi, 0, is_own=True)
        softmax_stage(0, is_own=True)
        value_stage(qi, 0)

    @pl.when(n_past > 0)
    def _():
        last = n_past - 1

        def past(t):
            return jnp.clip(last - t, 0, last)

        score_stage(qi, 1, is_own=True)
        score_stage(past(0), 0)
        softmax_stage(1, is_own=True)
        value_stage(qi, 1)
        softmax_stage(0)
        score_stage(past(1), 1, valid=1 < n_past)

        reach = jnp.zeros((1, LANES), F32)
        for hh in range(HEADS_PER_BLOCK):
            in_head = (lane >= hh * HEAD_DIM) & (lane < (hh + 1) * HEAD_DIM)
            qf = q2.astype(F32)
            qn2 = jnp.sum(jnp.where(in_head, qf * qf, 0.0), axis=1, keepdims=True)
            qn2 = jnp.max(jnp.broadcast_to(qn2, (blk, LANES)), axis=0, keepdims=True)
            bound = jnp.sqrt(qn2 * kn2_ref[hh, 0:1, :]) * SCALE
            m_min = jnp.min(m_ref[hh], axis=0, keepdims=True)
            reach = jnp.maximum(reach, (bound - m_min + MOBA_DEAD_GAP) / slopes[hh])
        n_past_f = jnp.full((1, LANES), n_past, jnp.int32).astype(F32)
        n_live = jnp.max(jnp.minimum((reach - 1.0) / blk + 1.0, n_past_f))

        def pair(state):
            tt, tf = state
            t = 2 * tt
            score_stage(past(t), 0, valid=t < n_past)
            softmax_stage(1)
            value_stage(past(t - 2), 0)
            score_stage(past(t + 1), 1, valid=t + 1 < n_past)
            softmax_stage(0)
            value_stage(past(t - 1), 1)
            return tt + 1, tf + 2.0

        lax.while_loop(lambda st: st[1] < n_live + 2.0, pair, (jnp.int32(1), jnp.float32(2.0)))

    acc0 = acc_ref[0]
    acc1 = acc_ref[1]
    o0 = acc0 / pltpu.roll(acc0, HEAD_DIM, axis=1)
    o1 = acc1 / pltpu.roll(acc1, HEAD_DIM, axis=1)
    o_ref[0] = jnp.where(lane < HEAD_DIM, o0, o1).astype(BF16)


def _moba_attention(qkv):
    b, s, _ = qkv.shape
    blk = MOBA_BLOCK
    n_blk = s // blk
    assert s % blk == 0 and MOBA_TOPK <= n_blk - 1 and n_blk <= 32
    n_pairs = N_HEADS_MOBA // HEADS_PER_BLOCK
    return pl.pallas_call(
        functools.partial(_moba_kernel, n_blk=n_blk),
        out_shape=jax.ShapeDtypeStruct((b, s, N_HEADS_MOBA * HEAD_DIM), BF16),
        grid=(b, n_pairs, n_blk),
        in_specs=[pl.BlockSpec((1, blk, LANES), lambda bi, p, i: (bi, i, Q_BLK0 + p)),
                  pl.BlockSpec((1, s, LANES), lambda bi, p, i: (bi, 0, K_BLK0 + p)),
                  pl.BlockSpec((1, s, LANES), lambda bi, p, i: (bi, 0, V_BLK0 + p))],
        out_specs=pl.BlockSpec((1, blk, LANES), lambda bi, p, i: (bi, i, p)),
        scratch_shapes=[pltpu.VMEM((LANES, LANES), F32),
                        pltpu.VMEM((HEADS_PER_BLOCK, 8, LANES), F32),
                        pltpu.VMEM((blk, LANES), BF16),
                        pltpu.VMEM((HEADS_PER_BLOCK, blk, 2 * LANES), BF16),
                        pltpu.VMEM((HEADS_PER_BLOCK, blk, LANES), F32),
                        pltpu.VMEM((2, HEADS_PER_BLOCK, blk, LANES), F32),
                        pltpu.VMEM((HEADS_PER_BLOCK, blk, LANES), F32),
                        pltpu.VMEM((2, HEADS_PER_BLOCK, blk, blk), F32),
                        pltpu.VMEM((2, HEADS_PER_BLOCK, blk, blk), BF16)],
        compiler_params=_params(3),
        name="moba_attn",
    )(qkv, qkv, qkv)


SB_TILE = 256


def _softplus(z):
    return jnp.maximum(z, 0.0) + jnp.log(1.0 + jnp.exp(-jnp.abs(z)))


_SB_SLOTS = 3
SB_DEAD_MASS = 128.0


def _sb_kernel(q_ref, k_ref, v_ref, u_ref, o_ref, qaug_ref, carry_ref, acc_ref,
               z_ref, sphl_ref, c_ref, rs_ref, a_ref):
    qi = pl.program_id(2)
    t = SB_TILE
    rc = _ROW_CHUNK
    lane = lax.broadcasted_iota(jnp.int32, (t, LANES), 1)
    lane1 = lax.broadcasted_iota(jnp.int32, (1, LANES), 1)
    q2 = q_ref[0]
    for hh in range(HEADS_PER_BLOCK):
        in_head = (lane >= hh * HEAD_DIM) & (lane < (hh + 1) * HEAD_DIM)
        qm = jnp.where(in_head, q2, jnp.zeros_like(q2)) * jnp.asarray(SCALE, BF16)
        qaug_ref[hh] = jnp.concatenate([qm, jnp.where(lane == 0, 1.0, 0.0).astype(BF16)], axis=1)

    def score_stage(j, slot, *, valid=True):
        kj = k_ref[0, pl.ds(pl.multiple_of(j * t, t), t), :]
        kill = jnp.where(lane1 == 0, jnp.where(valid, 0.0, NEG_INF), 0.0)
        k_aug = jnp.concatenate([kj, jnp.broadcast_to(kill, (t, LANES)).astype(BF16)], axis=1)
        z = _dot_nt(qaug_ref[...].reshape(HEADS_PER_BLOCK * t, 2 * LANES), k_aug)
        z_ref[slot] = z.reshape(HEADS_PER_BLOCK, t, t)

    def _past_mask(c):
        col_c = lax.broadcasted_iota(jnp.int32, (rc, t), 1)
        row_c = lax.broadcasted_iota(jnp.int32, (rc, t), 0) + c * rc
        return col_c < row_c

    def softplus_stage(slot, *, diagonal=False):
        for hh in range(HEADS_PER_BLOCK):
            for c in range(t // rc):
                rows = slice(c * rc, (c + 1) * rc)
                sp = _softplus(z_ref[slot, hh, rows, :])
                if diagonal:
                    sp = jnp.where(_past_mask(c), sp, 0.0)
                hi, lo = _split2(sp)
                sphl_ref[slot, pl.ds((2 * hh) * t + c * rc, rc), :] = hi
                sphl_ref[slot, pl.ds((2 * hh + 1) * t + c * rc, rc), :] = lo
                rs_ref[slot, hh, rows, :] = jnp.broadcast_to(jnp.sum(sp, axis=1, keepdims=True), (rc, LANES))

    def suffix_stage(slot):
        c_ref[slot] = _dot(sphl_ref[slot], u_ref[...])

    def weight_stage(slot, *, diagonal=False):
        for hh in range(HEADS_PER_BLOCK):
            for c in range(t // rc):
                rows = slice(c * rc, (c + 1) * rc)
                z = z_ref[slot, hh, rows, :]
                cc = c_ref[slot, pl.ds((2 * hh) * t + c * rc, rc), :] + \
                    c_ref[slot, pl.ds((2 * hh + 1) * t + c * rc, rc), :]
                if diagonal:
                    a = jnp.where(_past_mask(c), jnp.exp(z - cc), 0.0)
                    carry_ref[hh, rows, :] = rs_ref[slot, hh, rows, :]
                else:
                    carry = carry_ref[hh, rows, :]
                    a = jnp.exp(z - (cc + jnp.concatenate([carry, carry], axis=1)))
                    carry_ref[hh, rows, :] = carry + rs_ref[slot, hh, rows, :]
                a_ref[slot, hh, rows, :] = a.astype(BF16)

    def value_stage(j, slot, *, first=False):
        vj = v_ref[0, pl.ds(pl.multiple_of(j * t, t), t), :]
        av = _dot(a_ref[slot].reshape(HEADS_PER_BLOCK * t, t), vj)
        for hh in range(HEADS_PER_BLOCK):
            av_h = av[hh * t:(hh + 1) * t, :]
            acc_ref[hh] = av_h if first else acc_ref[hh] + av_h

    n_past = qi

    @pl.when(n_past == 0)
    def _():
        score_stage(qi, 0)
        softplus_stage(0, diagonal=True)
        suffix_stage(0)
        weight_stage(0, diagonal=True)
        value_stage(qi, 0, first=True)

    @pl.when(n_past > 0)
    def _():
        last = n_past - 1

        def key_tile(i):
            return jnp.clip(last - i, 0, last)

        score_stage(qi, 2)
        score_stage(key_tile(0), 0)
        softplus_stage(2, diagonal=True)
        suffix_stage(2)
        softplus_stage(0)
        suffix_stage(0)
        weight_stage(2, diagonal=True)
        value_stage(qi, 2, first=True)
        weight_stage(0)
        value_stage(key_tile(0), 0)

        def trip(tt, first=False):
            for k in range(_SB_SLOTS):
                i = 1 + _SB_SLOTS * tt + k
                score_stage(key_tile(i), (1 + k) % _SB_SLOTS, valid=i < n_past)
                if not (first and k < 1):
                    softplus_stage(k % _SB_SLOTS)
                    suffix_stage(k % _SB_SLOTS)
                if not (first and k < 2):
                    weight_stage((k - 1) % _SB_SLOTS)
                    value_stage(key_tile(i - 2), (k - 1) % _SB_SLOTS)
            return tt + 1, jnp.min(carry_ref[...])

        @pl.when((n_past > 1) & (jnp.min(carry_ref[...]) < SB_DEAD_MASS))
        def _():
            n_trips = (n_past - 1 + 2 + _SB_SLOTS - 1) // _SB_SLOTS
            lax.while_loop(lambda st: (st[0] < n_trips) & (st[1] < SB_DEAD_MASS), lambda st: trip(st[0]),
                           trip(jnp.int32(0), first=True))

    o_ref[0] = jnp.where(lane < HEAD_DIM, acc_ref[0], acc_ref[1]).astype(BF16)


def _sb_attention(qkv):
    b, s, _ = qkv.shape
    t = SB_TILE
    assert s % t == 0
    n_pairs = N_HEADS_SB // HEADS_PER_BLOCK
    u = (lax.broadcasted_iota(jnp.int32, (t, t), 0) >= lax.broadcasted_iota(jnp.int32, (t, t), 1)).astype(BF16)
    return pl.pallas_call(
        _sb_kernel,
        out_shape=jax.ShapeDtypeStruct((b, s, N_HEADS_SB * HEAD_DIM), BF16),
        grid=(b, n_pairs, s // t),
        in_specs=[pl.BlockSpec((1, t, LANES), lambda bi, p, i: (bi, i, Q_BLK0 + SB_BLK + p)),
                  pl.BlockSpec((1, s, LANES), lambda bi, p, i: (bi, 0, K_BLK0 + SB_BLK + p)),
                  pl.BlockSpec((1, s, LANES), lambda bi, p, i: (bi, 0, V_BLK0 + SB_BLK + p)),
                  pl.BlockSpec((t, t), lambda bi, p, i: (0, 0))],
        out_specs=pl.BlockSpec((1, t, LANES), lambda bi, p, i: (bi, i, p)),
        scratch_shapes=[pltpu.VMEM((HEADS_PER_BLOCK, t, 2 * LANES), BF16),
                        pltpu.VMEM((HEADS_PER_BLOCK, t, LANES), F32),
                        pltpu.VMEM((HEADS_PER_BLOCK, t, LANES), F32),
                        pltpu.VMEM((_SB_SLOTS, HEADS_PER_BLOCK, t, t), F32),
                        pltpu.VMEM((_SB_SLOTS, 2 * HEADS_PER_BLOCK * t, t), BF16),
                        pltpu.VMEM((_SB_SLOTS, 2 * HEADS_PER_BLOCK * t, t), F32),
                        pltpu.VMEM((_SB_SLOTS, HEADS_PER_BLOCK, t, LANES), F32),
                        pltpu.VMEM((_SB_SLOTS, HEADS_PER_BLOCK, t, t), BF16)],
        compiler_params=_params(3),
        name="sb_attn",
    )(qkv, qkv, qkv, u)


DIL_SPAN = 2048
_DIL_UNROLL = 16


def _dil_kernel(q_ref, k_ref, v_ref, kp_ref, vp_ref, o_ref, qf_ref, kf_ref, vf_ref, oc_ref, lse_ref, bias_ref):
    pair = pl.program_id(1)
    i = pl.program_id(2)
    n = DIL_N
    span = DIL_SPAN
    lane = lax.broadcasted_iota(jnp.int32, (n, LANES), 1)
    colh = lax.broadcasted_iota(jnp.int32, (n, 2 * n), 1)

    @pl.when(i == 0)
    def _():
        row = lax.broadcasted_iota(jnp.int32, (n, 2 * n), 0)
        delta = row + n - colh
        valid = (delta >= 0) & (delta <= n)
        for ci, (_, d) in enumerate(DIL_CONFIGS):
            for hh in range(HEADS_PER_BLOCK):
                slope = jnp.asarray(2.0 ** -(hh + 1), F32)
                for p in range(1, N_HEADS_DIL // HEADS_PER_BLOCK):
                    slope = jnp.where(pair == p, 2.0 ** -(2 * p + hh + 1), slope)
                bias_ref[ci, hh] = jnp.where(valid, -slope * (delta * d).astype(F32), NEG_INF)

    qf_ref[...] = q_ref[0].astype(F32)
    kf_ref[0:span, :] = kp_ref[0].astype(F32)
    kf_ref[span:2 * span, :] = k_ref[0].astype(F32)
    vf_ref[0:span, :] = vp_ref[0].astype(F32)
    vf_ref[span:2 * span, :] = v_ref[0].astype(F32)

    def rows(start, size, d):
        return pl.ds(start, size) if d == 1 else pl.ds(start, size, stride=d)

    def unit(ci, d, r, bl):
        q0 = r + bl * (n * d)
        qu = qf_ref[rows(q0, n, d), :].astype(BF16)
        ku = kf_ref[rows(span + q0 - n * d, 2 * n, d), :].astype(BF16)
        vu = vf_ref[rows(span + q0 - n * d, 2 * n, d), :].astype(BF16)
        no_prev = jnp.where((i == 0) & (bl == 0), NEG_INF, 0.0)
        qms = [jnp.where((lane >= hh * HEAD_DIM) & (lane < (hh + 1) * HEAD_DIM), qu, jnp.zeros_like(qu))
               * jnp.asarray(SCALE, BF16) for hh in range(HEADS_PER_BLOCK)]
        s_both = _dot_nt(jnp.concatenate(qms, axis=0), ku)
        ps, dens, lse_h = [], [], []
        for hh in range(HEADS_PER_BLOCK):
            s = s_both[hh * n:(hh + 1) * n, :] + bias_ref[ci, hh] + jnp.where(colh < n, no_prev, 0.0)
            m = jnp.max(s, axis=1, keepdims=True)
            p = jnp.exp(s - m)
            den = jnp.sum(p, axis=1, keepdims=True)
            ps.append(p.astype(BF16))
            dens.append(den)
            lse_h.append(jnp.broadcast_to(m + jnp.log(den), (n, LANES)))
        pv = _dot(jnp.concatenate(ps, axis=0), vu)
        o_h = [pv[hh * n:(hh + 1) * n, :] / dens[hh] for hh in range(HEADS_PER_BLOCK)]
        oc_ref[ci, rows(q0, n, d), :] = jnp.where(lane < HEAD_DIM, o_h[0], o_h[1])
        lse_ref[ci, rows(q0, n, d), :] = jnp.where(lane < HEAD_DIM, lse_h[0], lse_h[1])

    n_units = span // n
    for ci, (_, d) in enumerate(DIL_CONFIGS):
        per_res = n_units // d

        def group(g, c, ci=ci, d=d, per_res=per_res):
            for k in range(_DIL_UNROLL):
                u = g * _DIL_UNROLL + k
                unit(ci, d, u // per_res, u % per_res)
            return c

        lax.fori_loop(0, n_units // _DIL_UNROLL, group, 0)

    mc = 256
    for c0 in range(0, span, mc):
        l1, l2, l3 = (lse_ref[ci, c0:c0 + mc, :] for ci in range(3))
        lmax = jnp.maximum(jnp.maximum(l1, l2), l3)
        e1, e2, e3 = jnp.exp(l1 - lmax), jnp.exp(l2 - lmax), jnp.exp(l3 - lmax)
        mix = (e1 * oc_ref[0, c0:c0 + mc, :] + e2 * oc_ref[1, c0:c0 + mc, :] + e3 * oc_ref[2, c0:c0 + mc, :]) \
            / (e1 + e2 + e3)
        o_ref[0, c0:c0 + mc, :] = mix.astype(BF16)


def _dilated_mixture(qkv):
    b, s, _ = qkv.shape
    span = DIL_SPAN
    assert s % span == 0 and all(w <= span and span % (DIL_N * d) == 0 for w, d in DIL_CONFIGS)
    n_pairs = N_HEADS_DIL // HEADS_PER_BLOCK
    n_cfg = len(DIL_CONFIGS)

    def cur(off):
        return lambda bi, p, i: (bi, i, off + DIL_BLK + p)

    def prev(off):
        return lambda bi, p, i: (bi, jnp.maximum(i - 1, 0), off + DIL_BLK + p)

    blk = (1, span, LANES)
    return pl.pallas_call(
        _dil_kernel,
        out_shape=jax.ShapeDtypeStruct((b, s, N_HEADS_DIL * HEAD_DIM), BF16),
        grid=(b, n_pairs, s // span),
        in_specs=[pl.BlockSpec(blk, cur(Q_BLK0)), pl.BlockSpec(blk, cur(K_BLK0)), pl.BlockSpec(blk, cur(V_BLK0)),
                  pl.BlockSpec(blk, prev(K_BLK0)), pl.BlockSpec(blk, prev(V_BLK0))],
        out_specs=pl.BlockSpec(blk, lambda bi, p, i: (bi, i, p)),
        scratch_shapes=[pltpu.VMEM((span, LANES), F32),
                        pltpu.VMEM((2 * span, LANES), F32),
                        pltpu.VMEM((2 * span, LANES), F32),
                        pltpu.VMEM((n_cfg, span, LANES), F32),
                        pltpu.VMEM((n_cfg, span, LANES), F32),
                        pltpu.VMEM((n_cfg, HEADS_PER_BLOCK, DIL_N, 2 * DIL_N), F32)],
        compiler_params=_params(3),
        name="dilated_attn",
    )(qkv, qkv, qkv, qkv, qkv)


def _rms(x, g):
    return x * lax.rsqrt(jnp.mean(x * x, axis=-1, keepdims=True) + NORM_EPS) * g


def _outproj_kernel(x_ref, oa_ref, ob_ref, oc_ref, g_ref, w_ref, out_ref):
    g = g_ref[...]
    wa = N_HEADS_MOBA * HEAD_DIM
    wb = wa + N_HEADS_SB * HEAD_DIM
    rows = 256
    for c in range(x_ref.shape[0] // rows):
        r = slice(c * rows, (c + 1) * rows)
        y = jnp.concatenate([_rms(oa_ref[r, :].astype(F32), g[:, :wa]),
                             _rms(ob_ref[r, :].astype(F32), g[:, wa:wb]),
                             _rms(oc_ref[r, :].astype(F32), g[:, wb:])], axis=1).astype(BF16)
        out_ref[r, :] = x_ref[r, :] + _dot(y, w_ref[...])


def _out_proj(x2, oa, ob, oc, g, w, *, tm=1024):
    t, d = x2.shape
    row = lambda i: (i, 0)
    const = lambda i: (0, 0)
    return pl.pallas_call(
        _outproj_kernel,
        out_shape=jax.ShapeDtypeStruct((t, d), F32),
        grid=(t // tm,),
        in_specs=[pl.BlockSpec((tm, d), row),
                  pl.BlockSpec((tm, oa.shape[1]), row), pl.BlockSpec((tm, ob.shape[1]), row),
                  pl.BlockSpec((tm, oc.shape[1]), row),
                  pl.BlockSpec((1, d), const), pl.BlockSpec((d, d), const)],
        out_specs=pl.BlockSpec((tm, d), row),
        compiler_params=_params(1),
        name="out_proj",
    )(x2, oa, ob, oc, g.reshape(1, d), w)


_EXP_LANE0 = N_GROUPS
_MOE_ROWS = 256
_MOE_CHUNK = 128


def _moe_kernel(x_ref, g_ref, wrh_ref, wrl_ref, br_ref, ltri_ref, fg_ref, wg_ref, wu_ref, wd_ref, out_ref,
                h_ref, hs_ref, cws_ref, ys_ref, pt_ref, pos_ref, nck_ref, cb_ref, *, n_sorted, final_norm):
    e = pl.program_id(1)
    tm = x_ref.shape[0]
    lane = lax.broadcasted_iota(jnp.int32, (tm, LANES), 1)
    lane1 = lax.broadcasted_iota(jnp.int32, (1, LANES), 1)

    @pl.when(e == 0)
    def _():
        x = x_ref[...]
        h = _rms(x, g_ref[...])
        hh, hl = _split2(h)
        logits = _dot(hh, wrh_ref[...]) + _dot(hh, wrl_ref[...]) + _dot(hl, wrh_ref[...]) + br_ref[...]
        lane_f = lane.astype(F32)
        big = float(LANES)
        gl = jnp.where(lane < N_GROUPS, logits, -jnp.inf)
        gmax = jnp.max(gl, axis=1, keepdims=True)
        gidx = jnp.min(jnp.where(gl == gmax, lane_f, big), axis=1, keepdims=True)
        g_w = 1.0 / jnp.sum(jnp.exp(gl - gmax), axis=1, keepdims=True)
        lane_group = ((lane - _EXP_LANE0) // EXPERTS_PER_GROUP).astype(F32)
        in_group = (lane >= _EXP_LANE0) & (lane < _EXP_LANE0 + N_EXPERTS) & (lane_group == gidx)
        el = jnp.where(in_group, logits, -jnp.inf)
        v1 = jnp.max(el, axis=1, keepdims=True)
        i1 = jnp.min(jnp.where(el == v1, lane_f, big), axis=1, keepdims=True)
        el2 = jnp.where(lane_f == i1, -jnp.inf, el)
        v2 = jnp.max(el2, axis=1, keepdims=True)
        i2 = jnp.min(jnp.where(el2 == v2, lane_f, big), axis=1, keepdims=True)
        r = jnp.exp(v2 - v1)
        w1 = g_w / (1.0 + r)
        w2 = g_w * r / (1.0 + r)
        comb = jnp.where(lane_f == i1, w1, 0.0) + jnp.where(lane_f == i2, w2, 0.0)

        onehot = jnp.where((lane_f == gidx) & (lane < N_GROUPS), 1.0, 0.0)
        before = _dot(ltri_ref[...], onehot.astype(BF16))
        rank = jnp.sum(onehot * before, axis=1, keepdims=True)
        count = jnp.sum(onehot, axis=0, keepdims=True)
        chunks = jnp.floor((count + (_MOE_CHUNK - 1.0)) * (1.0 / _MOE_CHUNK))
        nck_ref[...] = jnp.broadcast_to(chunks, nck_ref.shape)
        pos = rank
        start = jnp.float32(0.0)
        for gi in range(N_GROUPS):
            pos = pos + jnp.where(gidx == float(gi), start, 0.0)
            start = start + jnp.max(jnp.where(lane1 == gi, chunks, 0.0)) * _MOE_CHUNK
        pos_rep = jnp.broadcast_to(pos, (tm, LANES))
        pos_ref[...] = pos_rep
        pos_t = jnp.transpose(pos_rep)[0:1, :]
        h_ref[...] = hh
        c_hi, c_lo = _split2(comb)
        for c in range(n_sorted // _MOE_ROWS):
            rows = slice(c * _MOE_ROWS, (c + 1) * _MOE_ROWS)
            slot = (lax.broadcasted_iota(jnp.int32, (_MOE_ROWS, tm), 0) + c * _MOE_ROWS).astype(F32)
            perm = jnp.where(slot == pos_t, 1.0, 0.0).astype(BF16)
            hs_ref[rows, :] = _dot(perm, h_ref[...]).astype(BF16)
            cws_ref[rows, :] = _dot(perm, c_hi) + _dot(perm, c_lo)
        ys_ref[...] = jnp.zeros_like(ys_ref)
        cb_ref[0] = 0

    lane_c = lax.broadcasted_iota(jnp.int32, (_MOE_CHUNK, LANES), 1)
    n_chunks = jnp.max(jnp.where(lane1 == e, nck_ref[0:1, :], 0.0))

    def chunk(state):
        cb, cf = state
        rows = pl.ds(pl.multiple_of(cb * _MOE_CHUNK, _MOE_CHUNK), _MOE_CHUNK)
        hc = hs_ref[rows, :]
        cwc = cws_ref[rows, :]
        acts = []
        for k in range(EXPERTS_PER_GROUP):
            gate = _dot(hc, wg_ref[0, k])
            up = _dot(hc, wu_ref[0, k])
            cw = jnp.sum(jnp.where(lane_c == _EXP_LANE0 + e * EXPERTS_PER_GROUP + k, cwc, 0.0),
                         axis=1, keepdims=True)
            acts.append((gate / (1.0 + jnp.exp(-gate)) * up * cw).astype(BF16))
        ys_ref[rows, :] = _dot(jnp.concatenate(acts, axis=1), wd_ref[0]).astype(BF16)
        return cb + 1, cf + 1.0

    cb_end, _ = lax.while_loop(lambda st: st[1] < n_chunks, chunk, (cb_ref[0], jnp.float32(0.0)))
    cb_ref[0] = cb_end

    @pl.when(e == N_GROUPS - 1)
    def _():
        pos_rep2 = jnp.concatenate([pos_ref[...], pos_ref[...]], axis=1)
        for c in range(n_sorted // _MOE_ROWS):
            slot = (lax.broadcasted_iota(jnp.int32, (tm, _MOE_ROWS), 1) + c * _MOE_ROWS).astype(F32)
            perm_t = jnp.where(slot == pos_rep2, 1.0, 0.0).astype(BF16)
            pt_ref[:, c * _MOE_ROWS:(c + 1) * _MOE_ROWS] = perm_t
        y = x_ref[...] + _dot(pt_ref[...], ys_ref[...])
        out_ref[...] = _rms(y, fg_ref[...]) if final_norm else y


def _moe(x2, g, wr_hi, wr_lo, br, wg, wu, wd, final_g, *, final_norm, tm=1024):
    t, d = x2.shape
    f = wg.shape[3]
    n_sorted = tm + N_GROUPS * _MOE_CHUNK
    assert n_sorted % _MOE_ROWS == 0
    ltri = (lax.broadcasted_iota(jnp.int32, (tm, tm), 0) > lax.broadcasted_iota(jnp.int32, (tm, tm), 1)).astype(BF16)
    row = lambda i, e: (i, 0)
    const = lambda i, e: (0, 0)
    return pl.pallas_call(
        functools.partial(_moe_kernel, n_sorted=n_sorted, final_norm=final_norm),
        out_shape=jax.ShapeDtypeStruct((t, d), F32),
        grid=(t // tm, N_GROUPS),
        in_specs=[pl.BlockSpec((tm, d), row),
                  pl.BlockSpec((1, d), const),
                  pl.BlockSpec((d, LANES), const), pl.BlockSpec((d, LANES), const),
                  pl.BlockSpec((1, LANES), const),
                  pl.BlockSpec((tm, tm), const),
                  pl.BlockSpec((1, d), const),
                  pl.BlockSpec((1, EXPERTS_PER_GROUP, d, f), lambda i, e: (e, 0, 0, 0)),
                  pl.BlockSpec((1, EXPERTS_PER_GROUP, d, f), lambda i, e: (e, 0, 0, 0)),
                  pl.BlockSpec((1, EXPERTS_PER_GROUP * f, d), lambda i, e: (e, 0, 0))],
        out_specs=pl.BlockSpec((tm, d), row),
        scratch_shapes=[pltpu.VMEM((tm, d), BF16),
                        pltpu.VMEM((n_sorted, d), BF16),
                        pltpu.VMEM((n_sorted, LANES), F32),
                        pltpu.VMEM((n_sorted, d), BF16),
                        pltpu.VMEM((tm, n_sorted), BF16),
                        pltpu.VMEM((tm, LANES), F32),
                        pltpu.VMEM((8, LANES), F32),
                        pltpu.SMEM((1,), jnp.int32)],
        compiler_params=_params(2),
        name="hier_moe",
    )(x2, g.reshape(1, d), wr_hi, wr_lo, br, ltri, final_g.reshape(1, d), wg, wu, wd)


def _router_weights(w_gr, b_gr, w_er, b_er):
    d = w_gr.shape[0]
    w = jnp.concatenate([w_gr, jnp.moveaxis(w_er, 0, 1).reshape(d, N_EXPERTS)], axis=1)
    w = jnp.pad(w, ((0, 0), (0, LANES - w.shape[1])))
    bias = jnp.pad(jnp.concatenate([b_gr, b_er.reshape(-1)]), (0, LANES - N_GROUPS - N_EXPERTS))
    hi, lo = _split2(w)
    return hi, lo, bias.reshape(1, LANES)


def _layer(x2, b, s, ln1_g, w_in, mix_g, w_out, ln2_g, w_gr, b_gr, w_er, b_er, w_gate, w_up, w_down,
           final_g, final_norm):
    t, d = x2.shape
    qkv = _qkv_proj(x2, ln1_g, w_in.astype(BF16)).reshape(b, s, 3 * d)
    oa = _moba_attention(qkv)
    ob = _sb_attention(qkv)
    oc = _dilated_mixture(qkv)
    x2 = _out_proj(x2, oa.reshape(t, -1), ob.reshape(t, -1), oc.reshape(t, -1), mix_g, w_out.astype(BF16))
    wr_hi, wr_lo, br = _router_weights(w_gr, b_gr, w_er, b_er)
    f = w_gate.shape[-1]
    return _moe(x2, ln2_g, wr_hi, wr_lo, br, w_gate.astype(BF16), w_up.astype(BF16),
                w_down.reshape(N_GROUPS, EXPERTS_PER_GROUP * f, d).astype(BF16), final_g, final_norm=final_norm)


def kernel(x, ln1_g, w_in, mix_norm_g, w_out, ln2_g, w_group_router, b_group_router,
           w_expert_router, b_expert_router, w_gate, w_up, w_down, final_norm_g):
    b, s, d = x.shape
    x2 = x.reshape(b * s, d)
    depth = ln1_g.shape[0]
    for l in range(depth):
        x2 = _layer(x2, b, s, ln1_g[l], w_in[l], mix_norm_g[l], w_out[l], ln2_g[l],
                    w_group_router[l], b_group_router[l], w_expert_router[l], b_expert_router[l],
                    w_gate[l], w_up[l], w_down[l], final_norm_g, final_norm=(l == depth - 1))
    return x2.reshape(b, s, d)
```

```python
import functools

import jax
import jax.numpy as jnp
from jax import lax
from jax.experimental import pallas as pl
from jax.experimental.pallas import tpu as pltpu

F32 = jnp.float32
BF16 = jnp.bfloat16

D_MODEL = 1024
HEAD_DIM = 64
N_HEADS = 16
LANES = 128
HEADS_PER_BLOCK = LANES // HEAD_DIM
N_HEADS_MOBA = 4
N_HEADS_SB = 4
N_HEADS_DIL = 8
MOBA_BLOCK = 256
MOBA_TOPK = 3
DIL_CONFIGS = ((128, 1), (512, 4), (2048, 16))
DIL_N = 128
N_GROUPS = 4
EXPERTS_PER_GROUP = 4
N_EXPERTS = N_GROUPS * EXPERTS_PER_GROUP
D_EXPERT = 256
NORM_EPS = 1e-6
NEG_INF = -1e30
SCALE = HEAD_DIM ** -0.5

Q_BLK0 = 0
K_BLK0 = D_MODEL // LANES
V_BLK0 = 2 * D_MODEL // LANES
ROW_BLKS = 3 * D_MODEL // LANES
SB_BLK = N_HEADS_MOBA // HEADS_PER_BLOCK
DIL_BLK = (N_HEADS_MOBA + N_HEADS_SB) // HEADS_PER_BLOCK

VMEM_LIMIT = 56 * 1024 * 1024
MOE_VMEM_LIMIT = 62 * 1024 * 1024


def _params(n_axes, vmem=VMEM_LIMIT):
    return pltpu.CompilerParams(dimension_semantics=("arbitrary",) * n_axes,
                                vmem_limit_bytes=vmem)


def _dot_nt(a, b):
    return lax.dot_general(a, b, (((1,), (1,)), ((), ())), preferred_element_type=F32)


def _dot(a, b):
    return jnp.dot(a, b, preferred_element_type=F32)


def _split3(x):
    hi = x.astype(BF16)
    r1 = x - hi.astype(F32)
    mid = r1.astype(BF16)
    lo = (r1 - mid.astype(F32)).astype(BF16)
    return hi, mid, lo


def _split2(x):
    hi = x.astype(BF16)
    lo = (x - hi.astype(F32)).astype(BF16)
    return hi, lo


def _qkv_kernel(x_ref, g_ref, w_ref, o_ref, *, rows, tn):
    for c in range(x_ref.shape[0] // rows):
        r = slice(c * rows, (c + 1) * rows)
        x = x_ref[r, :]
        ms = jnp.mean(x * x, axis=-1, keepdims=True)
        h = (x * lax.rsqrt(ms + NORM_EPS) * g_ref[...]).astype(BF16)
        for j in range(w_ref.shape[1] // tn):
            o_ref[r, j * tn:(j + 1) * tn] = _dot(h, w_ref[:, j * tn:(j + 1) * tn]).astype(BF16)


def _qkv_proj(x2, g, w, *, tm=512, rows=256, tn=1024):
    t, d = x2.shape
    n = w.shape[1]
    return pl.pallas_call(
        functools.partial(_qkv_kernel, rows=rows, tn=tn),
        out_shape=jax.ShapeDtypeStruct((t, n), BF16),
        grid=(t // tm,),
        in_specs=[pl.BlockSpec((tm, d), lambda i: (i, 0)),
                  pl.BlockSpec((1, d), lambda i: (0, 0)),
                  pl.BlockSpec((d, n), lambda i: (0, 0))],
        out_specs=pl.BlockSpec((tm, n), lambda i: (i, 0)),
        compiler_params=_params(1),
        name="qkv_proj",
    )(x2, g.reshape(1, d), w)


_MB_SEL0 = 0
_MB_POS0 = 32
_MB_KILL = 40
_MB_MAXBLK = 32
_MB_BLK0 = 64


_ROW_CHUNK = 32


MOBA_DEAD_GAP = 110.0


def _moba_kernel(q_ref, k_ref, v_ref, o_ref, kmean_ref, kn2_ref, kx_ref, qaug_ref, m_ref, alpha_ref, acc_ref,
                 s_ref, p_ref, *, n_blk):
    pair = pl.program_id(1)
    qi = pl.program_id(2)
    blk = MOBA_BLOCK
    lane = lax.broadcasted_iota(jnp.int32, (blk, LANES), 1)
    row = lax.broadcasted_iota(jnp.int32, (blk, LANES), 0)
    lane_f = lane.astype(F32)
    slopes = [jnp.where(pair == 0, 2.0 ** (-2 * (hh + 1)), 2.0 ** (-2 * (hh + 3))).astype(F32)
              for hh in range(HEADS_PER_BLOCK)]

    @pl.when(qi == 0)
    def _():
        kmean_ref[...] = jnp.zeros_like(kmean_ref)
        kn2_ref[...] = jnp.zeros_like(kn2_ref)

        def body(n, c):
            kb = k_ref[0, pl.ds(pl.multiple_of(n * blk, blk), blk), :].astype(F32)
            kmean_ref[pl.ds(n, 1), :] = jnp.sum(kb, axis=0, keepdims=True) * (1.0 / blk)
            for hh in range(HEADS_PER_BLOCK):
                in_head = (lane >= hh * HEAD_DIM) & (lane < (hh + 1) * HEAD_DIM)
                n2 = jnp.sum(jnp.where(in_head, kb * kb, 0.0), axis=1, keepdims=True)
                n2 = jnp.max(jnp.broadcast_to(n2, (blk, LANES)), axis=0, keepdims=True)
                kn2_ref[hh] = jnp.maximum(kn2_ref[hh], jnp.broadcast_to(n2, (8, LANES)))
            return c

        lax.fori_loop(0, n_blk, body, 0)
        kx = jnp.zeros((blk, LANES), F32)
        for hh in range(HEADS_PER_BLOCK):
            kx = jnp.where(lane == _MB_POS0 + 2 * hh, slopes[hh] * ((row // LANES) * LANES).astype(F32), kx)
            kx = jnp.where(lane == _MB_POS0 + 2 * hh + 1, slopes[hh] * (row % LANES).astype(F32), kx)
        kx_ref[...] = kx.astype(BF16)

    q2 = q_ref[0]
    km_parts = _split3(kmean_ref[...])
    blk_row = lax.broadcasted_iota(jnp.int32, (_MB_MAXBLK, blk), 0)
    blk_row_f = blk_row.astype(F32)
    for hh in range(HEADS_PER_BLOCK):
        in_head = (lane >= hh * HEAD_DIM) & (lane < (hh + 1) * HEAD_DIM)
        qm = jnp.where(in_head, q2, jnp.zeros_like(q2))
        gate = _dot_nt(qm, km_parts[0]) + _dot_nt(qm, km_parts[1]) + _dot_nt(qm, km_parts[2])
        gate = jnp.transpose(gate)[0:_MB_MAXBLK, :]
        gate = jnp.where(blk_row < qi, gate, NEG_INF)
        gate = jnp.where(blk_row < n_blk, gate, -jnp.inf)
        sel = jnp.zeros((_MB_MAXBLK, blk), jnp.bool_)
        for _ in range(MOBA_TOPK):
            gmax = jnp.max(gate, axis=0, keepdims=True)
            first = jnp.min(jnp.where(gate == gmax, blk_row_f, float(LANES)), axis=0, keepdims=True)
            pick = blk_row_f == first
            sel = sel | pick
            gate = jnp.where(pick, -jnp.inf, gate)
        sel = sel & (blk_row < qi)
        sel_bias = jnp.where(sel | (blk_row >= n_blk), 0.0, NEG_INF)
        extra = jnp.transpose(jnp.concatenate(
            [sel_bias, jnp.zeros((LANES - _MB_MAXBLK, blk), F32)], axis=0))
        extra = jnp.where((lane == _MB_POS0 + 2 * hh) | (lane == _MB_POS0 + 2 * hh + 1) | (lane == _MB_KILL),
                          1.0, extra)
        blk_lane = lane - (_MB_BLK0 + 32 * hh)
        extra = jnp.where((blk_lane >= 0) & (blk_lane < 32),
                          slopes[hh] * (blk_lane * blk).astype(F32), extra)
        qaug_ref[hh] = jnp.concatenate([qm * jnp.asarray(SCALE, BF16), extra.astype(BF16)], axis=1)
        m_ref[hh] = jnp.full((blk, LANES), -jnp.inf, F32)
        acc_ref[hh] = jnp.zeros((blk, LANES), F32)

    lane1 = lax.broadcasted_iota(jnp.int32, (1, LANES), 1)
    rc = _ROW_CHUNK

    def score_stage(j, buf, *, is_own=False, valid=True):
        kj = k_ref[0, pl.ds(pl.multiple_of(j * blk, blk), blk), :]
        ind = (lane1 % 32 == j) & (lane1 >= _MB_BLK0) if is_own else \
              (lane1 % 32 == j) & ((lane1 < 32) | (lane1 >= _MB_BLK0))
        kill = jnp.where(lane1 == _MB_KILL, jnp.where(valid, 0.0, NEG_INF), 0.0)
        side = jnp.broadcast_to(jnp.where(ind, 1.0, kill), (blk, LANES)).astype(BF16)
        kx = jnp.where(ind | (lane1 == _MB_KILL), side, kx_ref[...])
        k_aug = jnp.concatenate([kj, kx], axis=1)
        s = _dot_nt(qaug_ref[...].reshape(HEADS_PER_BLOCK * blk, 2 * LANES), k_aug)
        s_ref[buf] = s.reshape(HEADS_PER_BLOCK, blk, blk)

    def softmax_stage(buf, *, is_own=False):
        for hh in range(HEADS_PER_BLOCK):
            for c in range(blk // rc):
                rows = slice(c * rc, (c + 1) * rc)
                s = s_ref[buf, hh, rows, :]
                if is_own:
                    col_c = lax.broadcasted_iota(jnp.int32, (rc, blk), 1)
                    row_c = lax.broadcasted_iota(jnp.int32, (rc, blk), 0) + c * rc
                    s = jnp.where(col_c <= row_c, s, NEG_INF)
                m_old = m_ref[hh, rows, :]
                m_new = jnp.maximum(m_old, jnp.max(s, axis=1, keepdims=True))
                alpha_ref[buf, hh, rows, :] = jnp.exp(m_old - m_new)
                m_ref[hh, rows, :] = m_new
                p = jnp.exp(s - jnp.concatenate([m_new, m_new], axis=1))
                p_ref[buf, hh, rows, :] = p.astype(BF16)

    def value_stage(j, buf):
        vj = v_ref[0, pl.ds(pl.multiple_of(j * blk, blk), blk), :]
        v_aug = jnp.concatenate(
            [jnp.where((lane1 >= hh * HEAD_DIM) & (lane1 < (hh + 1) * HEAD_DIM), vj, jnp.ones_like(vj))
             for hh in range(HEADS_PER_BLOCK)], axis=1)
        pv = _dot(p_ref[buf].reshape(HEADS_PER_BLOCK * blk, blk), v_aug)
        for hh in range(HEADS_PER_BLOCK):
            acc_ref[hh] = alpha_ref[buf, hh] * acc_ref[hh] + pv[hh * blk:(hh + 1) * blk, hh * LANES:(hh + 1) * LANES]

    n_past = qi

    @pl.when(n_past == 0)
    def _():
        score_stage(qi, 0, is_own=True)
        softmax_stage(0, is_own=True)
        value_stage(qi, 0)

    @pl.when(n_past > 0)
    def _():
        last = n_past - 1

        def past(t):
            return jnp.clip(last - t, 0, last)

        score_stage(qi, 1, is_own=True)
        score_stage(past(0), 0)
        softmax_stage(1, is_own=True)
        value_stage(qi, 1)
        softmax_stage(0)
        score_stage(past(1), 1, valid=1 < n_past)

        reach = jnp.zeros((1, LANES), F32)
        for hh in range(HEADS_PER_BLOCK):
            in_head = (lane >= hh * HEAD_DIM) & (lane < (hh + 1) * HEAD_DIM)
            qf = q2.astype(F32)
            qn2 = jnp.sum(jnp.where(in_head, qf * qf, 0.0), axis=1, keepdims=True)
            qn2 = jnp.max(jnp.broadcast_to(qn2, (blk, LANES)), axis=0, keepdims=True)
            bound = jnp.sqrt(qn2 * kn2_ref[hh, 0:1, :]) * SCALE
            m_min = jnp.min(m_ref[hh], axis=0, keepdims=True)
            reach = jnp.maximum(reach, (bound - m_min + MOBA_DEAD_GAP) / slopes[hh])
        n_past_f = jnp.full((1, LANES), n_past, jnp.int32).astype(F32)
        n_live = jnp.max(jnp.minimum((reach - 1.0) / blk + 1.0, n_past_f))

        def pair(state):
            tt, tf = state
            t = 2 * tt
            score_stage(past(t), 0, valid=t < n_past)
            softmax_stage(1)
            value_stage(past(t - 2), 0)
            score_stage(past(t + 1), 1, valid=t + 1 < n_past)
            softmax_stage(0)
            value_stage(past(t - 1), 1)
            return tt + 1, tf + 2.0

        lax.while_loop(lambda st: st[1] < n_live + 2.0, pair, (jnp.int32(1), jnp.float32(2.0)))

    acc0 = acc_ref[0]
    acc1 = acc_ref[1]
    o0 = acc0 / pltpu.roll(acc0, HEAD_DIM, axis=1)
    o1 = acc1 / pltpu.roll(acc1, HEAD_DIM, axis=1)
    o_ref[0] = jnp.where(lane < HEAD_DIM, o0, o1).astype(BF16)


def _moba_attention(qkv):
    b, s, _ = qkv.shape
    blk = MOBA_BLOCK
    n_blk = s // blk
    assert s % blk == 0 and MOBA_TOPK <= n_blk - 1 and n_blk <= 32
    n_pairs = N_HEADS_MOBA // HEADS_PER_BLOCK
    return pl.pallas_call(
        functools.partial(_moba_kernel, n_blk=n_blk),
        out_shape=jax.ShapeDtypeStruct((b, s, N_HEADS_MOBA * HEAD_DIM), BF16),
        grid=(b, n_pairs, n_blk),
        in_specs=[pl.BlockSpec((1, blk, LANES), lambda bi, p, i: (bi, i, Q_BLK0 + p)),
                  pl.BlockSpec((1, s, LANES), lambda bi, p, i: (bi, 0, K_BLK0 + p)),
                  pl.BlockSpec((1, s, LANES), lambda bi, p, i: (bi, 0, V_BLK0 + p))],
        out_specs=pl.BlockSpec((1, blk, LANES), lambda bi, p, i: (bi, i, p)),
        scratch_shapes=[pltpu.VMEM((LANES, LANES), F32),
                        pltpu.VMEM((HEADS_PER_BLOCK, 8, LANES), F32),
                        pltpu.VMEM((blk, LANES), BF16),
                        pltpu.VMEM((HEADS_PER_BLOCK, blk, 2 * LANES), BF16),
                        pltpu.VMEM((HEADS_PER_BLOCK, blk, LANES), F32),
                        pltpu.VMEM((2, HEADS_PER_BLOCK, blk, LANES), F32),
                        pltpu.VMEM((HEADS_PER_BLOCK, blk, LANES), F32),
                        pltpu.VMEM((2, HEADS_PER_BLOCK, blk, blk), F32),
                        pltpu.VMEM((2, HEADS_PER_BLOCK, blk, blk), BF16)],
        compiler_params=_params(3),
        name="moba_attn",
    )(qkv, qkv, qkv)


SB_TILE = 256


def _softplus(z):
    return jnp.maximum(z, 0.0) + jnp.log(1.0 + jnp.exp(-jnp.abs(z)))


_SB_SLOTS = 3
SB_DEAD_MASS = 128.0


def _sb_kernel(q_ref, k_ref, v_ref, u_ref, o_ref, qaug_ref, carry_ref, acc_ref,
               z_ref, sphl_ref, c_ref, rs_ref, a_ref):
    qi = pl.program_id(2)
    t = SB_TILE
    rc = _ROW_CHUNK
    lane = lax.broadcasted_iota(jnp.int32, (t, LANES), 1)
    lane1 = lax.broadcasted_iota(jnp.int32, (1, LANES), 1)
    q2 = q_ref[0]
    for hh in range(HEADS_PER_BLOCK):
        in_head = (lane >= hh * HEAD_DIM) & (lane < (hh + 1) * HEAD_DIM)
        qm = jnp.where(in_head, q2, jnp.zeros_like(q2)) * jnp.asarray(SCALE, BF16)
        qaug_ref[hh] = jnp.concatenate([qm, jnp.where(lane == 0, 1.0, 0.0).astype(BF16)], axis=1)

    def score_stage(j, slot, *, valid=True):
        kj = k_ref[0, pl.ds(pl.multiple_of(j * t, t), t), :]
        kill = jnp.where(lane1 == 0, jnp.where(valid, 0.0, NEG_INF), 0.0)
        k_aug = jnp.concatenate([kj, jnp.broadcast_to(kill, (t, LANES)).astype(BF16)], axis=1)
        z = _dot_nt(qaug_ref[...].reshape(HEADS_PER_BLOCK * t, 2 * LANES), k_aug)
        z_ref[slot] = z.reshape(HEADS_PER_BLOCK, t, t)

    def _past_mask(c):
        col_c = lax.broadcasted_iota(jnp.int32, (rc, t), 1)
        row_c = lax.broadcasted_iota(jnp.int32, (rc, t), 0) + c * rc
        return col_c < row_c

    def softplus_stage(slot, *, diagonal=False):
        for hh in range(HEADS_PER_BLOCK):
            for c in range(t // rc):
                rows = slice(c * rc, (c + 1) * rc)
                sp = _softplus(z_ref[slot, hh, rows, :])
                if diagonal:
                    sp = jnp.where(_past_mask(c), sp, 0.0)
                hi, lo = _split2(sp)
                sphl_ref[slot, pl.ds((2 * hh) * t + c * rc, rc), :] = hi
                sphl_ref[slot, pl.ds((2 * hh + 1) * t + c * rc, rc), :] = lo
                rs_ref[slot, hh, rows, :] = jnp.broadcast_to(jnp.sum(sp, axis=1, keepdims=True), (rc, LANES))

    def suffix_stage(slot):
        c_ref[slot] = _dot(sphl_ref[slot], u_ref[...])

    def weight_stage(slot, *, diagonal=False):
        for hh in range(HEADS_PER_BLOCK):
            for c in range(t // rc):
                rows = slice(c * rc, (c + 1) * rc)
                z = z_ref[slot, hh, rows, :]
                cc = c_ref[slot, pl.ds((2 * hh) * t + c * rc, rc), :] + \
                    c_ref[slot, pl.ds((2 * hh + 1) * t + c * rc, rc), :]
                if diagonal:
                    a = jnp.where(_past_mask(c), jnp.exp(z - cc), 0.0)
                    carry_ref[hh, rows, :] = rs_ref[slot, hh, rows, :]
                else:
                    carry = carry_ref[hh, rows, :]
                    a = jnp.exp(z - (cc + jnp.concatenate([carry, carry], axis=1)))
                    carry_ref[hh, rows, :] = carry + rs_ref[slot, hh, rows, :]
                a_ref[slot, hh, rows, :] = a.astype(BF16)

    def value_stage(j, slot, *, first=False):
        vj = v_ref[0, pl.ds(pl.multiple_of(j * t, t), t), :]
        av = _dot(a_ref[slot].reshape(HEADS_PER_BLOCK * t, t), vj)
        for hh in range(HEADS_PER_BLOCK):
            av_h = av[hh * t:(hh + 1) * t, :]
            acc_ref[hh] = av_h if first else acc_ref[hh] + av_h

    n_past = qi

    @pl.when(n_past == 0)
    def _():
        score_stage(qi, 0)
        softplus_stage(0, diagonal=True)
        suffix_stage(0)
        weight_stage(0, diagonal=True)
        value_stage(qi, 0, first=True)

    @pl.when(n_past > 0)
    def _():
        last = n_past - 1

        def key_tile(i):
            return jnp.clip(last - i, 0, last)

        score_stage(qi, 2)
        score_stage(key_tile(0), 0)
        softplus_stage(2, diagonal=True)
        suffix_stage(2)
        softplus_stage(0)
        suffix_stage(0)
        weight_stage(2, diagonal=True)
        value_stage(qi, 2, first=True)
        weight_stage(0)
        value_stage(key_tile(0), 0)

        def trip(tt, first=False):
            for k in range(_SB_SLOTS):
                i = 1 + _SB_SLOTS * tt + k
                score_stage(key_tile(i), (1 + k) % _SB_SLOTS, valid=i < n_past)
                if not (first and k < 1):
                    softplus_stage(k % _SB_SLOTS)
                    suffix_stage(k % _SB_SLOTS)
                if not (first and k < 2):
                    weight_stage((k - 1) % _SB_SLOTS)
                    value_stage(key_tile(i - 2), (k - 1) % _SB_SLOTS)
            return tt + 1, jnp.min(carry_ref[...])

        @pl.when((n_past > 1) & (jnp.min(carry_ref[...]) < SB_DEAD_MASS))
        def _():
            n_trips = (n_past - 1 + 2 + _SB_SLOTS - 1) // _SB_SLOTS
            lax.while_loop(lambda st: (st[0] < n_trips) & (st[1] < SB_DEAD_MASS), lambda st: trip(st[0]),
                           trip(jnp.int32(0), first=True))

    o_ref[0] = jnp.where(lane < HEAD_DIM, acc_ref[0], acc_ref[1]).astype(BF16)


def _sb_attention(qkv):
    b, s, _ = qkv.shape
    t = SB_TILE
    assert s % t == 0
    n_pairs = N_HEADS_SB // HEADS_PER_BLOCK
    u = (lax.broadcasted_iota(jnp.int32, (t, t), 0) >= lax.broadcasted_iota(jnp.int32, (t, t), 1)).astype(BF16)
    return pl.pallas_call(
        _sb_kernel,
        out_shape=jax.ShapeDtypeStruct((b, s, N_HEADS_SB * HEAD_DIM), BF16),
        grid=(b, n_pairs, s // t),
        in_specs=[pl.BlockSpec((1, t, LANES), lambda bi, p, i: (bi, i, Q_BLK0 + SB_BLK + p)),
                  pl.BlockSpec((1, s, LANES), lambda bi, p, i: (bi, 0, K_BLK0 + SB_BLK + p)),
                  pl.BlockSpec((1, s, LANES), lambda bi, p, i: (bi, 0, V_BLK0 + SB_BLK + p)),
                  pl.BlockSpec((t, t), lambda bi, p, i: (0, 0))],
        out_specs=pl.BlockSpec((1, t, LANES), lambda bi, p, i: (bi, i, p)),
        scratch_shapes=[pltpu.VMEM((HEADS_PER_BLOCK, t, 2 * LANES), BF16),
                        pltpu.VMEM((HEADS_PER_BLOCK, t, LANES), F32),
                        pltpu.VMEM((HEADS_PER_BLOCK, t, LANES), F32),
                        pltpu.VMEM((_SB_SLOTS, HEADS_PER_BLOCK, t, t), F32),
                        pltpu.VMEM((_SB_SLOTS, 2 * HEADS_PER_BLOCK * t, t), BF16),
                        pltpu.VMEM((_SB_SLOTS, 2 * HEADS_PER_BLOCK * t, t), F32),
                        pltpu.VMEM((_SB_SLOTS, HEADS_PER_BLOCK, t, LANES), F32),
                        pltpu.VMEM((_SB_SLOTS, HEADS_PER_BLOCK, t, t), BF16)],
        compiler_params=_params(3),
        name="sb_attn",
    )(qkv, qkv, qkv, u)


DIL_SPAN = 2048
_DIL_UNROLL = 16


def _dil_kernel(q_ref, k_ref, v_ref, kp_ref, vp_ref, o_ref, qf_ref, kf_ref, vf_ref, oc_ref, lse_ref, bias_ref):
    pair = pl.program_id(1)
    i = pl.program_id(2)
    n = DIL_N
    span = DIL_SPAN
    lane = lax.broadcasted_iota(jnp.int32, (n, LANES), 1)
    colh = lax.broadcasted_iota(jnp.int32, (n, 2 * n), 1)

    @pl.when(i == 0)
    def _():
        row = lax.broadcasted_iota(jnp.int32, (n, 2 * n), 0)
        delta = row + n - colh
        valid = (delta >= 0) & (delta <= n)
        for ci, (_, d) in enumerate(DIL_CONFIGS):
            for hh in range(HEADS_PER_BLOCK):
                slope = jnp.asarray(2.0 ** -(hh + 1), F32)
                for p in range(1, N_HEADS_DIL // HEADS_PER_BLOCK):
                    slope = jnp.where(pair == p, 2.0 ** -(2 * p + hh + 1), slope)
                bias_ref[ci, hh] = jnp.where(valid, -slope * (delta * d).astype(F32), NEG_INF)

    qf_ref[...] = q_ref[0].astype(F32)
    kf_ref[0:span, :] = kp_ref[0].astype(F32)
    kf_ref[span:2 * span, :] = k_ref[0].astype(F32)
    vf_ref[0:span, :] = vp_ref[0].astype(F32)
    vf_ref[span:2 * span, :] = v_ref[0].astype(F32)

    def rows(start, size, d):
        return pl.ds(start, size) if d == 1 else pl.ds(start, size, stride=d)

    def unit(ci, d, r, bl):
        q0 = r + bl * (n * d)
        qu = qf_ref[rows(q0, n, d), :].astype(BF16)
        ku = kf_ref[rows(span + q0 - n * d, 2 * n, d), :].astype(BF16)
        vu = vf_ref[rows(span + q0 - n * d, 2 * n, d), :].astype(BF16)
        no_prev = jnp.where((i == 0) & (bl == 0), NEG_INF, 0.0)
        qms = [jnp.where((lane >= hh * HEAD_DIM) & (lane < (hh + 1) * HEAD_DIM), qu, jnp.zeros_like(qu))
               * jnp.asarray(SCALE, BF16) for hh in range(HEADS_PER_BLOCK)]
        s_both = _dot_nt(jnp.concatenate(qms, axis=0), ku)
        ps, dens, lse_h = [], [], []
        for hh in range(HEADS_PER_BLOCK):
            s = s_both[hh * n:(hh + 1) * n, :] + bias_ref[ci, hh] + jnp.where(colh < n, no_prev, 0.0)
            m = jnp.max(s, axis=1, keepdims=True)
            p = jnp.exp(s - m)
            den = jnp.sum(p, axis=1, keepdims=True)
            ps.append(p.astype(BF16))
            dens.append(den)
            lse_h.append(jnp.broadcast_to(m + jnp.log(den), (n, LANES)))
        pv = _dot(jnp.concatenate(ps, axis=0), vu)
        o_h = [pv[hh * n:(hh + 1) * n, :] / dens[hh] for hh in range(HEADS_PER_BLOCK)]
        oc_ref[ci, rows(q0, n, d), :] = jnp.where(lane < HEAD_DIM, o_h[0], o_h[1])
        lse_ref[ci, rows(q0, n, d), :] = jnp.where(lane < HEAD_DIM, lse_h[0], lse_h[1])

    n_units = span // n
    for ci, (_, d) in enumerate(DIL_CONFIGS):
        per_res = n_units // d

        def group(g, c, ci=ci, d=d, per_res=per_res):
            for k in range(_DIL_UNROLL):
                u = g * _DIL_UNROLL + k
                unit(ci, d, u // per_res, u % per_res)
            return c

        lax.fori_loop(0, n_units // _DIL_UNROLL, group, 0)

    mc = 256
    for c0 in range(0, span, mc):
        l1, l2, l3 = (lse_ref[ci, c0:c0 + mc, :] for ci in range(3))
        lmax = jnp.maximum(jnp.maximum(l1, l2), l3)
        e1, e2, e3 = jnp.exp(l1 - lmax), jnp.exp(l2 - lmax), jnp.exp(l3 - lmax)
        mix = (e1 * oc_ref[0, c0:c0 + mc, :] + e2 * oc_ref[1, c0:c0 + mc, :] + e3 * oc_ref[2, c0:c0 + mc, :]) \
            / (e1 + e2 + e3)
        o_ref[0, c0:c0 + mc, :] = mix.astype(BF16)


def _dilated_mixture(qkv):
    b, s, _ = qkv.shape
    span = DIL_SPAN
    assert s % span == 0 and all(w <= span and span % (DIL_N * d) == 0 for w, d in DIL_CONFIGS)
    n_pairs = N_HEADS_DIL // HEADS_PER_BLOCK
    n_cfg = len(DIL_CONFIGS)

    def cur(off):
        return lambda bi, p, i: (bi, i, off + DIL_BLK + p)

    def prev(off):
        return lambda bi, p, i: (bi, jnp.maximum(i - 1, 0), off + DIL_BLK + p)

    blk = (1, span, LANES)
    return pl.pallas_call(
        _dil_kernel,
        out_shape=jax.ShapeDtypeStruct((b, s, N_HEADS_DIL * HEAD_DIM), BF16),
        grid=(b, n_pairs, s // span),
        in_specs=[pl.BlockSpec(blk, cur(Q_BLK0)), pl.BlockSpec(blk, cur(K_BLK0)), pl.BlockSpec(blk, cur(V_BLK0)),
                  pl.BlockSpec(blk, prev(K_BLK0)), pl.BlockSpec(blk, prev(V_BLK0))],
        out_specs=pl.BlockSpec(blk, lambda bi, p, i: (bi, i, p)),
        scratch_shapes=[pltpu.VMEM((span, LANES), F32),
                        pltpu.VMEM((2 * span, LANES), F32),
                        pltpu.VMEM((2 * span, LANES), F32),
                        pltpu.VMEM((n_cfg, span, LANES), F32),
                        pltpu.VMEM((n_cfg, span, LANES), F32),
                        pltpu.VMEM((n_cfg, HEADS_PER_BLOCK, DIL_N, 2 * DIL_N), F32)],
        compiler_params=_params(3),
        name="dilated_attn",
    )(qkv, qkv, qkv, qkv, qkv)


def _rms(x, g):
    return x * lax.rsqrt(jnp.mean(x * x, axis=-1, keepdims=True) + NORM_EPS) * g


def _outproj_kernel(x_ref, oa_ref, ob_ref, oc_ref, g_ref, w_ref, out_ref):
    g = g_ref[...]
    wa = N_HEADS_MOBA * HEAD_DIM
    wb = wa + N_HEADS_SB * HEAD_DIM
    rows = 256
    for c in range(x_ref.shape[0] // rows):
        r = slice(c * rows, (c + 1) * rows)
        y = jnp.concatenate([_rms(oa_ref[r, :].astype(F32), g[:, :wa]),
                             _rms(ob_ref[r, :].astype(F32), g[:, wa:wb]),
                             _rms(oc_ref[r, :].astype(F32), g[:, wb:])], axis=1).astype(BF16)
        out_ref[r, :] = x_ref[r, :] + _dot(y, w_ref[...])


def _out_proj(x2, oa, ob, oc, g, w, *, tm=1024):
    t, d = x2.shape
    row = lambda i: (i, 0)
    const = lambda i: (0, 0)
    return pl.pallas_call(
        _outproj_kernel,
        out_shape=jax.ShapeDtypeStruct((t, d), F32),
        grid=(t // tm,),
        in_specs=[pl.BlockSpec((tm, d), row),
                  pl.BlockSpec((tm, oa.shape[1]), row), pl.BlockSpec((tm, ob.shape[1]), row),
                  pl.BlockSpec((tm, oc.shape[1]), row),
                  pl.BlockSpec((1, d), const), pl.BlockSpec((d, d), const)],
        out_specs=pl.BlockSpec((tm, d), row),
        compiler_params=_params(1),
        name="out_proj",
    )(x2, oa, ob, oc, g.reshape(1, d), w)


_EXP_LANE0 = N_GROUPS
_MOE_ROWS = 256
_MOE_CHUNK = 128


def _moe_kernel(x_ref, g_ref, wrh_ref, wrl_ref, br_ref, ltri_ref, fg_ref, wg_ref, wu_ref, wd_ref, out_ref,
                h_ref, hs_ref, cws_ref, ys_ref, pt_ref, pos_ref, nck_ref, cb_ref, *, n_sorted, final_norm):
    e = pl.program_id(1)
    tm = x_ref.shape[0]
    lane = lax.broadcasted_iota(jnp.int32, (tm, LANES), 1)
    lane1 = lax.broadcasted_iota(jnp.int32, (1, LANES), 1)

    @pl.when(e == 0)
    def _():
        x = x_ref[...]
        h = _rms(x, g_ref[...])
        hh, hl = _split2(h)
        logits = _dot(hh, wrh_ref[...]) + _dot(hh, wrl_ref[...]) + _dot(hl, wrh_ref[...]) + br_ref[...]
        lane_f = lane.astype(F32)
        big = float(LANES)
        gl = jnp.where(lane < N_GROUPS, logits, -jnp.inf)
        gmax = jnp.max(gl, axis=1, keepdims=True)
        gidx = jnp.min(jnp.where(gl == gmax, lane_f, big), axis=1, keepdims=True)
        g_w = 1.0 / jnp.sum(jnp.exp(gl - gmax), axis=1, keepdims=True)
        lane_group = ((lane - _EXP_LANE0) // EXPERTS_PER_GROUP).astype(F32)
        in_group = (lane >= _EXP_LANE0) & (lane < _EXP_LANE0 + N_EXPERTS) & (lane_group == gidx)
        el = jnp.where(in_group, logits, -jnp.inf)
        v1 = jnp.max(el, axis=1, keepdims=True)
        i1 = jnp.min(jnp.where(el == v1, lane_f, big), axis=1, keepdims=True)
        el2 = jnp.where(lane_f == i1, -jnp.inf, el)
        v2 = jnp.max(el2, axis=1, keepdims=True)
        i2 = jnp.min(jnp.where(el2 == v2, lane_f, big), axis=1, keepdims=True)
        r = jnp.exp(v2 - v1)
        w1 = g_w / (1.0 + r)
        w2 = g_w * r / (1.0 + r)
        comb = jnp.where(lane_f == i1, w1, 0.0) + jnp.where(lane_f == i2, w2, 0.0)

        onehot = jnp.where((lane_f == gidx) & (lane < N_GROUPS), 1.0, 0.0)
        before = _dot(ltri_ref[...], onehot.astype(BF16))
        rank = jnp.sum(onehot * before, axis=1, keepdims=True)
        count = jnp.sum(onehot, axis=0, keepdims=True)
        chunks = jnp.floor((count + (_MOE_CHUNK - 1.0)) * (1.0 / _MOE_CHUNK))
        nck_ref[...] = jnp.broadcast_to(chunks, nck_ref.shape)
        pos = rank
        start = jnp.float32(0.0)
        for gi in range(N_GROUPS):
            pos = pos + jnp.where(gidx == float(gi), start, 0.0)
            start = start + jnp.max(jnp.where(lane1 == gi, chunks, 0.0)) * _MOE_CHUNK
        pos_rep = jnp.broadcast_to(pos, (tm, LANES))
        pos_ref[...] = pos_rep
        pos_t = jnp.transpose(pos_rep)[0:1, :]
        h_ref[...] = hh
        c_hi, c_lo = _split2(comb)
        for c in range(n_sorted // _MOE_ROWS):
            rows = slice(c * _MOE_ROWS, (c + 1) * _MOE_ROWS)
            slot = (lax.broadcasted_iota(jnp.int32, (_MOE_ROWS, tm), 0) + c * _MOE_ROWS).astype(F32)
            perm = jnp.where(slot == pos_t, 1.0, 0.0).astype(BF16)
            hs_ref[rows, :] = _dot(perm, h_ref[...]).astype(BF16)
            cws_ref[rows, :] = _dot(perm, c_hi) + _dot(perm, c_lo)
        ys_ref[...] = jnp.zeros_like(ys_ref)
        cb_ref[0] = 0

    lane_c = lax.broadcasted_iota(jnp.int32, (_MOE_CHUNK, LANES), 1)
    n_chunks = jnp.max(jnp.where(lane1 == e, nck_ref[0:1, :], 0.0))

    def chunk(state):
        cb, cf = state
        rows = pl.ds(pl.multiple_of(cb * _MOE_CHUNK, _MOE_CHUNK), _MOE_CHUNK)
        hc = hs_ref[rows, :]
        cwc = cws_ref[rows, :]
        acts = []
        for k in range(EXPERTS_PER_GROUP):
            gate = _dot(hc, wg_ref[e, k])
            up = _dot(hc, wu_ref[e, k])
            cw = jnp.sum(jnp.where(lane_c == _EXP_LANE0 + e * EXPERTS_PER_GROUP + k, cwc, 0.0),
                         axis=1, keepdims=True)
            acts.append((gate / (1.0 + jnp.exp(-gate)) * up * cw).astype(BF16))
        ys_ref[rows, :] = _dot(jnp.concatenate(acts, axis=1), wd_ref[e]).astype(BF16)
        return cb + 1, cf + 1.0

    cb_end, _ = lax.while_loop(lambda st: st[1] < n_chunks, chunk, (cb_ref[0], jnp.float32(0.0)))
    cb_ref[0] = cb_end

    @pl.when(e == N_GROUPS - 1)
    def _():
        pos_rep2 = jnp.concatenate([pos_ref[...], pos_ref[...]], axis=1)
        for c in range(n_sorted // _MOE_ROWS):
            slot = (lax.broadcasted_iota(jnp.int32, (tm, _MOE_ROWS), 1) + c * _MOE_ROWS).astype(F32)
            perm_t = jnp.where(slot == pos_rep2, 1.0, 0.0).astype(BF16)
            pt_ref[:, c * _MOE_ROWS:(c + 1) * _MOE_ROWS] = perm_t
        y = x_ref[...] + _dot(pt_ref[...], ys_ref[...])
        out_ref[...] = _rms(y, fg_ref[...]) if final_norm else y


def _moe(x2, g, wr_hi, wr_lo, br, wg, wu, wd, final_g, *, final_norm, tm=1024):
    t, d = x2.shape
    f = wg.shape[3]
    n_sorted = tm + N_GROUPS * _MOE_CHUNK
    assert n_sorted % _MOE_ROWS == 0
    ltri = (lax.broadcasted_iota(jnp.int32, (tm, tm), 0) > lax.broadcasted_iota(jnp.int32, (tm, tm), 1)).astype(BF16)
    row = lambda i, e: (i, 0)
    const = lambda i, e: (0, 0)
    return pl.pallas_call(
        functools.partial(_moe_kernel, n_sorted=n_sorted, final_norm=final_norm),
        out_shape=jax.ShapeDtypeStruct((t, d), F32),
        grid=(t // tm, N_GROUPS),
        in_specs=[pl.BlockSpec((tm, d), row),
                  pl.BlockSpec((1, d), const),
                  pl.BlockSpec((d, LANES), const), pl.BlockSpec((d, LANES), const),
                  pl.BlockSpec((1, LANES), const),
                  pl.BlockSpec((tm, tm), const, pipeline_mode=pl.Buffered(1)),
                  pl.BlockSpec((1, d), const),
                  pl.BlockSpec((N_GROUPS, EXPERTS_PER_GROUP, d, f), lambda i, e: (0, 0, 0, 0),
                               pipeline_mode=pl.Buffered(1)),
                  pl.BlockSpec((N_GROUPS, EXPERTS_PER_GROUP, d, f), lambda i, e: (0, 0, 0, 0),
                               pipeline_mode=pl.Buffered(1)),
                  pl.BlockSpec((N_GROUPS, EXPERTS_PER_GROUP * f, d), lambda i, e: (0, 0, 0),
                               pipeline_mode=pl.Buffered(1))],
        out_specs=pl.BlockSpec((tm, d), row),
        scratch_shapes=[pltpu.VMEM((tm, d), BF16),
                        pltpu.VMEM((n_sorted, d), BF16),
                        pltpu.VMEM((n_sorted, LANES), F32),
                        pltpu.VMEM((n_sorted, d), BF16),
                        pltpu.VMEM((tm, n_sorted), BF16),
                        pltpu.VMEM((tm, LANES), F32),
                        pltpu.VMEM((8, LANES), F32),
                        pltpu.SMEM((1,), jnp.int32)],
        compiler_params=_params(2, vmem=MOE_VMEM_LIMIT),
        name="hier_moe",
    )(x2, g.reshape(1, d), wr_hi, wr_lo, br, ltri, final_g.reshape(1, d), wg, wu, wd)


def _router_weights(w_gr, b_gr, w_er, b_er):
    d = w_gr.shape[0]
    w = jnp.concatenate([w_gr, jnp.moveaxis(w_er, 0, 1).reshape(d, N_EXPERTS)], axis=1)
    w = jnp.pad(w, ((0, 0), (0, LANES - w.shape[1])))
    bias = jnp.pad(jnp.concatenate([b_gr, b_er.reshape(-1)]), (0, LANES - N_GROUPS - N_EXPERTS))
    hi, lo = _split2(w)
    return hi, lo, bias.reshape(1, LANES)


def _layer(x2, b, s, ln1_g, w_in, mix_g, w_out, ln2_g, w_gr, b_gr, w_er, b_er, w_gate, w_up, w_down,
           final_g, final_norm):
    t, d = x2.shape
    qkv = _qkv_proj(x2, ln1_g, w_in.astype(BF16)).reshape(b, s, 3 * d)
    oa = _moba_attention(qkv)
    ob = _sb_attention(qkv)
    oc = _dilated_mixture(qkv)
    x2 = _out_proj(x2, oa.reshape(t, -1), ob.reshape(t, -1), oc.reshape(t, -1), mix_g, w_out.astype(BF16))
    wr_hi, wr_lo, br = _router_weights(w_gr, b_gr, w_er, b_er)
    f = w_gate.shape[-1]
    return _moe(x2, ln2_g, wr_hi, wr_lo, br, w_gate.astype(BF16), w_up.astype(BF16),
                w_down.reshape(N_GROUPS, EXPERTS_PER_GROUP * f, d).astype(BF16), final_g, final_norm=final_norm)


def kernel(x, ln1_g, w_in, mix_norm_g, w_out, ln2_g, w_group_router, b_group_router,
           w_expert_router, b_expert_router, w_gate, w_up, w_down, final_norm_g):
    b, s, d = x.shape
    x2 = x.reshape(b * s, d)
    depth = ln1_g.shape[0]
    for l in range(depth):
        x2 = _layer(x2, b, s, ln1_g[l], w_in[l], mix_norm_g[l], w_out[l], ln2_g[l],
                    w_group_router[l], b_group_router[l], w_expert_router[l], b_expert_router[l],
                    w_gate[l], w_up[l], w_down[l], final_norm_g, final_norm=(l == depth - 1))
    return x2.reshape(b, s, d)
```

```python
import functools

import jax
import jax.numpy as jnp
from jax import lax
from jax.experimental import pallas as pl
from jax.experimental.pallas import tpu as pltpu

F32 = jnp.float32
BF16 = jnp.bfloat16

D_MODEL = 1024
HEAD_DIM = 64
N_HEADS = 16
LANES = 128
HEADS_PER_BLOCK = LANES // HEAD_DIM
N_HEADS_MOBA = 4
N_HEADS_SB = 4
N_HEADS_DIL = 8
MOBA_BLOCK = 256
MOBA_TOPK = 3
DIL_CONFIGS = ((128, 1), (512, 4), (2048, 16))
DIL_N = 128
N_GROUPS = 4
EXPERTS_PER_GROUP = 4
N_EXPERTS = N_GROUPS * EXPERTS_PER_GROUP
D_EXPERT = 256
NORM_EPS = 1e-6
NEG_INF = -1e30
SCALE = HEAD_DIM ** -0.5

Q_BLK0 = 0
K_BLK0 = D_MODEL // LANES
V_BLK0 = 2 * D_MODEL // LANES
SB_BLK = N_HEADS_MOBA // HEADS_PER_BLOCK
DIL_BLK = (N_HEADS_MOBA + N_HEADS_SB) // HEADS_PER_BLOCK

VMEM_LIMIT = 56 * 1024 * 1024


def _params(n_axes, vmem=VMEM_LIMIT):
    return pltpu.CompilerParams(dimension_semantics=("arbitrary",) * n_axes,
                                vmem_limit_bytes=vmem)


def _dot_nt(a, b):
    return lax.dot_general(a, b, (((1,), (1,)), ((), ())), preferred_element_type=F32)


def _dot(a, b):
    return jnp.dot(a, b, preferred_element_type=F32)


def _split3(x):
    hi = x.astype(BF16)
    r1 = x - hi.astype(F32)
    mid = r1.astype(BF16)
    lo = (r1 - mid.astype(F32)).astype(BF16)
    return hi, mid, lo


def _split2(x):
    hi = x.astype(BF16)
    lo = (x - hi.astype(F32)).astype(BF16)
    return hi, lo


def _qkv_kernel(x_ref, g_ref, w_ref, o_ref, *, rows, tn):
    for c in range(x_ref.shape[0] // rows):
        r = slice(c * rows, (c + 1) * rows)
        x = x_ref[r, :]
        ms = jnp.mean(x * x, axis=-1, keepdims=True)
        h = (x * lax.rsqrt(ms + NORM_EPS) * g_ref[...]).astype(BF16)
        for j in range(w_ref.shape[1] // tn):
            o_ref[r, j * tn:(j + 1) * tn] = _dot(h, w_ref[:, j * tn:(j + 1) * tn]).astype(BF16)


def _qkv_proj(x2, g, w, *, tm=1024, rows=256, tn=1024):
    t, d = x2.shape
    n = w.shape[1]
    return pl.pallas_call(
        functools.partial(_qkv_kernel, rows=rows, tn=tn),
        out_shape=jax.ShapeDtypeStruct((t, n), BF16),
        grid=(t // tm,),
        in_specs=[pl.BlockSpec((tm, d), lambda i: (i, 0)),
                  pl.BlockSpec((1, d), lambda i: (0, 0)),
                  pl.BlockSpec((d, n), lambda i: (0, 0))],
        out_specs=pl.BlockSpec((tm, n), lambda i: (i, 0)),
        compiler_params=_params(1),
        name="qkv_proj",
    )(x2, g.reshape(1, d), w)


_MB_MAXBLK = 32
_MB_POS0 = 32
_MB_KILL = 40
_MB_BLK0 = 64


_ROW_CHUNK = 32


MOBA_DEAD_GAP = 110.0


def _moba_kernel(q_ref, k_ref, v_ref, o_ref, kmean_ref, kn2_ref, kx_ref, qaug_ref, m_ref, alpha_ref, acc_ref,
                 s_ref, p_ref, *, n_blk):
    pair = pl.program_id(1)
    qi = pl.program_id(2)
    blk = MOBA_BLOCK
    lane = lax.broadcasted_iota(jnp.int32, (blk, LANES), 1)
    row = lax.broadcasted_iota(jnp.int32, (blk, LANES), 0)
    slopes = [jnp.where(pair == 0, 2.0 ** (-2 * (hh + 1)), 2.0 ** (-2 * (hh + 3))).astype(F32)
              for hh in range(HEADS_PER_BLOCK)]

    @pl.when(qi == 0)
    def _():
        kmean_ref[...] = jnp.zeros_like(kmean_ref)
        kn2_ref[...] = jnp.zeros_like(kn2_ref)

        def body(n, c):
            kb = k_ref[0, pl.ds(pl.multiple_of(n * blk, blk), blk), :].astype(F32)
            kmean_ref[pl.ds(n, 1), :] = jnp.sum(kb, axis=0, keepdims=True) * (1.0 / blk)
            for hh in range(HEADS_PER_BLOCK):
                in_head = (lane >= hh * HEAD_DIM) & (lane < (hh + 1) * HEAD_DIM)
                n2 = jnp.sum(jnp.where(in_head, kb * kb, 0.0), axis=1, keepdims=True)
                n2 = jnp.max(jnp.broadcast_to(n2, (blk, LANES)), axis=0, keepdims=True)
                kn2_ref[hh] = jnp.maximum(kn2_ref[hh], jnp.broadcast_to(n2, (8, LANES)))
            return c

        lax.fori_loop(0, n_blk, body, 0)
        kx = jnp.zeros((blk, LANES), F32)
        for hh in range(HEADS_PER_BLOCK):
            kx = jnp.where(lane == _MB_POS0 + 2 * hh, slopes[hh] * ((row // LANES) * LANES).astype(F32), kx)
            kx = jnp.where(lane == _MB_POS0 + 2 * hh + 1, slopes[hh] * (row % LANES).astype(F32), kx)
        kx_ref[...] = kx.astype(BF16)

    q2 = q_ref[0]
    km_parts = _split3(kmean_ref[...])
    blk_row = lax.broadcasted_iota(jnp.int32, (_MB_MAXBLK, blk), 0)
    blk_row_f = blk_row.astype(F32)
    for hh in range(HEADS_PER_BLOCK):
        in_head = (lane >= hh * HEAD_DIM) & (lane < (hh + 1) * HEAD_DIM)
        qm = jnp.where(in_head, q2, jnp.zeros_like(q2))
        gate = _dot_nt(qm, km_parts[0]) + _dot_nt(qm, km_parts[1]) + _dot_nt(qm, km_parts[2])
        gate = jnp.transpose(gate)[0:_MB_MAXBLK, :]
        gate = jnp.where(blk_row < qi, gate, NEG_INF)
        gate = jnp.where(blk_row < n_blk, gate, -jnp.inf)
        sel = jnp.zeros((_MB_MAXBLK, blk), jnp.bool_)
        for _ in range(MOBA_TOPK):
            gmax = jnp.max(gate, axis=0, keepdims=True)
            first = jnp.min(jnp.where(gate == gmax, blk_row_f, float(LANES)), axis=0, keepdims=True)
            pick = blk_row_f == first
            sel = sel | pick
            gate = jnp.where(pick, -jnp.inf, gate)
        sel = sel & (blk_row < qi)
        sel_bias = jnp.where(sel | (blk_row >= n_blk), 0.0, NEG_INF)
        extra = jnp.transpose(jnp.concatenate(
            [sel_bias, jnp.zeros((LANES - _MB_MAXBLK, blk), F32)], axis=0))
        extra = jnp.where((lane == _MB_POS0 + 2 * hh) | (lane == _MB_POS0 + 2 * hh + 1) | (lane == _MB_KILL),
                          1.0, extra)
        blk_lane = lane - (_MB_BLK0 + _MB_MAXBLK * hh)
        extra = jnp.where((blk_lane >= 0) & (blk_lane < _MB_MAXBLK),
                          slopes[hh] * (blk_lane * blk).astype(F32), extra)
        qaug_ref[hh] = jnp.concatenate([qm * jnp.asarray(SCALE, BF16), extra.astype(BF16)], axis=1)
        m_ref[hh] = jnp.full((blk, LANES), -jnp.inf, F32)
        acc_ref[hh] = jnp.zeros((blk, LANES), F32)

    lane1 = lax.broadcasted_iota(jnp.int32, (1, LANES), 1)
    rc = _ROW_CHUNK

    def score_stage(j, buf, *, is_own=False, valid=True):
        kj = k_ref[0, pl.ds(pl.multiple_of(j * blk, blk), blk), :]
        ind = (lane1 % _MB_MAXBLK == j) & (lane1 >= _MB_BLK0) if is_own else \
              (lane1 % _MB_MAXBLK == j) & ((lane1 < _MB_MAXBLK) | (lane1 >= _MB_BLK0))
        kill = jnp.where(lane1 == _MB_KILL, jnp.where(valid, 0.0, NEG_INF), 0.0)
        side = jnp.broadcast_to(jnp.where(ind, 1.0, kill), (blk, LANES)).astype(BF16)
        kx = jnp.where(ind | (lane1 == _MB_KILL), side, kx_ref[...])
        k_aug = jnp.concatenate([kj, kx], axis=1)
        s = _dot_nt(qaug_ref[...].reshape(HEADS_PER_BLOCK * blk, 2 * LANES), k_aug)
        s_ref[buf] = s.reshape(HEADS_PER_BLOCK, blk, blk)

    def softmax_stage(buf, *, is_own=False):
        for hh in range(HEADS_PER_BLOCK):
            for c in range(blk // rc):
                rows = slice(c * rc, (c + 1) * rc)
                s = s_ref[buf, hh, rows, :]
                if is_own:
                    col_c = lax.broadcasted_iota(jnp.int32, (rc, blk), 1)
                    row_c = lax.broadcasted_iota(jnp.int32, (rc, blk), 0) + c * rc
                    s = jnp.where(col_c <= row_c, s, NEG_INF)
                m_old = m_ref[hh, rows, :]
                m_new = jnp.maximum(m_old, jnp.max(s, axis=1, keepdims=True))
                alpha_ref[buf, hh, rows, :] = jnp.exp(m_old - m_new)
                m_ref[hh, rows, :] = m_new
                p = jnp.exp(s - jnp.concatenate([m_new, m_new], axis=1))
                p_ref[buf, hh, rows, :] = p.astype(BF16)

    def value_stage(j, buf):
        vj = v_ref[0, pl.ds(pl.multiple_of(j * blk, blk), blk), :]
        v_aug = jnp.concatenate(
            [jnp.where((lane1 >= hh * HEAD_DIM) & (lane1 < (hh + 1) * HEAD_DIM), vj, jnp.ones_like(vj))
             for hh in range(HEADS_PER_BLOCK)], axis=1)
        pv = _dot(p_ref[buf].reshape(HEADS_PER_BLOCK * blk, blk), v_aug)
        for hh in range(HEADS_PER_BLOCK):
            acc_ref[hh] = alpha_ref[buf, hh] * acc_ref[hh] + pv[hh * blk:(hh + 1) * blk, hh * LANES:(hh + 1) * LANES]

    n_past = qi

    @pl.when(n_past == 0)
    def _():
        score_stage(qi, 0, is_own=True)
        softmax_stage(0, is_own=True)
        value_stage(qi, 0)

    @pl.when(n_past > 0)
    def _():
        last = n_past - 1

        def past(t):
            return jnp.clip(last - t, 0, last)

        score_stage(qi, 1, is_own=True)
        score_stage(past(0), 0)
        softmax_stage(1, is_own=True)
        value_stage(qi, 1)
        softmax_stage(0)
        score_stage(past(1), 1, valid=1 < n_past)

        reach = jnp.zeros((1, LANES), F32)
        for hh in range(HEADS_PER_BLOCK):
            in_head = (lane >= hh * HEAD_DIM) & (lane < (hh + 1) * HEAD_DIM)
            qf = q2.astype(F32)
            qn2 = jnp.sum(jnp.where(in_head, qf * qf, 0.0), axis=1, keepdims=True)
            qn2 = jnp.max(jnp.broadcast_to(qn2, (blk, LANES)), axis=0, keepdims=True)
            bound = jnp.sqrt(qn2 * kn2_ref[hh, 0:1, :]) * SCALE
            m_min = jnp.min(m_ref[hh], axis=0, keepdims=True)
            reach = jnp.maximum(reach, (bound - m_min + MOBA_DEAD_GAP) / slopes[hh])
        n_past_f = jnp.full((1, LANES), n_past, jnp.int32).astype(F32)
        n_live = jnp.max(jnp.minimum((reach - 1.0) / blk + 1.0, n_past_f))

        def pair(state):
            tt, tf = state
            t = 2 * tt
            score_stage(past(t), 0, valid=t < n_past)
            softmax_stage(1)
            value_stage(past(t - 2), 0)
            score_stage(past(t + 1), 1, valid=t + 1 < n_past)
            softmax_stage(0)
            value_stage(past(t - 1), 1)
            return tt + 1, tf + 2.0

        lax.while_loop(lambda st: st[1] < n_live + 2.0, pair, (jnp.int32(1), jnp.float32(2.0)))

    acc0 = acc_ref[0]
    acc1 = acc_ref[1]
    o0 = acc0 / pltpu.roll(acc0, HEAD_DIM, axis=1)
    o1 = acc1 / pltpu.roll(acc1, HEAD_DIM, axis=1)
    o_ref[0] = jnp.where(lane < HEAD_DIM, o0, o1).astype(BF16)


def _moba_attention(qkv):
    b, s, _ = qkv.shape
    blk = MOBA_BLOCK
    n_blk = s // blk
    assert s % blk == 0 and MOBA_TOPK <= n_blk - 1 and n_blk <= _MB_MAXBLK
    n_pairs = N_HEADS_MOBA // HEADS_PER_BLOCK
    return pl.pallas_call(
        functools.partial(_moba_kernel, n_blk=n_blk),
        out_shape=jax.ShapeDtypeStruct((b, s, N_HEADS_MOBA * HEAD_DIM), BF16),
        grid=(b, n_pairs, n_blk),
        in_specs=[pl.BlockSpec((1, blk, LANES), lambda bi, p, i: (bi, i, Q_BLK0 + p)),
                  pl.BlockSpec((1, s, LANES), lambda bi, p, i: (bi, 0, K_BLK0 + p)),
                  pl.BlockSpec((1, s, LANES), lambda bi, p, i: (bi, 0, V_BLK0 + p))],
        out_specs=pl.BlockSpec((1, blk, LANES), lambda bi, p, i: (bi, i, p)),
        scratch_shapes=[pltpu.VMEM((LANES, LANES), F32),
                        pltpu.VMEM((HEADS_PER_BLOCK, 8, LANES), F32),
                        pltpu.VMEM((blk, LANES), BF16),
                        pltpu.VMEM((HEADS_PER_BLOCK, blk, 2 * LANES), BF16),
                        pltpu.VMEM((HEADS_PER_BLOCK, blk, LANES), F32),
                        pltpu.VMEM((2, HEADS_PER_BLOCK, blk, LANES), F32),
                        pltpu.VMEM((HEADS_PER_BLOCK, blk, LANES), F32),
                        pltpu.VMEM((2, HEADS_PER_BLOCK, blk, blk), F32),
                        pltpu.VMEM((2, HEADS_PER_BLOCK, blk, blk), BF16)],
        compiler_params=_params(3),
        name="moba_attn",
    )(qkv, qkv, qkv)


SB_TILE = 256


def _softplus(z):
    return jnp.maximum(z, 0.0) + jnp.log(1.0 + jnp.exp(-jnp.abs(z)))


_SB_SLOTS = 3
SB_DEAD_MASS = 128.0


def _sb_kernel(q_ref, k_ref, v_ref, u_ref, o_ref, qaug_ref, carry_ref, acc_ref,
               z_ref, sphl_ref, c_ref, rs_ref, a_ref):
    qi = pl.program_id(2)
    t = SB_TILE
    rc = _ROW_CHUNK
    lane = lax.broadcasted_iota(jnp.int32, (t, LANES), 1)
    lane1 = lax.broadcasted_iota(jnp.int32, (1, LANES), 1)
    q2 = q_ref[0]
    for hh in range(HEADS_PER_BLOCK):
        in_head = (lane >= hh * HEAD_DIM) & (lane < (hh + 1) * HEAD_DIM)
        qm = jnp.where(in_head, q2, jnp.zeros_like(q2)) * jnp.asarray(SCALE, BF16)
        qaug_ref[hh] = jnp.concatenate([qm, jnp.where(lane == 0, 1.0, 0.0).astype(BF16)], axis=1)

    def score_stage(j, slot, *, valid=True):
        kj = k_ref[0, pl.ds(pl.multiple_of(j * t, t), t), :]
        kill = jnp.where(lane1 == 0, jnp.where(valid, 0.0, NEG_INF), 0.0)
        k_aug = jnp.concatenate([kj, jnp.broadcast_to(kill, (t, LANES)).astype(BF16)], axis=1)
        z = _dot_nt(qaug_ref[...].reshape(HEADS_PER_BLOCK * t, 2 * LANES), k_aug)
        z_ref[slot] = z.reshape(HEADS_PER_BLOCK, t, t)

    def _past_mask(c):
        col_c = lax.broadcasted_iota(jnp.int32, (rc, t), 1)
        row_c = lax.broadcasted_iota(jnp.int32, (rc, t), 0) + c * rc
        return col_c < row_c

    def softplus_stage(slot, *, diagonal=False):
        for hh in range(HEADS_PER_BLOCK):
            for c in range(t // rc):
                rows = slice(c * rc, (c + 1) * rc)
                sp = _softplus(z_ref[slot, hh, rows, :])
                if diagonal:
                    sp = jnp.where(_past_mask(c), sp, 0.0)
                hi, lo = _split2(sp)
                sphl_ref[slot, pl.ds((2 * hh) * t + c * rc, rc), :] = hi
                sphl_ref[slot, pl.ds((2 * hh + 1) * t + c * rc, rc), :] = lo
                rs_ref[slot, hh, rows, :] = jnp.broadcast_to(jnp.sum(sp, axis=1, keepdims=True), (rc, LANES))

    def suffix_stage(slot):
        c_ref[slot] = _dot(sphl_ref[slot], u_ref[...])

    def weight_stage(slot, *, diagonal=False):
        for hh in range(HEADS_PER_BLOCK):
            for c in range(t // rc):
                rows = slice(c * rc, (c + 1) * rc)
                z = z_ref[slot, hh, rows, :]
                cc = c_ref[slot, pl.ds((2 * hh) * t + c * rc, rc), :] + \
                    c_ref[slot, pl.ds((2 * hh + 1) * t + c * rc, rc), :]
                if diagonal:
                    a = jnp.where(_past_mask(c), jnp.exp(z - cc), 0.0)
                    carry_ref[hh, rows, :] = rs_ref[slot, hh, rows, :]
                else:
                    carry = carry_ref[hh, rows, :]
                    a = jnp.exp(z - (cc + jnp.concatenate([carry, carry], axis=1)))
                    carry_ref[hh, rows, :] = carry + rs_ref[slot, hh, rows, :]
                a_ref[slot, hh, rows, :] = a.astype(BF16)

    def value_stage(j, slot, *, first=False):
        vj = v_ref[0, pl.ds(pl.multiple_of(j * t, t), t), :]
        av = _dot(a_ref[slot].reshape(HEADS_PER_BLOCK * t, t), vj)
        for hh in range(HEADS_PER_BLOCK):
            av_h = av[hh * t:(hh + 1) * t, :]
            acc_ref[hh] = av_h if first else acc_ref[hh] + av_h

    n_past = qi

    @pl.when(n_past == 0)
    def _():
        score_stage(qi, 0)
        softplus_stage(0, diagonal=True)
        suffix_stage(0)
        weight_stage(0, diagonal=True)
        value_stage(qi, 0, first=True)

    @pl.when(n_past > 0)
    def _():
        last = n_past - 1

        def key_tile(i):
            return jnp.clip(last - i, 0, last)

        score_stage(qi, 2)
        score_stage(key_tile(0), 0)
        softplus_stage(2, diagonal=True)
        suffix_stage(2)
        softplus_stage(0)
        suffix_stage(0)
        weight_stage(2, diagonal=True)
        value_stage(qi, 2, first=True)
        weight_stage(0)
        value_stage(key_tile(0), 0)

        def trip(tt, first=False):
            for k in range(_SB_SLOTS):
                i = 1 + _SB_SLOTS * tt + k
                score_stage(key_tile(i), (1 + k) % _SB_SLOTS, valid=i < n_past)
                if not (first and k < 1):
                    softplus_stage(k % _SB_SLOTS)
                    suffix_stage(k % _SB_SLOTS)
                if not (first and k < 2):
                    weight_stage((k - 1) % _SB_SLOTS)
                    value_stage(key_tile(i - 2), (k - 1) % _SB_SLOTS)
            return tt + 1, jnp.min(carry_ref[...])

        @pl.when((n_past > 1) & (jnp.min(carry_ref[...]) < SB_DEAD_MASS))
        def _():
            n_trips = (n_past - 1 + 2 + _SB_SLOTS - 1) // _SB_SLOTS
            lax.while_loop(lambda st: (st[0] < n_trips) & (st[1] < SB_DEAD_MASS), lambda st: trip(st[0]),
                           trip(jnp.int32(0), first=True))

    o_ref[0] = jnp.where(lane < HEAD_DIM, acc_ref[0], acc_ref[1]).astype(BF16)


def _sb_attention(qkv):
    b, s, _ = qkv.shape
    t = SB_TILE
    assert s % t == 0
    n_pairs = N_HEADS_SB // HEADS_PER_BLOCK
    u = (lax.broadcasted_iota(jnp.int32, (t, t), 0) >= lax.broadcasted_iota(jnp.int32, (t, t), 1)).astype(BF16)
    return pl.pallas_call(
        _sb_kernel,
        out_shape=jax.ShapeDtypeStruct((b, s, N_HEADS_SB * HEAD_DIM), BF16),
        grid=(b, n_pairs, s // t),
        in_specs=[pl.BlockSpec((1, t, LANES), lambda bi, p, i: (bi, i, Q_BLK0 + SB_BLK + p)),
                  pl.BlockSpec((1, s, LANES), lambda bi, p, i: (bi, 0, K_BLK0 + SB_BLK + p)),
                  pl.BlockSpec((1, s, LANES), lambda bi, p, i: (bi, 0, V_BLK0 + SB_BLK + p)),
                  pl.BlockSpec((t, t), lambda bi, p, i: (0, 0))],
        out_specs=pl.BlockSpec((1, t, LANES), lambda bi, p, i: (bi, i, p)),
        scratch_shapes=[pltpu.VMEM((HEADS_PER_BLOCK, t, 2 * LANES), BF16),
                        pltpu.VMEM((HEADS_PER_BLOCK, t, LANES), F32),
                        pltpu.VMEM((HEADS_PER_BLOCK, t, LANES), F32),
                        pltpu.VMEM((_SB_SLOTS, HEADS_PER_BLOCK, t, t), F32),
                        pltpu.VMEM((_SB_SLOTS, 2 * HEADS_PER_BLOCK * t, t), BF16),
                        pltpu.VMEM((_SB_SLOTS, 2 * HEADS_PER_BLOCK * t, t), F32),
                        pltpu.VMEM((_SB_SLOTS, HEADS_PER_BLOCK, t, LANES), F32),
                        pltpu.VMEM((_SB_SLOTS, HEADS_PER_BLOCK, t, t), BF16)],
        compiler_params=_params(3),
        name="sb_attn",
    )(qkv, qkv, qkv, u)


DIL_SPAN = 2048


def _dil_kernel(q_ref, k_ref, v_ref, kp_ref, vp_ref, o_ref, qf_ref, kf_ref, vf_ref, oc_ref, lse_ref, bias_ref):
    pair = pl.program_id(1)
    i = pl.program_id(2)
    n = DIL_N
    span = DIL_SPAN
    lane = lax.broadcasted_iota(jnp.int32, (n, LANES), 1)
    colh = lax.broadcasted_iota(jnp.int32, (n, 2 * n), 1)

    @pl.when(i == 0)
    def _():
        row = lax.broadcasted_iota(jnp.int32, (n, 2 * n), 0)
        delta = row + n - colh
        valid = (delta >= 0) & (delta <= n)
        for ci, (_, d) in enumerate(DIL_CONFIGS):
            for hh in range(HEADS_PER_BLOCK):
                slope = jnp.asarray(2.0 ** -(hh + 1), F32)
                for p in range(1, N_HEADS_DIL // HEADS_PER_BLOCK):
                    slope = jnp.where(pair == p, 2.0 ** -(2 * p + hh + 1), slope)
                bias_ref[ci, hh] = jnp.where(valid, -slope * (delta * d).astype(F32), NEG_INF)

    qf_ref[...] = q_ref[0].astype(F32)
    kf_ref[0:span, :] = kp_ref[0].astype(F32)
    kf_ref[span:2 * span, :] = k_ref[0].astype(F32)
    vf_ref[0:span, :] = vp_ref[0].astype(F32)
    vf_ref[span:2 * span, :] = v_ref[0].astype(F32)

    def rows(start, size, d):
        return pl.ds(start, size) if d == 1 else pl.ds(start, size, stride=d)

    def unit(ci, d, r, bl):
        q0 = r + bl * (n * d)
        qu = qf_ref[rows(q0, n, d), :].astype(BF16)
        ku = kf_ref[rows(span + q0 - n * d, 2 * n, d), :].astype(BF16)
        vu = vf_ref[rows(span + q0 - n * d, 2 * n, d), :].astype(BF16)
        no_prev = jnp.where(i == 0, NEG_INF, 0.0) if bl == 0 else None
        qms = [jnp.where((lane >= hh * HEAD_DIM) & (lane < (hh + 1) * HEAD_DIM), qu, jnp.zeros_like(qu))
               * jnp.asarray(SCALE, BF16) for hh in range(HEADS_PER_BLOCK)]
        s_both = _dot_nt(jnp.concatenate(qms, axis=0), ku)
        ps, dens, lse_h = [], [], []
        for hh in range(HEADS_PER_BLOCK):
            s = s_both[hh * n:(hh + 1) * n, :] + bias_ref[ci, hh]
            if no_prev is not None:
                s = s + jnp.where(colh < n, no_prev, 0.0)
            m = jnp.max(s, axis=1, keepdims=True)
            p = jnp.exp(s - m)
            den = jnp.sum(p, axis=1, keepdims=True)
            ps.append(p.astype(BF16))
            dens.append(den)
            lse_h.append(jnp.broadcast_to(m + jnp.log(den), (n, LANES)))
        pv = _dot(jnp.concatenate(ps, axis=0), vu)
        o_h = [pv[hh * n:(hh + 1) * n, :] / dens[hh] for hh in range(HEADS_PER_BLOCK)]
        oc_ref[ci, rows(q0, n, d), :] = jnp.where(lane < HEAD_DIM, o_h[0], o_h[1])
        lse_ref[ci, rows(q0, n, d), :] = jnp.where(lane < HEAD_DIM, lse_h[0], lse_h[1])

    n_units = span // n
    for ci, (_, d) in enumerate(DIL_CONFIGS):
        per_res = n_units // d

        for u in range(n_units):
            unit(ci, d, u // per_res, u % per_res)

    mc = 256
    for c0 in range(0, span, mc):
        l1, l2, l3 = (lse_ref[ci, c0:c0 + mc, :] for ci in range(3))
        lmax = jnp.maximum(jnp.maximum(l1, l2), l3)
        e1, e2, e3 = jnp.exp(l1 - lmax), jnp.exp(l2 - lmax), jnp.exp(l3 - lmax)
        mix = (e1 * oc_ref[0, c0:c0 + mc, :] + e2 * oc_ref[1, c0:c0 + mc, :] + e3 * oc_ref[2, c0:c0 + mc, :]) \
            / (e1 + e2 + e3)
        o_ref[0, c0:c0 + mc, :] = mix.astype(BF16)


def _dilated_mixture(qkv):
    b, s, _ = qkv.shape
    span = DIL_SPAN
    assert s % span == 0 and all(w <= span and span % (DIL_N * d) == 0 for w, d in DIL_CONFIGS)
    n_pairs = N_HEADS_DIL // HEADS_PER_BLOCK
    n_cfg = len(DIL_CONFIGS)

    def cur(off):
        return lambda bi, p, i: (bi, i, off + DIL_BLK + p)

    def prev(off):
        return lambda bi, p, i: (bi, jnp.maximum(i - 1, 0), off + DIL_BLK + p)

    blk = (1, span, LANES)
    return pl.pallas_call(
        _dil_kernel,
        out_shape=jax.ShapeDtypeStruct((b, s, N_HEADS_DIL * HEAD_DIM), BF16),
        grid=(b, n_pairs, s // span),
        in_specs=[pl.BlockSpec(blk, cur(Q_BLK0)), pl.BlockSpec(blk, cur(K_BLK0)), pl.BlockSpec(blk, cur(V_BLK0)),
                  pl.BlockSpec(blk, prev(K_BLK0)), pl.BlockSpec(blk, prev(V_BLK0))],
        out_specs=pl.BlockSpec(blk, lambda bi, p, i: (bi, i, p)),
        scratch_shapes=[pltpu.VMEM((span, LANES), F32),
                        pltpu.VMEM((2 * span, LANES), F32),
                        pltpu.VMEM((2 * span, LANES), F32),
                        pltpu.VMEM((n_cfg, span, LANES), F32),
                        pltpu.VMEM((n_cfg, span, LANES), F32),
                        pltpu.VMEM((n_cfg, HEADS_PER_BLOCK, DIL_N, 2 * DIL_N), F32)],
        compiler_params=_params(3),
        name="dilated_attn",
    )(qkv, qkv, qkv, qkv, qkv)


def _rms(x, g):
    return x * lax.rsqrt(jnp.mean(x * x, axis=-1, keepdims=True) + NORM_EPS) * g


def _outproj_kernel(x_ref, oa_ref, ob_ref, oc_ref, g_ref, w_ref, out_ref):
    g = g_ref[...]
    wa = N_HEADS_MOBA * HEAD_DIM
    wb = wa + N_HEADS_SB * HEAD_DIM
    rows = 256
    for c in range(x_ref.shape[0] // rows):
        r = slice(c * rows, (c + 1) * rows)
        y = jnp.concatenate([_rms(oa_ref[r, :].astype(F32), g[:, :wa]),
                             _rms(ob_ref[r, :].astype(F32), g[:, wa:wb]),
                             _rms(oc_ref[r, :].astype(F32), g[:, wb:])], axis=1).astype(BF16)
        out_ref[r, :] = x_ref[r, :] + _dot(y, w_ref[...])


def _out_proj(x2, oa, ob, oc, g, w, *, tm=1024):
    t, d = x2.shape
    row = lambda i: (i, 0)
    const = lambda i: (0, 0)
    return pl.pallas_call(
        _outproj_kernel,
        out_shape=jax.ShapeDtypeStruct((t, d), F32),
        grid=(t // tm,),
        in_specs=[pl.BlockSpec((tm, d), row),
                  pl.BlockSpec((tm, oa.shape[1]), row), pl.BlockSpec((tm, ob.shape[1]), row),
                  pl.BlockSpec((tm, oc.shape[1]), row),
                  pl.BlockSpec((1, d), const), pl.BlockSpec((d, d), const)],
        out_specs=pl.BlockSpec((tm, d), row),
        compiler_params=_params(1),
        name="out_proj",
    )(x2, oa, ob, oc, g.reshape(1, d), w)


_EXP_LANE0 = N_GROUPS
_MOE_ROWS = 256
_MOE_CHUNK = 128


def _moe_kernel(x_ref, g_ref, wrh_ref, wrl_ref, br_ref, ltri_ref, fg_ref, wg_ref, wu_ref, wd_ref, out_ref,
                h_ref, hs_ref, cws_ref, ys_ref, pt_ref, pos_ref, nck_ref, cb_ref, *, n_sorted, final_norm):
    e = pl.program_id(1)
    tm = x_ref.shape[0]
    lane = lax.broadcasted_iota(jnp.int32, (tm, LANES), 1)
    lane1 = lax.broadcasted_iota(jnp.int32, (1, LANES), 1)

    @pl.when(e == 0)
    def _():
        x = x_ref[...]
        h = _rms(x, g_ref[...])
        hh, hl = _split2(h)
        logits = _dot(hh, wrh_ref[...]) + _dot(hh, wrl_ref[...]) + _dot(hl, wrh_ref[...]) + br_ref[...]
        lane_f = lane.astype(F32)
        big = float(LANES)
        gl = jnp.where(lane < N_GROUPS, logits, -jnp.inf)
        gmax = jnp.max(gl, axis=1, keepdims=True)
        gidx = jnp.min(jnp.where(gl == gmax, lane_f, big), axis=1, keepdims=True)
        g_w = 1.0 / jnp.sum(jnp.exp(gl - gmax), axis=1, keepdims=True)
        lane_group = ((lane - _EXP_LANE0) // EXPERTS_PER_GROUP).astype(F32)
        in_group = (lane >= _EXP_LANE0) & (lane < _EXP_LANE0 + N_EXPERTS) & (lane_group == gidx)
        el = jnp.where(in_group, logits, -jnp.inf)
        v1 = jnp.max(el, axis=1, keepdims=True)
        i1 = jnp.min(jnp.where(el == v1, lane_f, big), axis=1, keepdims=True)
        el2 = jnp.where(lane_f == i1, -jnp.inf, el)
        v2 = jnp.max(el2, axis=1, keepdims=True)
        i2 = jnp.min(jnp.where(el2 == v2, lane_f, big), axis=1, keepdims=True)
        r = jnp.exp(v2 - v1)
        w1 = g_w / (1.0 + r)
        w2 = g_w * r / (1.0 + r)
        comb = jnp.where(lane_f == i1, w1, 0.0) + jnp.where(lane_f == i2, w2, 0.0)

        onehot = jnp.where((lane_f == gidx) & (lane < N_GROUPS), 1.0, 0.0)
        before = _dot(ltri_ref[...], onehot.astype(BF16))
        rank = jnp.sum(onehot * before, axis=1, keepdims=True)
        count = jnp.sum(onehot, axis=0, keepdims=True)
        chunks = jnp.floor((count + (_MOE_CHUNK - 1.0)) * (1.0 / _MOE_CHUNK))
        nck_ref[...] = jnp.broadcast_to(chunks, nck_ref.shape)
        pos = rank
        start = jnp.float32(0.0)
        for gi in range(N_GROUPS):
            pos = pos + jnp.where(gidx == float(gi), start, 0.0)
            start = start + jnp.max(jnp.where(lane1 == gi, chunks, 0.0)) * _MOE_CHUNK
        pos_rep = jnp.broadcast_to(pos, (tm, LANES))
        pos_ref[...] = pos_rep
        pos_t = jnp.transpose(pos_rep)[0:1, :]
        h_ref[...] = hh
        c_hi, c_lo = _split2(comb)
        for c in range(n_sorted // _MOE_ROWS):
            rows = slice(c * _MOE_ROWS, (c + 1) * _MOE_ROWS)
            slot = (lax.broadcasted_iota(jnp.int32, (_MOE_ROWS, tm), 0) + c * _MOE_ROWS).astype(F32)
            perm = jnp.where(slot == pos_t, 1.0, 0.0).astype(BF16)
            hs_ref[rows, :] = _dot(perm, h_ref[...]).astype(BF16)
            cws_ref[rows, :] = _dot(perm, c_hi) + _dot(perm, c_lo)
        ys_ref[...] = jnp.zeros_like(ys_ref)
        cb_ref[0] = 0

    lane_c = lax.broadcasted_iota(jnp.int32, (_MOE_CHUNK, LANES), 1)
    n_chunks = jnp.max(jnp.where(lane1 == e, nck_ref[0:1, :], 0.0))

    def chunk(state):
        cb, cf = state
        rows = pl.ds(pl.multiple_of(cb * _MOE_CHUNK, _MOE_CHUNK), _MOE_CHUNK)
        hc = hs_ref[rows, :]
        cwc = cws_ref[rows, :]
        acts = []
        for k in range(EXPERTS_PER_GROUP):
            gate = _dot(hc, wg_ref[0, k])
            up = _dot(hc, wu_ref[0, k])
            cw = jnp.sum(jnp.where(lane_c == _EXP_LANE0 + e * EXPERTS_PER_GROUP + k, cwc, 0.0),
                         axis=1, keepdims=True)
            acts.append((gate / (1.0 + jnp.exp(-gate)) * up * cw).astype(BF16))
        ys_ref[rows, :] = _dot(jnp.concatenate(acts, axis=1), wd_ref[0]).astype(BF16)
        return cb + 1, cf + 1.0

    cb_end, _ = lax.while_loop(lambda st: st[1] < n_chunks, chunk, (cb_ref[0], jnp.float32(0.0)))
    cb_ref[0] = cb_end

    @pl.when(e == N_GROUPS - 1)
    def _():
        pos_rep2 = jnp.concatenate([pos_ref[...], pos_ref[...]], axis=1)
        for c in range(n_sorted // _MOE_ROWS):
            slot = (lax.broadcasted_iota(jnp.int32, (tm, _MOE_ROWS), 1) + c * _MOE_ROWS).astype(F32)
            perm_t = jnp.where(slot == pos_rep2, 1.0, 0.0).astype(BF16)
            pt_ref[:, c * _MOE_ROWS:(c + 1) * _MOE_ROWS] = perm_t
        y = x_ref[...] + _dot(pt_ref[...], ys_ref[...])
        out_ref[...] = _rms(y, fg_ref[...]) if final_norm else y


def _moe(x2, g, wr_hi, wr_lo, br, wg, wu, wd, final_g, *, final_norm, tm=1024):
    t, d = x2.shape
    f = wg.shape[3]
    n_sorted = tm + N_GROUPS * _MOE_CHUNK
    assert n_sorted % _MOE_ROWS == 0
    ltri = (lax.broadcasted_iota(jnp.int32, (tm, tm), 0) > lax.broadcasted_iota(jnp.int32, (tm, tm), 1)).astype(BF16)
    row = lambda i, e: (i, 0)
    const = lambda i, e: (0, 0)
    return pl.pallas_call(
        functools.partial(_moe_kernel, n_sorted=n_sorted, final_norm=final_norm),
        out_shape=jax.ShapeDtypeStruct((t, d), F32),
        grid=(t // tm, N_GROUPS),
        in_specs=[pl.BlockSpec((tm, d), row),
                  pl.BlockSpec((1, d), const),
                  pl.BlockSpec((d, LANES), const), pl.BlockSpec((d, LANES), const),
                  pl.BlockSpec((1, LANES), const),
                  pl.BlockSpec((tm, tm), const),
                  pl.BlockSpec((1, d), const),
                  pl.BlockSpec((1, EXPERTS_PER_GROUP, d, f), lambda i, e: (e, 0, 0, 0)),
                  pl.BlockSpec((1, EXPERTS_PER_GROUP, d, f), lambda i, e: (e, 0, 0, 0)),
                  pl.BlockSpec((1, EXPERTS_PER_GROUP * f, d), lambda i, e: (e, 0, 0))],
        out_specs=pl.BlockSpec((tm, d), row),
        scratch_shapes=[pltpu.VMEM((tm, d), BF16),
                        pltpu.VMEM((n_sorted, d), BF16),
                        pltpu.VMEM((n_sorted, LANES), F32),
                        pltpu.VMEM((n_sorted, d), BF16),
                        pltpu.VMEM((tm, n_sorted), BF16),
                        pltpu.VMEM((tm, LANES), F32),
                        pltpu.VMEM((8, LANES), F32),
                        pltpu.SMEM((1,), jnp.int32)],
        compiler_params=_params(2),
        name="hier_moe",
    )(x2, g.reshape(1, d), wr_hi, wr_lo, br, ltri, final_g.reshape(1, d), wg, wu, wd)


def _router_weights(w_gr, b_gr, w_er, b_er):
    d = w_gr.shape[0]
    w = jnp.concatenate([w_gr, jnp.moveaxis(w_er, 0, 1).reshape(d, N_EXPERTS)], axis=1)
    w = jnp.pad(w, ((0, 0), (0, LANES - w.shape[1])))
    bias = jnp.pad(jnp.concatenate([b_gr, b_er.reshape(-1)]), (0, LANES - N_GROUPS - N_EXPERTS))
    hi, lo = _split2(w)
    return hi, lo, bias.reshape(1, LANES)


def _layer(x2, b, s, ln1_g, w_in, mix_g, w_out, ln2_g, w_gr, b_gr, w_er, b_er, w_gate, w_up, w_down,
           final_g, final_norm):
    t, d = x2.shape
    qkv = _qkv_proj(x2, ln1_g, w_in.astype(BF16)).reshape(b, s, 3 * d)
    oa = _moba_attention(qkv)
    ob = _sb_attention(qkv)
    oc = _dilated_mixture(qkv)
    x2 = _out_proj(x2, oa.reshape(t, -1), ob.reshape(t, -1), oc.reshape(t, -1), mix_g, w_out.astype(BF16))
    wr_hi, wr_lo, br = _router_weights(w_gr, b_gr, w_er, b_er)
    f = w_gate.shape[-1]
    return _moe(x2, ln2_g, wr_hi, wr_lo, br, w_gate.astype(BF16), w_up.astype(BF16),
                w_down.reshape(N_GROUPS, EXPERTS_PER_GROUP * f, d).astype(BF16), final_g, final_norm=final_norm)


def kernel(x, ln1_g, w_in, mix_norm_g, w_out, ln2_g, w_group_router, b_group_router,
           w_expert_router, b_expert_router, w_gate, w_up, w_down, final_norm_g):
    b, s, d = x.shape
    x2 = x.reshape(b * s, d)
    depth = ln1_g.shape[0]
    for l in range(depth):
        x2 = _layer(x2, b, s, ln1_g[l], w_in[l], mix_norm_g[l], w_out[l], ln2_g[l],
                    w_group_router[l], b_group_router[l], w_expert_router[l], b_expert_router[l],
                    w_gate[l], w_up[l], w_down[l], final_norm_g, final_norm=(l == depth - 1))
    return x2.reshape(b, s, d)
```

```python
import functools

import jax
import jax.numpy as jnp
from jax import lax
from jax.experimental import pallas as pl
from jax.experimental.pallas import tpu as pltpu

F32 = jnp.float32
BF16 = jnp.bfloat16

D_MODEL = 1024
HEAD_DIM = 64
N_HEADS = 16
LANES = 128
HEADS_PER_BLOCK = LANES // HEAD_DIM
N_HEADS_MOBA = 4
N_HEADS_SB = 4
N_HEADS_DIL = 8
MOBA_BLOCK = 256
MOBA_TOPK = 3
DIL_CONFIGS = ((128, 1), (512, 4), (2048, 16))
DIL_N = 128
N_GROUPS = 4
EXPERTS_PER_GROUP = 4
N_EXPERTS = N_GROUPS * EXPERTS_PER_GROUP
D_EXPERT = 256
NORM_EPS = 1e-6
NEG_INF = -1e30
SCALE = HEAD_DIM ** -0.5

Q_BLK0 = 0
K_BLK0 = D_MODEL // LANES
V_BLK0 = 2 * D_MODEL // LANES
SB_BLK = N_HEADS_MOBA // HEADS_PER_BLOCK
DIL_BLK = (N_HEADS_MOBA + N_HEADS_SB) // HEADS_PER_BLOCK

VMEM_LIMIT = 56 * 1024 * 1024


def _params(n_axes, vmem=VMEM_LIMIT):
    return pltpu.CompilerParams(dimension_semantics=("arbitrary",) * n_axes,
                                vmem_limit_bytes=vmem)


def _dot_nt(a, b):
    return lax.dot_general(a, b, (((1,), (1,)), ((), ())), preferred_element_type=F32)


def _dot(a, b):
    return jnp.dot(a, b, preferred_element_type=F32)


def _split3(x):
    hi = x.astype(BF16)
    r1 = x - hi.astype(F32)
    mid = r1.astype(BF16)
    lo = (r1 - mid.astype(F32)).astype(BF16)
    return hi, mid, lo


def _split2(x):
    hi = x.astype(BF16)
    lo = (x - hi.astype(F32)).astype(BF16)
    return hi, lo


def _qkv_kernel(x_ref, g_ref, w_ref, o_ref, *, rows, tn):
    for c in range(x_ref.shape[0] // rows):
        r = slice(c * rows, (c + 1) * rows)
        x = x_ref[r, :]
        ms = jnp.mean(x * x, axis=-1, keepdims=True)
        h = (x * lax.rsqrt(ms + NORM_EPS) * g_ref[...]).astype(BF16)
        for j in range(w_ref.shape[1] // tn):
            o_ref[r, j * tn:(j + 1) * tn] = _dot(h, w_ref[:, j * tn:(j + 1) * tn]).astype(BF16)


def _qkv_proj(x2, g, w, *, tm=1024, rows=256, tn=1024):
    t, d = x2.shape
    n = w.shape[1]
    return pl.pallas_call(
        functools.partial(_qkv_kernel, rows=rows, tn=tn),
        out_shape=jax.ShapeDtypeStruct((t, n), BF16),
        grid=(t // tm,),
        in_specs=[pl.BlockSpec((tm, d), lambda i: (i, 0)),
                  pl.BlockSpec((1, d), lambda i: (0, 0)),
                  pl.BlockSpec((d, n), lambda i: (0, 0))],
        out_specs=pl.BlockSpec((tm, n), lambda i: (i, 0)),
        compiler_params=_params(1),
        name="qkv_proj",
    )(x2, g.reshape(1, d), w)


_MB_MAXBLK = 32
_MB_POS0 = 32
_MB_KILL = 40
_MB_BLK0 = 64


_ROW_CHUNK = 32


MOBA_DEAD_GAP = 110.0


def _moba_kernel(q_ref, k_ref, v_ref, o_ref, kmean_ref, kn2_ref, kx_ref, qaug_ref, m_ref, alpha_ref, acc_ref,
                 s_ref, p_ref, *, n_blk):
    pair = pl.program_id(1)
    qi = pl.program_id(2)
    blk = MOBA_BLOCK
    lane = lax.broadcasted_iota(jnp.int32, (blk, LANES), 1)
    row = lax.broadcasted_iota(jnp.int32, (blk, LANES), 0)
    slopes = [jnp.where(pair == 0, 2.0 ** (-2 * (hh + 1)), 2.0 ** (-2 * (hh + 3))).astype(F32)
              for hh in range(HEADS_PER_BLOCK)]

    @pl.when(qi == 0)
    def _():
        kmean_ref[...] = jnp.zeros_like(kmean_ref)
        kn2_ref[...] = jnp.zeros_like(kn2_ref)

        def body(n, c):
            kb = k_ref[0, pl.ds(pl.multiple_of(n * blk, blk), blk), :].astype(F32)
            kmean_ref[pl.ds(n, 1), :] = jnp.sum(kb, axis=0, keepdims=True) * (1.0 / blk)
            for hh in range(HEADS_PER_BLOCK):
                in_head = (lane >= hh * HEAD_DIM) & (lane < (hh + 1) * HEAD_DIM)
                n2 = jnp.sum(jnp.where(in_head, kb * kb, 0.0), axis=1, keepdims=True)
                n2 = jnp.max(jnp.broadcast_to(n2, (blk, LANES)), axis=0, keepdims=True)
                kn2_ref[hh] = jnp.maximum(kn2_ref[hh], jnp.broadcast_to(n2, (8, LANES)))
            return c

        lax.fori_loop(0, n_blk, body, 0)
        kx = jnp.zeros((blk, LANES), F32)
        for hh in range(HEADS_PER_BLOCK):
            kx = jnp.where(lane == _MB_POS0 + 2 * hh, slopes[hh] * ((row // LANES) * LANES).astype(F32), kx)
            kx = jnp.where(lane == _MB_POS0 + 2 * hh + 1, slopes[hh] * (row % LANES).astype(F32), kx)
        kx_ref[...] = kx.astype(BF16)

    q2 = q_ref[0]
    km_parts = _split3(kmean_ref[...])
    blk_row = lax.broadcasted_iota(jnp.int32, (_MB_MAXBLK, blk), 0)
    blk_row_f = blk_row.astype(F32)
    for hh in range(HEADS_PER_BLOCK):
        in_head = (lane >= hh * HEAD_DIM) & (lane < (hh + 1) * HEAD_DIM)
        qm = jnp.where(in_head, q2, jnp.zeros_like(q2))
        gate = _dot_nt(qm, km_parts[0]) + _dot_nt(qm, km_parts[1]) + _dot_nt(qm, km_parts[2])
        gate = jnp.transpose(gate)[0:_MB_MAXBLK, :]
        gate = jnp.where(blk_row < qi, gate, NEG_INF)
        gate = jnp.where(blk_row < n_blk, gate, -jnp.inf)
        sel = jnp.zeros((_MB_MAXBLK, blk), jnp.bool_)
        for _ in range(MOBA_TOPK):
            gmax = jnp.max(gate, axis=0, keepdims=True)
            first = jnp.min(jnp.where(gate == gmax, blk_row_f, float(LANES)), axis=0, keepdims=True)
            pick = blk_row_f == first
            sel = sel | pick
            gate = jnp.where(pick, -jnp.inf, gate)
        sel = sel & (blk_row < qi)
        sel_bias = jnp.where(sel | (blk_row >= n_blk), 0.0, NEG_INF)
        extra = jnp.transpose(jnp.concatenate(
            [sel_bias, jnp.zeros((LANES - _MB_MAXBLK, blk), F32)], axis=0))
        extra = jnp.where((lane == _MB_POS0 + 2 * hh) | (lane == _MB_POS0 + 2 * hh + 1) | (lane == _MB_KILL),
                          1.0, extra)
        blk_lane = lane - (_MB_BLK0 + _MB_MAXBLK * hh)
        extra = jnp.where((blk_lane >= 0) & (blk_lane < _MB_MAXBLK),
                          slopes[hh] * (blk_lane * blk).astype(F32), extra)
        qaug_ref[hh] = jnp.concatenate([qm * jnp.asarray(SCALE, BF16), extra.astype(BF16)], axis=1)
        m_ref[hh] = jnp.full((blk, LANES), -jnp.inf, F32)
        acc_ref[hh] = jnp.zeros((blk, LANES), F32)

    lane1 = lax.broadcasted_iota(jnp.int32, (1, LANES), 1)
    rc = _ROW_CHUNK

    _LAST = HEADS_PER_BLOCK - 1

    def score_stage(j, buf, *, is_own=False, valid=True, last_head_only=False):
        kj = k_ref[0, pl.ds(pl.multiple_of(j * blk, blk), blk), :]
        ind = (lane1 % _MB_MAXBLK == j) & (lane1 >= _MB_BLK0) if is_own else \
              (lane1 % _MB_MAXBLK == j) & ((lane1 < _MB_MAXBLK) | (lane1 >= _MB_BLK0))
        kill = jnp.where(lane1 == _MB_KILL, jnp.where(valid, 0.0, NEG_INF), 0.0)
        side = jnp.broadcast_to(jnp.where(ind, 1.0, kill), (blk, LANES)).astype(BF16)
        kx = jnp.where(ind | (lane1 == _MB_KILL), side, kx_ref[...])
        k_aug = jnp.concatenate([kj, kx], axis=1)
        if last_head_only:
            s_ref[buf, _LAST] = _dot_nt(qaug_ref[_LAST], k_aug)
            return
        s = _dot_nt(qaug_ref[...].reshape(HEADS_PER_BLOCK * blk, 2 * LANES), k_aug)
        s_ref[buf] = s.reshape(HEADS_PER_BLOCK, blk, blk)

    def softmax_stage(buf, *, is_own=False, last_head_only=False):
        for hh in ((_LAST,) if last_head_only else range(HEADS_PER_BLOCK)):
            for c in range(blk // rc):
                rows = slice(c * rc, (c + 1) * rc)
                s = s_ref[buf, hh, rows, :]
                if is_own:
                    col_c = lax.broadcasted_iota(jnp.int32, (rc, blk), 1)
                    row_c = lax.broadcasted_iota(jnp.int32, (rc, blk), 0) + c * rc
                    s = jnp.where(col_c <= row_c, s, NEG_INF)
                m_old = m_ref[hh, rows, :]
                m_new = jnp.maximum(m_old, jnp.max(s, axis=1, keepdims=True))
                alpha_ref[buf, hh, rows, :] = jnp.exp(m_old - m_new)
                m_ref[hh, rows, :] = m_new
                p = jnp.exp(s - jnp.concatenate([m_new, m_new], axis=1))
                p_ref[buf, hh, rows, :] = p.astype(BF16)

    def value_stage(j, buf, *, last_head_only=False):
        vj = v_ref[0, pl.ds(pl.multiple_of(j * blk, blk), blk), :]
        if last_head_only:
            in_head1 = (lane1 >= _LAST * HEAD_DIM) & (lane1 < (_LAST + 1) * HEAD_DIM)
            pv = _dot(p_ref[buf, _LAST], jnp.where(in_head1, vj, jnp.ones_like(vj)))
            acc_ref[_LAST] = alpha_ref[buf, _LAST] * acc_ref[_LAST] + pv
            return
        v_aug = jnp.concatenate(
            [jnp.where((lane1 >= hh * HEAD_DIM) & (lane1 < (hh + 1) * HEAD_DIM), vj, jnp.ones_like(vj))
             for hh in range(HEADS_PER_BLOCK)], axis=1)
        pv = _dot(p_ref[buf].reshape(HEADS_PER_BLOCK * blk, blk), v_aug)
        for hh in range(HEADS_PER_BLOCK):
            acc_ref[hh] = alpha_ref[buf, hh] * acc_ref[hh] + pv[hh * blk:(hh + 1) * blk, hh * LANES:(hh + 1) * LANES]

    n_past = qi

    @pl.when(n_past == 0)
    def _():
        score_stage(qi, 0, is_own=True)
        softmax_stage(0, is_own=True)
        value_stage(qi, 0)

    @pl.when(n_past > 0)
    def _():
        last = n_past - 1

        def past(t):
            return jnp.clip(last - t, 0, last)

        score_stage(qi, 1, is_own=True)
        score_stage(past(0), 0)
        softmax_stage(1, is_own=True)
        value_stage(qi, 1)
        softmax_stage(0)
        score_stage(past(1), 1, valid=1 < n_past)

        n_past_f = jnp.full((1, LANES), n_past, jnp.int32).astype(F32)
        live = []
        for hh in range(HEADS_PER_BLOCK):
            in_head = (lane >= hh * HEAD_DIM) & (lane < (hh + 1) * HEAD_DIM)
            qf = q2.astype(F32)
            qn2 = jnp.sum(jnp.where(in_head, qf * qf, 0.0), axis=1, keepdims=True)
            qn2 = jnp.max(jnp.broadcast_to(qn2, (blk, LANES)), axis=0, keepdims=True)
            bound = jnp.sqrt(qn2 * kn2_ref[hh, 0:1, :]) * SCALE
            m_rel = m_ref[hh] - slopes[hh] * (row + qi * blk).astype(F32)
            m_min = jnp.min(m_rel, axis=0, keepdims=True)
            reach = (bound - m_min + MOBA_DEAD_GAP) / slopes[hh]
            live.append(jnp.max(jnp.minimum((reach - 1.0) / blk + 1.0, n_past_f)))
        n_last = live[_LAST]
        n_both = live[0]
        for hh in range(1, _LAST):
            n_both = jnp.maximum(n_both, live[hh])

        def pair(state, first_block=0, last_head_only=False):
            tt, tf, n_ok = state
            t = 2 * tt
            n_cap = n_last if last_head_only else n_both
            ok0 = (t >= first_block) & (t < n_past) & (tf < n_cap)
            ok1 = (t + 1 >= first_block) & (t + 1 < n_past) & (tf + 1.0 < n_cap)
            score_stage(past(t), 0, valid=ok0, last_head_only=last_head_only)
            softmax_stage(1, last_head_only=last_head_only)
            value_stage(past(t - 2), 0, last_head_only=last_head_only)
            score_stage(past(t + 1), 1, valid=ok1, last_head_only=last_head_only)
            softmax_stage(0, last_head_only=last_head_only)
            value_stage(past(t - 1), 1, last_head_only=last_head_only)
            return tt + 1, tf + 2.0, n_ok + jnp.where(ok0, 1, 0) + jnp.where(ok1, 1, 0)

        _, _, n_done = lax.while_loop(lambda st: st[1] < n_both + 2.0, pair,
                                      (jnp.int32(1), jnp.float32(2.0), jnp.int32(2)))
        tt0 = n_done // 2
        tf0 = jnp.max(jnp.full((1, LANES), 2 * tt0, jnp.int32).astype(F32))
        lax.while_loop(lambda st: st[1] < n_last + 2.0,
                       functools.partial(pair, first_block=n_done, last_head_only=True),
                       (tt0, tf0, jnp.int32(0)))

    acc0 = acc_ref[0]
    acc1 = acc_ref[1]
    o0 = acc0 / pltpu.roll(acc0, HEAD_DIM, axis=1)
    o1 = acc1 / pltpu.roll(acc1, HEAD_DIM, axis=1)
    o_ref[0] = jnp.where(lane < HEAD_DIM, o0, o1).astype(BF16)


def _moba_attention(qkv):
    b, s, _ = qkv.shape
    blk = MOBA_BLOCK
    n_blk = s // blk
    assert s % blk == 0 and MOBA_TOPK <= n_blk - 1 and n_blk <= _MB_MAXBLK
    n_pairs = N_HEADS_MOBA // HEADS_PER_BLOCK
    return pl.pallas_call(
        functools.partial(_moba_kernel, n_blk=n_blk),
        out_shape=jax.ShapeDtypeStruct((b, s, N_HEADS_MOBA * HEAD_DIM), BF16),
        grid=(b, n_pairs, n_blk),
        in_specs=[pl.BlockSpec((1, blk, LANES), lambda bi, p, i: (bi, i, Q_BLK0 + p)),
                  pl.BlockSpec((1, s, LANES), lambda bi, p, i: (bi, 0, K_BLK0 + p)),
                  pl.BlockSpec((1, s, LANES), lambda bi, p, i: (bi, 0, V_BLK0 + p))],
        out_specs=pl.BlockSpec((1, blk, LANES), lambda bi, p, i: (bi, i, p)),
        scratch_shapes=[pltpu.VMEM((LANES, LANES), F32),
                        pltpu.VMEM((HEADS_PER_BLOCK, 8, LANES), F32),
                        pltpu.VMEM((blk, LANES), BF16),
                        pltpu.VMEM((HEADS_PER_BLOCK, blk, 2 * LANES), BF16),
                        pltpu.VMEM((HEADS_PER_BLOCK, blk, LANES), F32),
                        pltpu.VMEM((2, HEADS_PER_BLOCK, blk, LANES), F32),
                        pltpu.VMEM((HEADS_PER_BLOCK, blk, LANES), F32),
                        pltpu.VMEM((2, HEADS_PER_BLOCK, blk, blk), F32),
                        pltpu.VMEM((2, HEADS_PER_BLOCK, blk, blk), BF16)],
        compiler_params=_params(3),
        name="moba_attn",
    )(qkv, qkv, qkv)


SB_TILE = 256


def _softplus(z):
    return jnp.maximum(z, 0.0) + jnp.log(1.0 + jnp.exp(-jnp.abs(z)))


_SB_SLOTS = 3
SB_DEAD_MASS = 128.0


def _sb_kernel(q_ref, k_ref, v_ref, u_ref, o_ref, qaug_ref, carry_ref, acc_ref,
               z_ref, sphl_ref, c_ref, rs_ref, a_ref):
    qi = pl.program_id(2)
    t = SB_TILE
    rc = _ROW_CHUNK
    lane = lax.broadcasted_iota(jnp.int32, (t, LANES), 1)
    lane1 = lax.broadcasted_iota(jnp.int32, (1, LANES), 1)
    q2 = q_ref[0]
    for hh in range(HEADS_PER_BLOCK):
        in_head = (lane >= hh * HEAD_DIM) & (lane < (hh + 1) * HEAD_DIM)
        qm = jnp.where(in_head, q2, jnp.zeros_like(q2)) * jnp.asarray(SCALE, BF16)
        qaug_ref[hh] = jnp.concatenate([qm, jnp.where(lane == 0, 1.0, 0.0).astype(BF16)], axis=1)

    def score_stage(j, slot, *, valid=True):
        kj = k_ref[0, pl.ds(pl.multiple_of(j * t, t), t), :]
        kill = jnp.where(lane1 == 0, jnp.where(valid, 0.0, NEG_INF), 0.0)
        k_aug = jnp.concatenate([kj, jnp.broadcast_to(kill, (t, LANES)).astype(BF16)], axis=1)
        z = _dot_nt(qaug_ref[...].reshape(HEADS_PER_BLOCK * t, 2 * LANES), k_aug)
        z_ref[slot] = z.reshape(HEADS_PER_BLOCK, t, t)

    def _past_mask(c):
        col_c = lax.broadcasted_iota(jnp.int32, (rc, t), 1)
        row_c = lax.broadcasted_iota(jnp.int32, (rc, t), 0) + c * rc
        return col_c < row_c

    def softplus_stage(slot, *, diagonal=False):
        for hh in range(HEADS_PER_BLOCK):
            for c in range(t // rc):
                rows = slice(c * rc, (c + 1) * rc)
                sp = _softplus(z_ref[slot, hh, rows, :])
                if diagonal:
                    sp = jnp.where(_past_mask(c), sp, 0.0)
                hi, lo = _split2(sp)
                sphl_ref[slot, pl.ds((2 * hh) * t + c * rc, rc), :] = hi
                sphl_ref[slot, pl.ds((2 * hh + 1) * t + c * rc, rc), :] = lo
                rs_ref[slot, hh, rows, :] = jnp.broadcast_to(jnp.sum(sp, axis=1, keepdims=True), (rc, LANES))

    def suffix_stage(slot):
        c_ref[slot] = _dot(sphl_ref[slot], u_ref[...])

    def weight_stage(slot, *, diagonal=False):
        for hh in range(HEADS_PER_BLOCK):
            for c in range(t // rc):
                rows = slice(c * rc, (c + 1) * rc)
                z = z_ref[slot, hh, rows, :]
                cc = c_ref[slot, pl.ds((2 * hh) * t + c * rc, rc), :] + \
                    c_ref[slot, pl.ds((2 * hh + 1) * t + c * rc, rc), :]
                if diagonal:
                    a = jnp.where(_past_mask(c), jnp.exp(z - cc), 0.0)
                    carry_ref[hh, rows, :] = rs_ref[slot, hh, rows, :]
                else:
                    carry = carry_ref[hh, rows, :]
                    a = jnp.exp(z - (cc + jnp.concatenate([carry, carry], axis=1)))
                    carry_ref[hh, rows, :] = carry + rs_ref[slot, hh, rows, :]
                a_ref[slot, hh, rows, :] = a.astype(BF16)

    def value_stage(j, slot, *, first=False):
        vj = v_ref[0, pl.ds(pl.multiple_of(j * t, t), t), :]
        av = _dot(a_ref[slot].reshape(HEADS_PER_BLOCK * t, t), vj)
        for hh in range(HEADS_PER_BLOCK):
            av_h = av[hh * t:(hh + 1) * t, :]
            acc_ref[hh] = av_h if first else acc_ref[hh] + av_h

    n_past = qi

    @pl.when(n_past == 0)
    def _():
        score_stage(qi, 0)
        softplus_stage(0, diagonal=True)
        suffix_stage(0)
        weight_stage(0, diagonal=True)
        value_stage(qi, 0, first=True)

    @pl.when(n_past > 0)
    def _():
        last = n_past - 1

        def key_tile(i):
            return jnp.clip(last - i, 0, last)

        score_stage(qi, 2)
        score_stage(key_tile(0), 0)
        softplus_stage(2, diagonal=True)
        suffix_stage(2)
        softplus_stage(0)
        suffix_stage(0)
        weight_stage(2, diagonal=True)
        value_stage(qi, 2, first=True)
        weight_stage(0)
        value_stage(key_tile(0), 0)

        def trip(tt, first=False):
            for k in range(_SB_SLOTS):
                i = 1 + _SB_SLOTS * tt + k
                score_stage(key_tile(i), (1 + k) % _SB_SLOTS, valid=i < n_past)
                if not (first and k < 1):
                    softplus_stage(k % _SB_SLOTS)
                    suffix_stage(k % _SB_SLOTS)
                if not (first and k < 2):
                    weight_stage((k - 1) % _SB_SLOTS)
                    value_stage(key_tile(i - 2), (k - 1) % _SB_SLOTS)
            return tt + 1, jnp.min(carry_ref[...])

        @pl.when((n_past > 1) & (jnp.min(carry_ref[...]) < SB_DEAD_MASS))
        def _():
            n_trips = (n_past - 1 + 2 + _SB_SLOTS - 1) // _SB_SLOTS
            lax.while_loop(lambda st: (st[0] < n_trips) & (st[1] < SB_DEAD_MASS), lambda st: trip(st[0]),
                           trip(jnp.int32(0), first=True))

    o_ref[0] = jnp.where(lane < HEAD_DIM, acc_ref[0], acc_ref[1]).astype(BF16)


def _sb_attention(qkv):
    b, s, _ = qkv.shape
    t = SB_TILE
    assert s % t == 0
    n_pairs = N_HEADS_SB // HEADS_PER_BLOCK
    u = (lax.broadcasted_iota(jnp.int32, (t, t), 0) >= lax.broadcasted_iota(jnp.int32, (t, t), 1)).astype(BF16)
    return pl.pallas_call(
        _sb_kernel,
        out_shape=jax.ShapeDtypeStruct((b, s, N_HEADS_SB * HEAD_DIM), BF16),
        grid=(b, n_pairs, s // t),
        in_specs=[pl.BlockSpec((1, t, LANES), lambda bi, p, i: (bi, i, Q_BLK0 + SB_BLK + p)),
                  pl.BlockSpec((1, s, LANES), lambda bi, p, i: (bi, 0, K_BLK0 + SB_BLK + p)),
                  pl.BlockSpec((1, s, LANES), lambda bi, p, i: (bi, 0, V_BLK0 + SB_BLK + p)),
                  pl.BlockSpec((t, t), lambda bi, p, i: (0, 0))],
        out_specs=pl.BlockSpec((1, t, LANES), lambda bi, p, i: (bi, i, p)),
        scratch_shapes=[pltpu.VMEM((HEADS_PER_BLOCK, t, 2 * LANES), BF16),
                        pltpu.VMEM((HEADS_PER_BLOCK, t, LANES), F32),
                        pltpu.VMEM((HEADS_PER_BLOCK, t, LANES), F32),
                        pltpu.VMEM((_SB_SLOTS, HEADS_PER_BLOCK, t, t), F32),
                        pltpu.VMEM((_SB_SLOTS, 2 * HEADS_PER_BLOCK * t, t), BF16),
                        pltpu.VMEM((_SB_SLOTS, 2 * HEADS_PER_BLOCK * t, t), F32),
                        pltpu.VMEM((_SB_SLOTS, HEADS_PER_BLOCK, t, LANES), F32),
                        pltpu.VMEM((_SB_SLOTS, HEADS_PER_BLOCK, t, t), BF16)],
        compiler_params=_params(3),
        name="sb_attn",
    )(qkv, qkv, qkv, u)


DIL_SPAN = 2048


def _dil_kernel(q_ref, k_ref, v_ref, kp_ref, vp_ref, o_ref, qf_ref, kf_ref, vf_ref, oc_ref, lse_ref, bias_ref):
    pair = pl.program_id(1)
    i = pl.program_id(2)
    n = DIL_N
    span = DIL_SPAN
    lane = lax.broadcasted_iota(jnp.int32, (n, LANES), 1)
    colh = lax.broadcasted_iota(jnp.int32, (n, 2 * n), 1)

    @pl.when(i == 0)
    def _():
        row = lax.broadcasted_iota(jnp.int32, (n, 2 * n), 0)
        delta = row + n - colh
        valid = (delta >= 0) & (delta <= n)
        for ci, (_, d) in enumerate(DIL_CONFIGS):
            for hh in range(HEADS_PER_BLOCK):
                slope = jnp.asarray(2.0 ** -(hh + 1), F32)
                for p in range(1, N_HEADS_DIL // HEADS_PER_BLOCK):
                    slope = jnp.where(pair == p, 2.0 ** -(2 * p + hh + 1), slope)
                bias_ref[ci, hh] = jnp.where(valid, -slope * (delta * d).astype(F32), NEG_INF)

    qf_ref[...] = q_ref[0].astype(F32)
    kf_ref[0:span, :] = kp_ref[0].astype(F32)
    kf_ref[span:2 * span, :] = k_ref[0].astype(F32)
    vf_ref[0:span, :] = vp_ref[0].astype(F32)
    vf_ref[span:2 * span, :] = v_ref[0].astype(F32)

    def rows(start, size, d):
        return pl.ds(start, size) if d == 1 else pl.ds(start, size, stride=d)

    def unit(ci, d, r, bl):
        q0 = r + bl * (n * d)
        qu = qf_ref[rows(q0, n, d), :].astype(BF16)
        ku = kf_ref[rows(span + q0 - n * d, 2 * n, d), :].astype(BF16)
        vu = vf_ref[rows(span + q0 - n * d, 2 * n, d), :].astype(BF16)
        no_prev = jnp.where(i == 0, NEG_INF, 0.0) if bl == 0 else None
        qms = [jnp.where((lane >= hh * HEAD_DIM) & (lane < (hh + 1) * HEAD_DIM), qu, jnp.zeros_like(qu))
               * jnp.asarray(SCALE, BF16) for hh in range(HEADS_PER_BLOCK)]
        s_both = _dot_nt(jnp.concatenate(qms, axis=0), ku)
        ps, dens, lse_h = [], [], []
        for hh in range(HEADS_PER_BLOCK):
            s = s_both[hh * n:(hh + 1) * n, :] + bias_ref[ci, hh]
            if no_prev is not None:
                s = s + jnp.where(colh < n, no_prev, 0.0)
            m = jnp.max(s, axis=1, keepdims=True)
            p = jnp.exp(s - m)
            den = jnp.sum(p, axis=1, keepdims=True)
            ps.append(p.astype(BF16))
            dens.append(den)
            lse_h.append(jnp.broadcast_to(m + jnp.log(den), (n, LANES)))
        pv = _dot(jnp.concatenate(ps, axis=0), vu)
        o_h = [pv[hh * n:(hh + 1) * n, :] / dens[hh] for hh in range(HEADS_PER_BLOCK)]
        oc_ref[ci, rows(q0, n, d), :] = jnp.where(lane < HEAD_DIM, o_h[0], o_h[1])
        lse_ref[ci, rows(q0, n, d), :] = jnp.where(lane < HEAD_DIM, lse_h[0], lse_h[1])

    n_units = span // n
    for ci, (_, d) in enumerate(DIL_CONFIGS):
        per_res = n_units // d

        for u in range(n_units):
            unit(ci, d, u // per_res, u % per_res)

    mc = 256
    for c0 in range(0, span, mc):
        l1, l2, l3 = (lse_ref[ci, c0:c0 + mc, :] for ci in range(3))
        lmax = jnp.maximum(jnp.maximum(l1, l2), l3)
        e1, e2, e3 = jnp.exp(l1 - lmax), jnp.exp(l2 - lmax), jnp.exp(l3 - lmax)
        mix = (e1 * oc_ref[0, c0:c0 + mc, :] + e2 * oc_ref[1, c0:c0 + mc, :] + e3 * oc_ref[2, c0:c0 + mc, :]) \
            / (e1 + e2 + e3)
        o_ref[0, c0:c0 + mc, :] = mix.astype(BF16)


def _dilated_mixture(qkv):
    b, s, _ = qkv.shape
    span = DIL_SPAN
    assert s % span == 0 and all(w <= span and span % (DIL_N * d) == 0 for w, d in DIL_CONFIGS)
    n_pairs = N_HEADS_DIL // HEADS_PER_BLOCK
    n_cfg = len(DIL_CONFIGS)

    def cur(off):
        return lambda bi, p, i: (bi, i, off + DIL_BLK + p)

    def prev(off):
        return lambda bi, p, i: (bi, jnp.maximum(i - 1, 0), off + DIL_BLK + p)

    blk = (1, span, LANES)
    return pl.pallas_call(
        _dil_kernel,
        out_shape=jax.ShapeDtypeStruct((b, s, N_HEADS_DIL * HEAD_DIM), BF16),
        grid=(b, n_pairs, s // span),
        in_specs=[pl.BlockSpec(blk, cur(Q_BLK0)), pl.BlockSpec(blk, cur(K_BLK0)), pl.BlockSpec(blk, cur(V_BLK0)),
                  pl.BlockSpec(blk, prev(K_BLK0)), pl.BlockSpec(blk, prev(V_BLK0))],
        out_specs=pl.BlockSpec(blk, lambda bi, p, i: (bi, i, p)),
        scratch_shapes=[pltpu.VMEM((span, LANES), F32),
                        pltpu.VMEM((2 * span, LANES), F32),
                        pltpu.VMEM((2 * span, LANES), F32),
                        pltpu.VMEM((n_cfg, span, LANES), F32),
                        pltpu.VMEM((n_cfg, span, LANES), F32),
                        pltpu.VMEM((n_cfg, HEADS_PER_BLOCK, DIL_N, 2 * DIL_N), F32)],
        compiler_params=_params(3),
        name="dilated_attn",
    )(qkv, qkv, qkv, qkv, qkv)


def _rms(x, g):
    return x * lax.rsqrt(jnp.mean(x * x, axis=-1, keepdims=True) + NORM_EPS) * g


def _outproj_kernel(x_ref, oa_ref, ob_ref, oc_ref, g_ref, w_ref, out_ref):
    g = g_ref[...]
    wa = N_HEADS_MOBA * HEAD_DIM
    wb = wa + N_HEADS_SB * HEAD_DIM
    rows = 256
    for c in range(x_ref.shape[0] // rows):
        r = slice(c * rows, (c + 1) * rows)
        y = jnp.concatenate([_rms(oa_ref[r, :].astype(F32), g[:, :wa]),
                             _rms(ob_ref[r, :].astype(F32), g[:, wa:wb]),
                             _rms(oc_ref[r, :].astype(F32), g[:, wb:])], axis=1).astype(BF16)
        out_ref[r, :] = x_ref[r, :] + _dot(y, w_ref[...])


def _out_proj(x2, oa, ob, oc, g, w, *, tm=1024):
    t, d = x2.shape
    row = lambda i: (i, 0)
    const = lambda i: (0, 0)
    return pl.pallas_call(
        _outproj_kernel,
        out_shape=jax.ShapeDtypeStruct((t, d), F32),
        grid=(t // tm,),
        in_specs=[pl.BlockSpec((tm, d), row),
                  pl.BlockSpec((tm, oa.shape[1]), row), pl.BlockSpec((tm, ob.shape[1]), row),
                  pl.BlockSpec((tm, oc.shape[1]), row),
                  pl.BlockSpec((1, d), const), pl.BlockSpec((d, d), const)],
        out_specs=pl.BlockSpec((tm, d), row),
        compiler_params=_params(1),
        name="out_proj",
    )(x2, oa, ob, oc, g.reshape(1, d), w)


_EXP_LANE0 = N_GROUPS
_MOE_ROWS = 256
_MOE_CHUNK = 128


def _moe_kernel(x_ref, g_ref, wrh_ref, wrl_ref, br_ref, ltri_ref, fg_ref, wg_ref, wu_ref, wd_ref, out_ref,
                h_ref, hs_ref, cws_ref, ys_ref, pt_ref, pos_ref, nck_ref, cb_ref, *, n_sorted, final_norm):
    e = pl.program_id(1)
    tm = x_ref.shape[0]
    lane = lax.broadcasted_iota(jnp.int32, (tm, LANES), 1)
    lane1 = lax.broadcasted_iota(jnp.int32, (1, LANES), 1)

    @pl.when(e == 0)
    def _():
        x = x_ref[...]
        h = _rms(x, g_ref[...])
        hh, hl = _split2(h)
        logits = _dot(hh, wrh_ref[...]) + _dot(hh, wrl_ref[...]) + _dot(hl, wrh_ref[...]) + br_ref[...]
        lane_f = lane.astype(F32)
        big = float(LANES)
        gl = jnp.where(lane < N_GROUPS, logits, -jnp.inf)
        gmax = jnp.max(gl, axis=1, keepdims=True)
        gidx = jnp.min(jnp.where(gl == gmax, lane_f, big), axis=1, keepdims=True)
        g_w = 1.0 / jnp.sum(jnp.exp(gl - gmax), axis=1, keepdims=True)
        lane_group = ((lane - _EXP_LANE0) // EXPERTS_PER_GROUP).astype(F32)
        in_group = (lane >= _EXP_LANE0) & (lane < _EXP_LANE0 + N_EXPERTS) & (lane_group == gidx)
        el = jnp.where(in_group, logits, -jnp.inf)
        v1 = jnp.max(el, axis=1, keepdims=True)
        i1 = jnp.min(jnp.where(el == v1, lane_f, big), axis=1, keepdims=True)
        el2 = jnp.where(lane_f == i1, -jnp.inf, el)
        v2 = jnp.max(el2, axis=1, keepdims=True)
        i2 = jnp.min(jnp.where(el2 == v2, lane_f, big), axis=1, keepdims=True)
        r = jnp.exp(v2 - v1)
        w1 = g_w / (1.0 + r)
        w2 = g_w * r / (1.0 + r)
        comb = jnp.where(lane_f == i1, w1, 0.0) + jnp.where(lane_f == i2, w2, 0.0)

        onehot = jnp.where((lane_f == gidx) & (lane < N_GROUPS), 1.0, 0.0)
        before = _dot(ltri_ref[...], onehot.astype(BF16))
        rank = jnp.sum(onehot * before, axis=1, keepdims=True)
        count = jnp.sum(onehot, axis=0, keepdims=True)
        chunks = jnp.floor((count + (_MOE_CHUNK - 1.0)) * (1.0 / _MOE_CHUNK))
        nck_ref[...] = jnp.broadcast_to(chunks, nck_ref.shape)
        pos = rank
        start = jnp.float32(0.0)
        for gi in range(N_GROUPS):
            pos = pos + jnp.where(gidx == float(gi), start, 0.0)
            start = start + jnp.max(jnp.where(lane1 == gi, chunks, 0.0)) * _MOE_CHUNK
        pos_rep = jnp.broadcast_to(pos, (tm, LANES))
        pos_ref[...] = pos_rep
        pos_t = jnp.transpose(pos_rep)[0:1, :]
        h_ref[...] = hh
        c_hi, c_lo = _split2(comb)
        for c in range(n_sorted // _MOE_ROWS):
            rows = slice(c * _MOE_ROWS, (c + 1) * _MOE_ROWS)
            slot = (lax.broadcasted_iota(jnp.int32, (_MOE_ROWS, tm), 0) + c * _MOE_ROWS).astype(F32)
            perm = jnp.where(slot == pos_t, 1.0, 0.0).astype(BF16)
            hs_ref[rows, :] = _dot(perm, h_ref[...]).astype(BF16)
            cws_ref[rows, :] = _dot(perm, c_hi) + _dot(perm, c_lo)
        ys_ref[...] = jnp.zeros_like(ys_ref)
        cb_ref[0] = 0

    lane_c = lax.broadcasted_iota(jnp.int32, (_MOE_CHUNK, LANES), 1)
    n_chunks = jnp.max(jnp.where(lane1 == e, nck_ref[0:1, :], 0.0))

    def chunk(state):
        cb, cf = state
        rows = pl.ds(pl.multiple_of(cb * _MOE_CHUNK, _MOE_CHUNK), _MOE_CHUNK)
        hc = hs_ref[rows, :]
        cwc = cws_ref[rows, :]
        acts = []
        for k in range(EXPERTS_PER_GROUP):
            gate = _dot(hc, wg_ref[0, k])
            up = _dot(hc, wu_ref[0, k])
            cw = jnp.sum(jnp.where(lane_c == _EXP_LANE0 + e * EXPERTS_PER_GROUP + k, cwc, 0.0),
                         axis=1, keepdims=True)
            acts.append((gate / (1.0 + jnp.exp(-gate)) * up * cw).astype(BF16))
        ys_ref[rows, :] = _dot(jnp.concatenate(acts, axis=1), wd_ref[0]).astype(BF16)
        return cb + 1, cf + 1.0

    cb_end, _ = lax.while_loop(lambda st: st[1] < n_chunks, chunk, (cb_ref[0], jnp.float32(0.0)))
    cb_ref[0] = cb_end

    @pl.when(e == N_GROUPS - 1)
    def _():
        pos_rep2 = jnp.concatenate([pos_ref[...], pos_ref[...]], axis=1)
        for c in range(n_sorted // _MOE_ROWS):
            slot = (lax.broadcasted_iota(jnp.int32, (tm, _MOE_ROWS), 1) + c * _MOE_ROWS).astype(F32)
            perm_t = jnp.where(slot == pos_rep2, 1.0, 0.0).astype(BF16)
            pt_ref[:, c * _MOE_ROWS:(c + 1) * _MOE_ROWS] = perm_t
        y = x_ref[...] + _dot(pt_ref[...], ys_ref[...])
        out_ref[...] = _rms(y, fg_ref[...]) if final_norm else y


def _moe(x2, g, wr_hi, wr_lo, br, wg, wu, wd, final_g, *, final_norm, tm=1024):
    t, d = x2.shape
    f = wg.shape[3]
    n_sorted = tm + N_GROUPS * _MOE_CHUNK
    assert n_sorted % _MOE_ROWS == 0
    ltri = (lax.broadcasted_iota(jnp.int32, (tm, tm), 0) > lax.broadcasted_iota(jnp.int32, (tm, tm), 1)).astype(BF16)
    row = lambda i, e: (i, 0)
    const = lambda i, e: (0, 0)
    return pl.pallas_call(
        functools.partial(_moe_kernel, n_sorted=n_sorted, final_norm=final_norm),
        out_shape=jax.ShapeDtypeStruct((t, d), F32),
        grid=(t // tm, N_GROUPS),
        in_specs=[pl.BlockSpec((tm, d), row),
                  pl.BlockSpec((1, d), const),
                  pl.BlockSpec((d, LANES), const), pl.BlockSpec((d, LANES), const),
                  pl.BlockSpec((1, LANES), const),
                  pl.BlockSpec((tm, tm), const),
                  pl.BlockSpec((1, d), const),
                  pl.BlockSpec((1, EXPERTS_PER_GROUP, d, f), lambda i, e: (e, 0, 0, 0)),
                  pl.BlockSpec((1, EXPERTS_PER_GROUP, d, f), lambda i, e: (e, 0, 0, 0)),
                  pl.BlockSpec((1, EXPERTS_PER_GROUP * f, d), lambda i, e: (e, 0, 0))],
        out_specs=pl.BlockSpec((tm, d), row),
        scratch_shapes=[pltpu.VMEM((tm, d), BF16),
                        pltpu.VMEM((n_sorted, d), BF16),
                        pltpu.VMEM((n_sorted, LANES), F32),
                        pltpu.VMEM((n_sorted, d), BF16),
                        pltpu.VMEM((tm, n_sorted), BF16),
                        pltpu.VMEM((tm, LANES), F32),
                        pltpu.VMEM((8, LANES), F32),
                        pltpu.SMEM((1,), jnp.int32)],
        compiler_params=_params(2),
        name="hier_moe",
    )(x2, g.reshape(1, d), wr_hi, wr_lo, br, ltri, final_g.reshape(1, d), wg, wu, wd)


def _router_weights(w_gr, b_gr, w_er, b_er):
    d = w_gr.shape[0]
    w = jnp.concatenate([w_gr, jnp.moveaxis(w_er, 0, 1).reshape(d, N_EXPERTS)], axis=1)
    w = jnp.pad(w, ((0, 0), (0, LANES - w.shape[1])))
    bias = jnp.pad(jnp.concatenate([b_gr, b_er.reshape(-1)]), (0, LANES - N_GROUPS - N_EXPERTS))
    hi, lo = _split2(w)
    return hi, lo, bias.reshape(1, LANES)


def _layer(x2, b, s, ln1_g, w_in, mix_g, w_out, ln2_g, w_gr, b_gr, w_er, b_er, w_gate, w_up, w_down,
           final_g, final_norm):
    t, d = x2.shape
    qkv = _qkv_proj(x2, ln1_g, w_in.astype(BF16)).reshape(b, s, 3 * d)
    oa = _moba_attention(qkv)
    ob = _sb_attention(qkv)
    oc = _dilated_mixture(qkv)
    x2 = _out_proj(x2, oa.reshape(t, -1), ob.reshape(t, -1), oc.reshape(t, -1), mix_g, w_out.astype(BF16))
    wr_hi, wr_lo, br = _router_weights(w_gr, b_gr, w_er, b_er)
    f = w_gate.shape[-1]
    return _moe(x2, ln2_g, wr_hi, wr_lo, br, w_gate.astype(BF16), w_up.astype(BF16),
                w_down.reshape(N_GROUPS, EXPERTS_PER_GROUP * f, d).astype(BF16), final_g, final_norm=final_norm)


def kernel(x, ln1_g, w_in, mix_norm_g, w_out, ln2_g, w_group_router, b_group_router,
           w_expert_router, b_expert_router, w_gate, w_up, w_down, final_norm_g):
    b, s, d = x.shape
    x2 = x.reshape(b * s, d)
    depth = ln1_g.shape[0]
    for l in range(depth):
        x2 = _layer(x2, b, s, ln1_g[l], w_in[l], mix_norm_g[l], w_out[l], ln2_g[l],
                    w_group_router[l], b_group_router[l], w_expert_router[l], b_expert_router[l],
                    w_gate[l], w_up[l], w_down[l], final_norm_g, final_norm=(l == depth - 1))
    return x2.reshape(b, s, d)
```
